```python
import math
import jax, jax.numpy as jnp
from jax import lax
import numpy as np

D_MODEL = 1024
BATCH = 4
SEQ = 4096
DEPTH = 4
DEC_BATCH = 8
DEC_SEQ = 8192
PAST_LEN = 128

HEAD_DIM = 64
ATTN_GROUPS = ((128, 1), (512, 4), (2048, 16))
ATTN_GROUP_HEADS = 4
ATTN_HEADS = ATTN_GROUP_HEADS * len(ATTN_GROUPS)
ATTN_WIDTH = ATTN_HEADS * HEAD_DIM
ATTN_OUT = ATTN_GROUP_HEADS * HEAD_DIM
ROT_DIM = HEAD_DIM // 4
ROPE_THETA = 500000.0
MASK_VALUE = -1e30
DN_HEADS = 6
DN_HEAD_DIM = 128
DN_WIDTH = DN_HEADS * DN_HEAD_DIM
DN_CONV = 5
DN_CHUNK = 64
SC_WIDTH = 768
SC_CONV = 3
FFN_DIM = 2816
N_BRANCH = 3
NORM_EPS = 1e-6
IN_SIZES = (3 * ATTN_WIDTH, 3 * DN_WIDTH, DN_WIDTH, 2 * DN_HEADS, 2 * DN_HEADS, 3 * SC_WIDTH, N_BRANCH * D_MODEL)
IN_DIM = sum(IN_SIZES)

kernel_name = 'hybrid_dilated_deltanet_shortconv_encoder'


def rms_norm(x, g):
    x32 = x.astype(jnp.float32)
    y = x32 * lax.rsqrt(jnp.mean(x32 * x32, axis=-1, keepdims=True) + NORM_EPS)
    return (y * g.astype(jnp.float32)).astype(x.dtype)


def swiglu(x, w_gate, w_up, w_down):
    return (jax.nn.silu(x @ w_gate) * (x @ w_up)) @ w_down


def conv_centered(x, w):
    K = w.shape[0]
    S = x.shape[1]
    p = K // 2
    xp = jnp.pad(x, ((0, 0), (p, p), (0, 0)))
    out = xp[:, 0:S] * w[0]
    for j in range(1, K):
        out = out + xp[:, j:j + S] * w[j]
    return out


def partial_rotary(x):
    S = x.shape[1]
    half = ROT_DIM // 2
    inv_freq = ROPE_THETA ** (-2.0 * jnp.arange(half, dtype=jnp.float32) / ROT_DIM)
    ang = jnp.arange(S, dtype=jnp.float32)[:, None] * inv_freq[None, :]
    cos = jnp.cos(ang)[None, :, None, :].astype(x.dtype)
    sin = jnp.sin(ang)[None, :, None, :].astype(x.dtype)
    x1 = x[..., :half]
    x2 = x[..., half:ROT_DIM]
    return jnp.concatenate([x1 * cos - x2 * sin, x2 * cos + x1 * sin, x[..., ROT_DIM:]], axis=-1)


def dilated_window_attention(q, k, v, window, dil):
    B, S, H, Dh = q.shape
    half = (window // 2) // dil
    blk = half
    L = -(-S // (dil * blk)) * blk
    S_pad = L * dil
    nb = L // blk

    def to_residues(t):
        t = jnp.pad(t, ((0, 0), (0, S_pad - S), (0, 0), (0, 0)))
        return t.reshape(B, L, dil, H, Dh).transpose(0, 2, 1, 3, 4)

    def band(t):
        tp = jnp.pad(t, ((0, 0), (0, 0), (blk, blk), (0, 0), (0, 0)))
        return jnp.concatenate([tp[:, :, i * blk:i * blk + L].reshape(B, dil, nb, blk, H, Dh) for i in range(3)], axis=3)

    qb = to_residues(q).reshape(B, dil, nb, blk, H, Dh)
    kb = band(to_residues(k))
    vb = band(to_residues(v))
    l_idx = jnp.arange(-blk, L + blk)
    pos = l_idx[None, :] * dil + jnp.arange(dil)[:, None]
    valid = (l_idx[None, :] >= 0) & (pos < S)
    valid_b = jnp.concatenate([valid[:, i * blk:i * blk + L].reshape(dil, nb, blk) for i in range(3)], axis=2)
    rel = jnp.arange(3 * blk)[None, :] - blk - jnp.arange(blk)[:, None]
    mask = (jnp.abs(rel) <= half)[None, None] & valid_b[:, :, None, :]
    s = jnp.einsum('brnqhd,brnkhd->brnhqk', qb, kb).astype(jnp.float32) * (Dh ** -0.5)
    s = jnp.where(mask[None, :, :, None], s, MASK_VALUE)
    lse = jax.nn.logsumexp(s, axis=-1)
    p = jnp.exp(s - lse[..., None]).astype(v.dtype)
    o = jnp.einsum('brnhqk,brnkhd->brnqhd', p, vb)
    o = o.reshape(B, dil, L, H, Dh).transpose(0, 2, 1, 3, 4).reshape(B, S_pad, H, Dh)[:, :S]
    lse = lse.transpose(0, 1, 2, 4, 3).reshape(B, dil, L, H).transpose(0, 2, 1, 3).reshape(B, S_pad, H)[:, :S]
    return o, lse


def attention_mixer(qkv):
    B, S, _ = qkv.shape
    q, k, v = [t.reshape(B, S, ATTN_HEADS, HEAD_DIM) for t in jnp.split(qkv, 3, axis=-1)]
    q = partial_rotary(q)
    k = partial_rotary(k)
    outs, lses = [], []
    for gi, (window, dil) in enumerate(ATTN_GROUPS):
        hs = slice(gi * ATTN_GROUP_HEADS, (gi + 1) * ATTN_GROUP_HEADS)
        o, l = dilated_window_attention(q[:, :, hs], k[:, :, hs], v[:, :, hs], window, dil)
        outs.append(o)
        lses.append(l)
    alpha = jax.nn.softmax(jnp.stack(lses, axis=0), axis=0)
    o = jnp.sum(jnp.stack(outs, axis=0) * alpha[..., None].astype(qkv.dtype), axis=0)
    return o.reshape(B, S, ATTN_OUT)


def l2_normalize(t):
    return t * lax.rsqrt(jnp.sum(t * t, axis=-1, keepdims=True) + NORM_EPS)


def gated_delta_rule_chunked(q, k, v, g, beta):
    B, S, H, Dk = q.shape
    Dv = v.shape[-1]
    C = DN_CHUNK
    N = S // C

    def chunks(t):
        return jnp.swapaxes(t.reshape((B, N, C, H) + t.shape[3:]), 2, 3)

    qc = chunks(q * (Dk ** -0.5))
    kc = chunks(k)
    vc = chunks(v)
    bc = chunks(beta)
    gc = jnp.cumsum(chunks(g), axis=-1)
    tril = jnp.tril(jnp.ones((C, C), dtype=bool))
    strict = jnp.tril(jnp.ones((C, C), dtype=bool), -1)
    diff = gc[..., :, None] - gc[..., None, :]
    decay = jnp.where(tril, jnp.exp(jnp.where(tril, diff, 0.0)), 0.0)
    kbeta = kc * bc[..., None]
    vbeta = vc * bc[..., None]
    a = jnp.where(strict, jnp.einsum('bnhid,bnhjd->bnhij', kbeta, kc) * decay, 0.0) + jnp.eye(C, dtype=jnp.float32)
    rhs = jnp.concatenate([vbeta, kbeta * jnp.exp(gc)[..., None]], axis=-1)
    sol = lax.linalg.triangular_solve(a, rhs, left_side=True, lower=True)
    u = sol[..., :Dv]
    w = sol[..., Dv:]
    intra = jnp.where(tril, jnp.einsum('bnhid,bnhjd->bnhij', qc, kc) * decay, 0.0)
    q_dec = qc * jnp.exp(gc)[..., None]
    g_last = gc[..., -1]
    k_dec = kc * jnp.exp(g_last[..., None] - gc)[..., None]

    def step(state, xs):
        u_n, w_n, q_n, k_n, intra_n, gl_n = xs
        v_new = u_n - jnp.einsum('bhck,bhkv->bhcv', w_n, state)
        o_n = jnp.einsum('bhck,bhkv->bhcv', q_n, state) + jnp.einsum('bhij,bhjv->bhiv', intra_n, v_new)
        state = state * jnp.exp(gl_n)[..., None, None] + jnp.einsum('bhck,bhcv->bhkv', k_n, v_new)
        return state, o_n

    xs = tuple(jnp.moveaxis(t, 1, 0) for t in (u, w, q_dec, k_dec, intra, g_last))
    state0 = jnp.zeros((B, H, Dk, Dv), jnp.float32)
    _, o = lax.scan(step, state0, xs)
    return o.transpose(1, 0, 3, 2, 4).reshape(B, S, H, Dv)


def deltanet_mixer(qkv, z, beta_raw, alpha_raw, conv_w, a_log, dt_bias, norm_g):
    B, S, _ = qkv.shape
    qkv = jax.nn.silu(conv_centered(qkv, conv_w)).astype(jnp.float32)
    q, k, v = [t.reshape(B, S, DN_HEADS, DN_HEAD_DIM) for t in jnp.split(qkv, 3, axis=-1)]
    q = l2_normalize(q)
    k = l2_normalize(k)
    beta = jax.nn.sigmoid(beta_raw.astype(jnp.float32).reshape(B, S, 2, DN_HEADS))
    g = -jnp.exp(a_log.astype(jnp.float32)) * jax.nn.softplus(
        alpha_raw.astype(jnp.float32).reshape(B, S, 2, DN_HEADS) + dt_bias.astype(jnp.float32))
    o_f = gated_delta_rule_chunked(q, k, v, g[:, :, 0], beta[:, :, 0])
    fl = lambda t: jnp.flip(t, axis=1)
    o_b = fl(gated_delta_rule_chunked(fl(q), fl(k), fl(v), fl(g[:, :, 1]), fl(beta[:, :, 1])))
    o = o_f + o_b
    o = o * lax.rsqrt(jnp.mean(o * o, axis=-1, keepdims=True) + NORM_EPS) * norm_g.astype(jnp.float32)
    o = o * jax.nn.silu(z.astype(jnp.float32).reshape(B, S, DN_HEADS, DN_HEAD_DIM))
    return o.reshape(B, S, DN_WIDTH).astype(z.dtype)


def short_conv_mixer(bcx, conv_w):
    b, c, xs = jnp.split(bcx, 3, axis=-1)
    return b * conv_centered(c * xs, conv_w)


def hybrid_mixer(u, w_in, dn_conv, dn_a_log, dn_dt_bias, dn_norm, sc_conv, gate_bias, w_attn_br, w_dn_br, w_sc_br, w_out):
    B, S, _ = u.shape
    proj = u @ w_in
    offs = [0]
    for sz in IN_SIZES:
        offs.append(offs[-1] + sz)
    attn_qkv, dn_qkv, dn_z, dn_beta, dn_alpha, sc_bcx, gate_raw = [proj[..., offs[i]:offs[i + 1]] for i in range(len(IN_SIZES))]
    y_attn = attention_mixer(attn_qkv) @ w_attn_br
    y_dn = deltanet_mixer(dn_qkv, dn_z, dn_beta, dn_alpha, dn_conv, dn_a_log, dn_dt_bias, dn_norm) @ w_dn_br
    y_sc = short_conv_mixer(sc_bcx, sc_conv) @ w_sc_br
    gates = jax.nn.sigmoid(gate_raw.reshape(B, S, N_BRANCH, D_MODEL).astype(jnp.float32)
                           + gate_bias.astype(jnp.float32)).astype(u.dtype)
    merged = gates[:, :, 0] * y_attn + gates[:, :, 1] * y_dn + gates[:, :, 2] * y_sc
    return merged @ w_out


def setup_inputs(seed: int = 0) -> dict:
    key = jax.random.key(seed)
    ks = jax.random.split(key, 24)
    f32 = jnp.float32

    def nrm(k, shape, fan_in):
        return jax.random.normal(k, shape, f32) * (fan_in ** -0.5)

    def gain(k, shape):
        return 1.0 + 0.01 * jax.random.normal(k, shape, f32)

    dt = jnp.exp(jax.random.uniform(ks[11], (DEPTH, 2, DN_HEADS), f32, math.log(1e-3), math.log(1e-1)))
    return {
        'x_prompt': jax.random.normal(ks[0], (BATCH, SEQ, D_MODEL), f32),
        'x_sample': jax.random.normal(ks[1], (DEC_BATCH, DEC_SEQ, D_MODEL), f32),
        'ffn1_norm': gain(ks[2], (DEPTH, D_MODEL)),
        'ffn1_w_gate': nrm(ks[3], (DEPTH, D_MODEL, FFN_DIM), D_MODEL),
        'ffn1_w_up': nrm(ks[4], (DEPTH, D_MODEL, FFN_DIM), D_MODEL),
        'ffn1_w_down': nrm(ks[5], (DEPTH, FFN_DIM, D_MODEL), FFN_DIM),
        'mix_norm': gain(ks[6], (DEPTH, D_MODEL)),
        'w_in': nrm(ks[7], (DEPTH, D_MODEL, IN_DIM), D_MODEL),
        'dn_conv': nrm(ks[8], (DEPTH, DN_CONV, 3 * DN_WIDTH), DN_CONV),
        'dn_a_log': jnp.log(jax.random.uniform(ks[9], (DEPTH, 2, DN_HEADS), f32, 1.0, 16.0)),
        'dn_dt_bias': dt + jnp.log(-jnp.expm1(-dt)),
        'dn_norm': gain(ks[10], (DEPTH, DN_HEAD_DIM)),
        'sc_conv': nrm(ks[12], (DEPTH, SC_CONV, SC_WIDTH), SC_CONV),
        'gate_bias': 0.1 * jax.random.normal(ks[13], (DEPTH, N_BRANCH, D_MODEL), f32),
        'w_attn_br': nrm(ks[14], (DEPTH, ATTN_OUT, D_MODEL), ATTN_OUT),
        'w_dn_br': nrm(ks[15], (DEPTH, DN_WIDTH, D_MODEL), DN_WIDTH),
        'w_sc_br': nrm(ks[16], (DEPTH, SC_WIDTH, D_MODEL), SC_WIDTH),
        'w_out': nrm(ks[17], (DEPTH, D_MODEL, D_MODEL), D_MODEL),
        'ffn2_norm': gain(ks[18], (DEPTH, D_MODEL)),
        'ffn2_w_gate': nrm(ks[19], (DEPTH, D_MODEL, FFN_DIM), D_MODEL),
        'ffn2_w_up': nrm(ks[20], (DEPTH, D_MODEL, FFN_DIM), D_MODEL),
        'ffn2_w_down': nrm(ks[21], (DEPTH, FFN_DIM, D_MODEL), FFN_DIM),
        'final_norm': gain(ks[22], (D_MODEL,)),
    }


def reference(x_prompt, x_sample, ffn1_norm, ffn1_w_gate, ffn1_w_up, ffn1_w_down, mix_norm, w_in, dn_conv,
              dn_a_log, dn_dt_bias, dn_norm, sc_conv, gate_bias, w_attn_br, w_dn_br, w_sc_br, w_out,
              ffn2_norm, ffn2_w_gate, ffn2_w_up, ffn2_w_down, final_norm):
    def trunk(x):
        for l in range(DEPTH):
            x = x + 0.5 * swiglu(rms_norm(x, ffn1_norm[l]), ffn1_w_gate[l], ffn1_w_up[l], ffn1_w_down[l])
            x = x + hybrid_mixer(rms_norm(x, mix_norm[l]), w_in[l], dn_conv[l], dn_a_log[l], dn_dt_bias[l],
                                 dn_norm[l], sc_conv[l], gate_bias[l], w_attn_br[l], w_dn_br[l], w_sc_br[l], w_out[l])
            x = x + 0.5 * swiglu(rms_norm(x, ffn2_norm[l]), ffn2_w_gate[l], ffn2_w_up[l], ffn2_w_down[l])
        return rms_norm(x, final_norm)

    y_prompt = trunk(x_prompt)
    y_sample = trunk(x_sample)
    return (y_prompt, y_sample)
```

```python
import functools
import math

import jax
import jax.numpy as jnp
from jax import lax
from jax.experimental import pallas as pl
from jax.experimental.pallas import tpu as pltpu

F32 = jnp.float32
BF16 = jnp.bfloat16

D_MODEL = 1024
HEAD_DIM = 64
ATTN_GROUPS = ((128, 1), (512, 4), (2048, 16))
ATTN_GROUP_HEADS = 4
ATTN_GW = ATTN_GROUP_HEADS * HEAD_DIM
ATTN_WIDTH = 3 * ATTN_GW
ROT_DIM = HEAD_DIM // 4
ROPE_THETA = 500000.0
MASK_VALUE = -1e30
DN_HEADS = 6
DN_HEAD_DIM = 128
DN_WIDTH = DN_HEADS * DN_HEAD_DIM
DN_CONV = 5
DN_CHUNK = 64
SC_WIDTH = 768
SC_CONV = 3
FFN_DIM = 2816
N_BRANCH = 3
NORM_EPS = 1e-6

COL_GATE = 0
COL_ATTN = COL_GATE + N_BRANCH * D_MODEL
COL_DN = COL_ATTN + 3 * ATTN_WIDTH
COL_SC = COL_DN + 3 * DN_WIDTH
COL_Z = COL_SC + 3 * SC_WIDTH
PROJ_W = COL_Z + DN_WIDTH
PROJ_TN = 768
BA_W = 128
BA_G0 = 2 * DN_HEADS

HALO = 8
HALO_BLK = 16

VMEM_LIMIT = 56 * 1024 * 1024


def _cparams(sem):
    return pltpu.CompilerParams(dimension_semantics=sem, vmem_limit_bytes=VMEM_LIMIT)


def _rms(x, g):
    return x * lax.rsqrt(jnp.mean(x * x, axis=-1, keepdims=True) + NORM_EPS) * g


def _dot(a, b):
    return jnp.dot(a, b, preferred_element_type=F32)


def _dot_nt(a, b):
    return lax.dot_general(a, b, (((1,), (1,)), ((), ())), preferred_element_type=F32)


def _split3(x):
    x1 = x.astype(BF16)
    r = x - x1.astype(F32)
    x2 = r.astype(BF16)
    x3 = (r - x2.astype(F32)).astype(BF16)
    return x1, x2, x3


def _dot_exact_rhs(x, sel):
    x1, x2, x3 = _split3(x)
    return _dot(x1, sel) + _dot(x2, sel) + _dot(x3, sel)


def _dot_exact_lhs(sel, x):
    x1, x2, x3 = _split3(x)
    return _dot(sel, x1) + _dot(sel, x2) + _dot(sel, x3)


def _ffn_kernel(x_ref, g_ref, wg_ref, wu_ref, wd_ref, fg_ref, o_ref, *, tf, final):
    x = x_ref[...]
    xn = _rms(x, g_ref[...]).astype(BF16)
    acc = jnp.zeros_like(x)
    for c in range(FFN_DIM // tf):
        sl = slice(c * tf, (c + 1) * tf)
        hg = _dot(xn, wg_ref[:, sl])
        hu = _dot(xn, wu_ref[:, sl])
        h = (hg * jax.nn.sigmoid(hg) * hu).astype(BF16)
        acc = acc + _dot(h, wd_ref[sl, :])
    y = x + 0.5 * acc
    if final:
        y = _rms(y, fg_ref[...])
    o_ref[...] = y


def _ffn(x, g, wg, wu, wd, final_g=None, *, tm=512, tf=256):
    m, d = x.shape
    final = final_g is not None
    fg = final_g if final else g
    const = lambda i: (0, 0)
    return pl.pallas_call(
        functools.partial(_ffn_kernel, tf=tf, final=final),
        grid=(m // tm,),
        in_specs=[
            pl.BlockSpec((tm, d), lambda i: (i, 0)),
            pl.BlockSpec((1, d), const),
            pl.BlockSpec((d, FFN_DIM), const, pipeline_mode=pl.Buffered(1)),
            pl.BlockSpec((d, FFN_DIM), const, pipeline_mode=pl.Buffered(1)),
            pl.BlockSpec((FFN_DIM, d), const, pipeline_mode=pl.Buffered(1)),
            pl.BlockSpec((1, d), const),
        ],
        out_specs=pl.BlockSpec((tm, d), lambda i: (i, 0)),
        out_shape=jax.ShapeDtypeStruct((m, d), F32),
        compiler_params=_cparams(("parallel",)),
        name="ffn",
    )(x, g, wg, wu, wd, fg)


def _proj_kernel(x_ref, g_ref, w_ref, wba_ref, cos_ref, sin_ref, pj_ref, ba_ref, xn_ref):
    j = pl.program_id(1)

    @pl.when(j == 0)
    def _():
        xn = _rms(x_ref[...], g_ref[...]).astype(BF16)
        xn_ref[...] = xn
        ba_ref[...] = _dot(xn, wba_ref[...])

    y = _dot(xn_ref[...], w_ref[...])
    q_tile = COL_ATTN // PROJ_TN

    @pl.when((j == q_tile) | (j == q_tile + 1))
    def _():
        cos = cos_ref[...]
        sin = sin_ref[...]
        lane = lax.broadcasted_iota(jnp.int32, (1, 128), 1) % HEAD_DIM
        scale = jnp.where(j == q_tile, HEAD_DIM ** -0.5, 1.0).astype(F32)
        half = ROT_DIM // 2
        for s in range(PROJ_TN // 128):
            ys = y[:, s * 128:(s + 1) * 128]
            sw = jnp.where(lane < half, pltpu.roll(ys, 128 - half, 1), pltpu.roll(ys, half, 1))
            pj_ref[:, s * 128:(s + 1) * 128] = ((ys * cos + sw * sin) * scale).astype(BF16)

    @pl.when((j != q_tile) & (j != q_tile + 1))
    def _():
        pj_ref[...] = y.astype(BF16)


def _proj(x, g, w_big, w_ba, cos_t, sin_t, seq, *, tm=512):
    m, d = x.shape
    nseq = seq // tm
    return pl.pallas_call(
        _proj_kernel,
        grid=(m // tm, PROJ_W // PROJ_TN),
        in_specs=[
            pl.BlockSpec((tm, d), lambda i, j: (i, 0)),
            pl.BlockSpec((1, d), lambda i, j: (0, 0)),
            pl.BlockSpec((d, PROJ_TN), lambda i, j: (0, j)),
            pl.BlockSpec((d, BA_W), lambda i, j: (0, 0)),
            pl.BlockSpec((tm, 128), lambda i, j: (i % nseq, 0)),
            pl.BlockSpec((tm, 128), lambda i, j: (i % nseq, 0)),
        ],
        out_specs=[
            pl.BlockSpec((tm, PROJ_TN), lambda i, j: (i, j)),
            pl.BlockSpec((tm, BA_W), lambda i, j: (i, 0)),
        ],
        out_shape=[jax.ShapeDtypeStruct((m, PROJ_W), BF16), jax.ShapeDtypeStruct((m, BA_W), F32)],
        scratch_shapes=[pltpu.VMEM((tm, d), BF16)],
        compiler_params=_cparams(("parallel", "arbitrary")),
        name="proj",
    )(x, g, w_big, w_ba, cos_t, sin_t)


def _rope_tables(seq):
    half = ROT_DIM // 2
    inv_freq = ROPE_THETA ** (-2.0 * jnp.arange(half, dtype=F32) / ROT_DIM)
    ang = jnp.arange(seq, dtype=F32)[:, None] * inv_freq[None, :]
    cos, sin = jnp.cos(ang), jnp.sin(ang)
    ones = jnp.ones((seq, HEAD_DIM - ROT_DIM), F32)
    cos_h = jnp.concatenate([cos, cos, ones], axis=1)
    sin_h = jnp.concatenate([-sin, sin, 0.0 * ones], axis=1)
    return jnp.tile(cos_h, (1, 128 // HEAD_DIM)), jnp.tile(sin_h, (1, 128 // HEAD_DIM))


def _attn_kernel(q_ref, k_ref, v_ref, o_ref, lse_ref, *, tq, tk, half):
    i = pl.program_id(2)
    length = k_ref.shape[1]
    ks = pl.multiple_of(jnp.clip(i * tq - half, 0, length - tk), 64)
    q = q_ref[0]
    kw = k_ref[0, pl.ds(ks, tk), :]
    vw = v_ref[0, pl.ds(ks, tk), :]
    qpos = i * tq + lax.broadcasted_iota(jnp.int32, (tq, 1), 0)
    kpos = ks + lax.broadcasted_iota(jnp.int32, (1, tk), 1)
    mask = jnp.abs(qpos - kpos) <= half
    lane_head = lax.broadcasted_iota(jnp.int32, (1, 128), 1) // HEAD_DIM
    for s in range(ATTN_GW // 128):
        sl = slice(s * 128, (s + 1) * 128)
        qs, kss, vs = q[:, sl], kw[:, sl], vw[:, sl]
        o_slab = None
        lse_slab = None
        for hh in range(128 // HEAD_DIM):
            sel = lane_head == hh
            sc = _dot_nt(jnp.where(sel, qs, jnp.zeros_like(qs)), kss)
            sc = jnp.where(mask, sc, MASK_VALUE)
            mx = jnp.max(sc, axis=-1, keepdims=True)
            p = jnp.exp(sc - mx)
            den = jnp.sum(p, axis=-1, keepdims=True)
            o = _dot(p.astype(BF16), vs) / den
            lse = mx + jnp.log(den)
            o_slab = o if o_slab is None else jnp.where(sel, o, o_slab)
            lse_b = jnp.broadcast_to(lse, o.shape)
            lse_slab = lse_b if lse_slab is None else jnp.where(sel, lse_b, lse_slab)
        o_ref[0, :, sl] = o_slab
        lse_ref[0, :, sl] = lse_slab


def _attn_group(pj, batch, seq, gi):
    window, dil = ATTN_GROUPS[gi]
    half = (window // 2) // dil
    length = seq // dil
    assert seq % (dil * half) == 0
    tq = min(256, length // 2)
    tk = tq + 2 * half
    assert half == 64 and tq % 64 == 0 and tk <= length and length % tq == 0
    nblk = PROJ_W // ATTN_GW
    qb = COL_ATTN // ATTN_GW + gi
    pj3 = pj.reshape(batch, length, dil * PROJ_W)
    o, lse = pl.pallas_call(
        functools.partial(_attn_kernel, tq=tq, tk=tk, half=half),
        grid=(batch, dil, length // tq),
        in_specs=[
            pl.BlockSpec((1, tq, ATTN_GW), lambda b, r, i: (b, i, r * nblk + qb)),
            pl.BlockSpec((1, length, ATTN_GW), lambda b, r, i: (b, 0, r * nblk + qb + 3)),
            pl.BlockSpec((1, length, ATTN_GW), lambda b, r, i: (b, 0, r * nblk + qb + 6)),
        ],
        out_specs=[
            pl.BlockSpec((1, tq, ATTN_GW), lambda b, r, i: (b, i, r)),
            pl.BlockSpec((1, tq, ATTN_GW), lambda b, r, i: (b, i, r)),
        ],
        out_shape=[jax.ShapeDtypeStruct((batch, length, dil * ATTN_GW), F32)] * 2,
        compiler_params=_cparams(("parallel", "parallel", "arbitrary")),
        name=f"attn{gi}",
    )(pj3, pj3, pj3)
    return o.reshape(batch * seq, ATTN_GW), lse.reshape(batch * seq, ATTN_GW)


def _halo_tile(prev_ref, main_ref, next_ref, first, last):
    xm = main_ref[...].astype(F32)
    xp = prev_ref[...].astype(F32)[HALO_BLK - HALO:]
    xn = next_ref[...].astype(F32)[:HALO]
    xp = jnp.where(first, 0.0, xp)
    xn = jnp.where(last, 0.0, xn)
    return jnp.concatenate([xp, xm, xn], axis=0)


def _tap(ext, off, t):
    n = ext.shape[0]
    return pltpu.roll(ext, (-off) % n, 0)[HALO:HALO + t] if off else ext[HALO:HALO + t]


def _seq_edges(i, t, seq):
    p0 = (i * t) % seq
    return p0 == 0, p0 + t == seq


def _dn_prep_kernel(prev_ref, main_ref, next_ref, cw_ref, o_ref, *, t, seq):
    i = pl.program_id(0)
    j = pl.program_id(1)
    first, last = _seq_edges(i, t, seq)
    ext = _halo_tile(prev_ref, main_ref, next_ref, first, last)
    cw = cw_ref[...]
    acc = None
    for tap in range(DN_CONV):
        term = _tap(ext, tap - DN_CONV // 2, t) * cw[tap:tap + 1]
        acc = term if acc is None else acc + term
    y = acc * jax.nn.sigmoid(acc)

    @pl.when(j < 2)
    def _():
        scale = jnp.where(j == 0, DN_HEAD_DIM ** -0.5, 1.0).astype(F32)
        for h in range(DN_HEADS):
            yh = y[:, h * DN_HEAD_DIM:(h + 1) * DN_HEAD_DIM]
            inv = lax.rsqrt(jnp.sum(yh * yh, axis=-1, keepdims=True) + NORM_EPS) * scale
            o_ref[:, h * DN_HEAD_DIM:(h + 1) * DN_HEAD_DIM] = (yh * inv).astype(BF16)

    @pl.when(j == 2)
    def _():
        o_ref[...] = y.astype(BF16)


def _halo_specs(t, m, colblk, width, nj):
    nb = m // HALO_BLK
    r = t // HALO_BLK
    if nj:
        return [
            pl.BlockSpec((HALO_BLK, width), lambda i, j: (jnp.maximum(i * r - 1, 0), colblk + j)),
            pl.BlockSpec((t, width), lambda i, j: (i, colblk + j)),
            pl.BlockSpec((HALO_BLK, width), lambda i, j: (jnp.minimum((i + 1) * r, nb - 1), colblk + j)),
        ]
    return [
        pl.BlockSpec((HALO_BLK, width), lambda i: (jnp.maximum(i * r - 1, 0), colblk)),
        pl.BlockSpec((t, width), lambda i: (i, colblk)),
        pl.BlockSpec((HALO_BLK, width), lambda i: (jnp.minimum((i + 1) * r, nb - 1), colblk)),
    ]


def _dn_prep(pj, dn_conv, seq, *, t=256):
    m = pj.shape[0]
    w = DN_WIDTH
    return pl.pallas_call(
        functools.partial(_dn_prep_kernel, t=t, seq=seq),
        grid=(m // t, 3),
        in_specs=_halo_specs(t, m, COL_DN // w, w, True) + [pl.BlockSpec((DN_CONV, w), lambda i, j: (0, j))],
        out_specs=pl.BlockSpec((t, w), lambda i, j: (i, j)),
        out_shape=jax.ShapeDtypeStruct((m, 3 * w), BF16),
        compiler_params=_cparams(("parallel", "arbitrary")),
        name="dn_prep",
    )(pj, pj, pj, dn_conv)


def _sc_kernel(b_ref, cp_ref, cm_ref, cn_ref, xp_ref, xm_ref, xn_ref, cw_ref, o_ref, *, t, seq):
    i = pl.program_id(0)
    first, last = _seq_edges(i, t, seq)
    ext = _halo_tile(cp_ref, cm_ref, cn_ref, first, last) * _halo_tile(xp_ref, xm_ref, xn_ref, first, last)
    cw = cw_ref[...]
    acc = None
    for tap in range(SC_CONV):
        term = _tap(ext, tap - SC_CONV // 2, t) * cw[tap:tap + 1]
        acc = term if acc is None else acc + term
    o_ref[...] = (b_ref[...].astype(F32) * acc).astype(BF16)


def _sc_mix(pj, sc_conv, seq, *, t=256):
    m = pj.shape[0]
    w = SC_WIDTH
    cb = COL_SC // w
    return pl.pallas_call(
        functools.partial(_sc_kernel, t=t, seq=seq),
        grid=(m // t,),
        in_specs=[pl.BlockSpec((t, w), lambda i: (i, cb))]
        + _halo_specs(t, m, cb + 1, w, False)
        + _halo_specs(t, m, cb + 2, w, False)
        + [pl.BlockSpec((SC_CONV, w), lambda i: (0, 0))],
        out_specs=pl.BlockSpec((t, w), lambda i: (i, 0)),
        out_shape=jax.ShapeDtypeStruct((m, w), BF16),
        compiler_params=_cparams(("parallel",)),
        name="sc_mix",
    )(pj, pj, pj, pj, pj, pj, pj, sc_conv)


def _unit_lower_inverse(lm):
    c = lm.shape[0]
    eye = (lax.broadcasted_iota(jnp.int32, (c, c), 0) == lax.broadcasted_iota(jnp.int32, (c, c), 1)).astype(F32)
    mk = -lm
    p = eye + mk
    for _ in range(int(math.log2(c)) - 1):
        mkb = mk.astype(BF16)
        mk = _dot(mkb, mkb)
        p = p + _dot(p.astype(BF16), mk.astype(BF16))
    return p


def _dn_kernel(q_ref, k_ref, v_ref, ba_ref, alog_ref, dtb_ref, tri_ref, blk_ref, eb_ref, eg_ref, o_ref,
               st_ref, beta_ref, gc_ref, egc_ref, ekd_ref, etot_ref, gct_ref, *, ct, reverse, lane0):
    c = DN_CHUNK
    nchunk = ct // c

    @pl.when(pl.program_id(1) == 0)
    def _():
        st_ref[...] = jnp.zeros_like(st_ref)

    ba = ba_ref[0]
    beta_all = jax.nn.sigmoid(ba)
    xg = ba + dtb_ref[...]
    softplus = jnp.maximum(xg, 0.0) + jnp.log1p(jnp.exp(-jnp.abs(xg)))
    g_all = -jnp.exp(alog_ref[...]) * softplus
    gc_all = _dot_exact_lhs(tri_ref[...], g_all)
    tot_all = _dot_exact_lhs(blk_ref[...], g_all)
    beta_ref[...] = _dot_exact_rhs(beta_all, eb_ref[...])
    gc_x = _dot_exact_rhs(gc_all, eg_ref[...])
    tot_x = _dot_exact_rhs(tot_all, eg_ref[...])
    gc_ref[...] = gc_x
    egc_ref[...] = jnp.exp(gc_x)
    ekd_ref[...] = jnp.exp(tot_x - gc_x)
    etot_ref[...] = jnp.exp(tot_x)
    for n in range(nchunk):
        gct_ref[n] = gc_all[n * c:(n + 1) * c, :].T

    row = lax.broadcasted_iota(jnp.int32, (c, c), 0)
    col = lax.broadcasted_iota(jnp.int32, (c, c), 1)
    incl = (col >= row) if reverse else (col <= row)
    strict = (col > row) if reverse else (col < row)

    def chunk_body(it, carry):
        n = (nchunk - 1 - it) if reverse else it
        r0 = pl.multiple_of(n * c, c)
        rows = pl.ds(r0, c)
        for h in range(DN_HEADS):
            hs = slice(h * DN_HEAD_DIM, (h + 1) * DN_HEAD_DIM)
            kb16 = k_ref[0, rows, hs]
            k = kb16.astype(F32)
            q = q_ref[0, rows, hs].astype(F32)
            v = v_ref[0, rows, hs].astype(F32)
            beta = beta_ref[rows, hs]
            eg = egc_ref[rows, hs]
            gcol = gc_ref[rows, h * DN_HEAD_DIM:h * DN_HEAD_DIM + c]
            grow = gct_ref[n, lane0 + h:lane0 + h + 1, :]
            diff = gcol - grow
            decay = jnp.where(incl, jnp.exp(jnp.where(incl, diff, 0.0)), 0.0)
            kb = k * beta
            kq = _dot_nt(jnp.concatenate([kb, q], axis=0).astype(BF16), kb16)
            lm = jnp.where(strict, kq[:c] * decay, 0.0)
            intra = jnp.where(incl, kq[c:] * decay, 0.0)
            tinv = _unit_lower_inverse(lm)
            rhs = jnp.concatenate([v * beta, kb * eg], axis=1)
            sol = _dot(tinv.astype(BF16), rhs.astype(BF16))
            u = sol[:, :DN_HEAD_DIM]
            w = sol[:, DN_HEAD_DIM:]
            state = st_ref[h]
            wq = _dot(jnp.concatenate([w, q * eg], axis=0).astype(BF16), state.astype(BF16))
            v_new = u - wq[:c]
            kdec_t = (k * ekd_ref[rows, hs]).T
            upd = _dot(jnp.concatenate([intra, kdec_t], axis=0).astype(BF16), v_new.astype(BF16))
            o_ref[0, rows, hs] = wq[c:] + upd[:c]
            st_ref[h] = state * etot_ref[pl.ds(r0, 1), hs] + upd[c:]
        return carry

    lax.fori_loop(0, nchunk, chunk_body, 0)


def _dn_consts(ct, reverse, direction):
    c = DN_CHUNK
    r = jnp.arange(ct)
    same = (r[:, None] // c) == (r[None, :] // c)
    tri = same & ((r[None, :] >= r[:, None]) if reverse else (r[None, :] <= r[:, None]))
    lanes = jnp.arange(BA_W)[:, None]
    heads = (jnp.arange(DN_WIDTH) // DN_HEAD_DIM)[None, :]
    eb = lanes == direction * DN_HEADS + heads
    eg = lanes == BA_G0 + direction * DN_HEADS + heads
    return tri.astype(BF16), same.astype(BF16), eb.astype(BF16), eg.astype(BF16)


def _deltanet(qkv, ba, alog_row, dtb_row, batch, seq, direction, *, ct=256):
    reverse = direction == 1
    nstep = seq // ct
    w = DN_WIDTH
    tri, blk, eb, eg = _dn_consts(ct, reverse, direction)
    qkv3 = qkv.reshape(batch, seq, 3 * w)
    ba3 = ba.reshape(batch, seq, BA_W)
    step = (lambda n: nstep - 1 - n) if reverse else (lambda n: n)
    const = lambda b, n: (0, 0)
    out = pl.pallas_call(
        functools.partial(_dn_kernel, ct=ct, reverse=reverse, lane0=BA_G0 + direction * DN_HEADS),
        grid=(batch, nstep),
        in_specs=[
            pl.BlockSpec((1, ct, w), lambda b, n: (b, step(n), 0)),
            pl.BlockSpec((1, ct, w), lambda b, n: (b, step(n), 1)),
            pl.BlockSpec((1, ct, w), lambda b, n: (b, step(n), 2)),
            pl.BlockSpec((1, ct, BA_W), lambda b, n: (b, step(n), 0)),
            pl.BlockSpec((1, BA_W), const),
            pl.BlockSpec((1, BA_W), const),
            pl.BlockSpec((ct, ct), const),
            pl.BlockSpec((ct, ct), const),
            pl.BlockSpec((BA_W, w), const),
            pl.BlockSpec((BA_W, w), const),
        ],
        out_specs=pl.BlockSpec((1, ct, w), lambda b, n: (b, step(n), 0)),
        out_shape=jax.ShapeDtypeStruct((batch, seq, w), F32),
        scratch_shapes=[
            pltpu.VMEM((DN_HEADS, DN_HEAD_DIM, DN_HEAD_DIM), F32),
            pltpu.VMEM((ct, w), F32),
            pltpu.VMEM((ct, w), F32),
            pltpu.VMEM((ct, w), F32),
            pltpu.VMEM((ct, w), F32),
            pltpu.VMEM((ct, w), F32),
            pltpu.VMEM((ct // DN_CHUNK, BA_W, DN_CHUNK), F32),
        ],
        compiler_params=_cparams(("parallel", "arbitrary")),
        name=f"deltanet{direction}",
    )(qkv3, qkv3, qkv3, ba3, alog_row, dtb_row, tri, blk, eb, eg)
    return out.reshape(batch * seq, w)


def _merge_kernel(x_ref, o1_ref, o2_ref, o3_ref, l1_ref, l2_ref, l3_ref, df_ref, db_ref, z_ref, sc_ref, gate_ref,
                  gb_ref, dng_ref, wa_ref, wd_ref, ws_ref, wo_ref, out_ref):
    l1, l2, l3 = l1_ref[...], l2_ref[...], l3_ref[...]
    mx = jnp.maximum(jnp.maximum(l1, l2), l3)
    e1, e2, e3 = jnp.exp(l1 - mx), jnp.exp(l2 - mx), jnp.exp(l3 - mx)
    attn = (o1_ref[...] * e1 + o2_ref[...] * e2 + o3_ref[...] * e3) / (e1 + e2 + e3)
    y_attn = _dot(attn.astype(BF16), wa_ref[...])
    dn = df_ref[...] + db_ref[...]
    dng = dng_ref[...]
    parts = []
    for h in range(DN_HEADS):
        hs = slice(h * DN_HEAD_DIM, (h + 1) * DN_HEAD_DIM)
        oh = dn[:, hs]
        oh = oh * lax.rsqrt(jnp.mean(oh * oh, axis=-1, keepdims=True) + NORM_EPS) * dng
        zh = z_ref[:, hs].astype(F32)
        parts.append((oh * (zh * jax.nn.sigmoid(zh))).astype(BF16))
    y_dn = _dot(jnp.concatenate(parts, axis=1), wd_ref[...])
    y_sc = _dot(sc_ref[...], ws_ref[...])
    gates = jax.nn.sigmoid(gate_ref[...].astype(F32) + gb_ref[...])
    d = D_MODEL
    merged = gates[:, :d] * y_attn + gates[:, d:2 * d] * y_dn + gates[:, 2 * d:] * y_sc
    out_ref[...] = x_ref[...] + _dot(merged.astype(BF16), wo_ref[...])


def _merge(x, attn_o, attn_l, dn_f, dn_b, pj, sc, gate_bias, dn_norm, wa, wd, ws, wo, *, tm=256):
    m, d = x.shape
    row = lambda i: (i, 0)
    const = lambda i: (0, 0)
    gw = N_BRANCH * d
    return pl.pallas_call(
        _merge_kernel,
        grid=(m // tm,),
        in_specs=[pl.BlockSpec((tm, d), row)]
        + [pl.BlockSpec((tm, ATTN_GW), row)] * 6
        + [pl.BlockSpec((tm, DN_WIDTH), row)] * 2
        + [
            pl.BlockSpec((tm, DN_WIDTH), lambda i: (i, COL_Z // DN_WIDTH)),
            pl.BlockSpec((tm, SC_WIDTH), row),
            pl.BlockSpec((tm, gw), lambda i: (i, COL_GATE // gw)),
            pl.BlockSpec((1, gw), const),
            pl.BlockSpec((1, DN_HEAD_DIM), const),
            pl.BlockSpec((ATTN_GW, d), const),
            pl.BlockSpec((DN_WIDTH, d), const),
            pl.BlockSpec((SC_WIDTH, d), const),
            pl.BlockSpec((d, d), const),
        ],
        out_specs=pl.BlockSpec((tm, d), row),
        out_shape=jax.ShapeDtypeStruct((m, d), F32),
        compiler_params=_cparams(("parallel",)),
        name="merge",
    )(x, *attn_o, *attn_l, dn_f, dn_b, pj, sc, pj, gate_bias, dn_norm, wa, wd, ws, wo)


def _prep_layer(l, p):
    w_in = p["w_in"][l]
    o_attn, o_dn = 0, 3 * ATTN_WIDTH
    o_z = o_dn + 3 * DN_WIDTH
    o_beta = o_z + DN_WIDTH
    o_sc = o_beta + 4 * DN_HEADS
    o_gate = o_sc + 3 * SC_WIDTH
    w_big = jnp.concatenate(
        [w_in[:, o_gate:], w_in[:, o_attn:o_dn], w_in[:, o_dn:o_z], w_in[:, o_sc:o_gate], w_in[:, o_z:o_beta]], axis=1)
    w_ba = jnp.pad(w_in[:, o_beta:o_sc], ((0, 0), (0, BA_W - 4 * DN_HEADS)))
    pad_row = lambda t: jnp.pad(t.reshape(1, 2 * DN_HEADS), ((0, 0), (BA_G0, BA_W - BA_G0 - 2 * DN_HEADS)))
    bf = lambda t: t.astype(BF16)
    row = lambda t: t.reshape(1, -1)
    return dict(
        ffn1=(row(p["ffn1_norm"][l]), bf(p["ffn1_w_gate"][l]), bf(p["ffn1_w_up"][l]), bf(p["ffn1_w_down"][l])),
        ffn2=(row(p["ffn2_norm"][l]), bf(p["ffn2_w_gate"][l]), bf(p["ffn2_w_up"][l]), bf(p["ffn2_w_down"][l])),
        mix_norm=row(p["mix_norm"][l]),
        w_big=bf(w_big),
        w_ba=bf(w_ba),
        dn_conv=p["dn_conv"][l],
        alog=pad_row(p["dn_a_log"][l]),
        dtb=pad_row(p["dn_dt_bias"][l]),
        dn_norm=row(p["dn_norm"][l]),
        sc_conv=p["sc_conv"][l],
        gate_bias=row(p["gate_bias"][l]),
        wa=bf(p["w_attn_br"][l]),
        wd=bf(p["w_dn_br"][l]),
        ws=bf(p["w_sc_br"][l]),
        wo=bf(p["w_out"][l]),
    )


def _mixer(x, lp, batch, seq, tables):
    pj, ba = _proj(x, lp["mix_norm"], lp["w_big"], lp["w_ba"], tables[0], tables[1], seq)
    attn = [_attn_group(pj, batch, seq, gi) for gi in range(len(ATTN_GROUPS))]
    qkv = _dn_prep(pj, lp["dn_conv"], seq)
    dn_f = _deltanet(qkv, ba, lp["alog"], lp["dtb"], batch, seq, 0)
    dn_b = _deltanet(qkv, ba, lp["alog"], lp["dtb"], batch, seq, 1)
    sc = _sc_mix(pj, lp["sc_conv"], seq)
    return _merge(x, [a[0] for a in attn], [a[1] for a in attn], dn_f, dn_b, pj, sc, lp["gate_bias"],
                  lp["dn_norm"], lp["wa"], lp["wd"], lp["ws"], lp["wo"])


def _trunk(x, layers, final_g):
    batch, seq, d = x.shape
    tables = _rope_tables(seq)
    x = x.reshape(batch * seq, d)
    for l, lp in enumerate(layers):
        x = _ffn(x, *lp["ffn1"])
        x = _mixer(x, lp, batch, seq, tables)
        x = _ffn(x, *lp["ffn2"], final_g=final_g if l == len(layers) - 1 else None)
    return x.reshape(batch, seq, d)


def kernel(x_prompt, x_sample, ffn1_norm, ffn1_w_gate, ffn1_w_up, ffn1_w_down, mix_norm, w_in, dn_conv, dn_a_log, dn_dt_bias, dn_norm, sc_conv, gate_bias, w_attn_br, w_dn_br, w_sc_br, w_out, ffn2_norm, ffn2_w_gate, ffn2_w_up, ffn2_w_down, final_norm):
    p = dict(ffn1_norm=ffn1_norm, ffn1_w_gate=ffn1_w_gate, ffn1_w_up=ffn1_w_up, ffn1_w_down=ffn1_w_down,
             mix_norm=mix_norm, w_in=w_in, dn_conv=dn_conv, dn_a_log=dn_a_log, dn_dt_bias=dn_dt_bias,
             dn_norm=dn_norm, sc_conv=sc_conv, gate_bias=gate_bias, w_attn_br=w_attn_br, w_dn_br=w_dn_br,
             w_sc_br=w_sc_br, w_out=w_out, ffn2_norm=ffn2_norm, ffn2_w_gate=ffn2_w_gate, ffn2_w_up=ffn2_w_up,
             ffn2_w_down=ffn2_w_down)
    layers = [_prep_layer(l, p) for l in range(ffn1_norm.shape[0])]
    final_g = final_norm.reshape(1, -1)
    return _trunk(x_prompt, layers, final_g), _trunk(x_sample, layers, final_g)
```

```python
import functools
import math

import jax
import jax.numpy as jnp
from jax import lax
from jax.experimental import pallas as pl
from jax.experimental.pallas import tpu as pltpu

F32 = jnp.float32
BF16 = jnp.bfloat16

D_MODEL = 1024
HEAD_DIM = 64
ATTN_GROUPS = ((128, 1), (512, 4), (2048, 16))
ATTN_GROUP_HEADS = 4
ATTN_GW = ATTN_GROUP_HEADS * HEAD_DIM
ATTN_WIDTH = 3 * ATTN_GW
ROT_DIM = HEAD_DIM // 4
ROPE_THETA = 500000.0
MASK_VALUE = -1e30
DN_HEADS = 6
DN_HEAD_DIM = 128
DN_WIDTH = DN_HEADS * DN_HEAD_DIM
DN_CONV = 5
DN_CHUNK = 64
SC_WIDTH = 768
SC_CONV = 3
FFN_DIM = 2816
N_BRANCH = 3
NORM_EPS = 1e-6

PROJ_TN = 768
ATTN_TILES = 3 * ATTN_WIDTH // PROJ_TN
PJ_GATE = 0
PJ_DN = PJ_GATE + N_BRANCH * D_MODEL
PJ_SC = PJ_DN + 3 * DN_WIDTH
PJ_Z = PJ_SC + 3 * SC_WIDTH
PJ_W = PJ_Z + DN_WIDTH
PROJ_TILES = ATTN_TILES + PJ_W // PROJ_TN
BA_W = 128
BA_G0 = 2 * DN_HEADS

HALO = 8
HALO_BLK = 16

VMEM_LIMIT = 56 * 1024 * 1024


def _cparams(sem):
    return pltpu.CompilerParams(dimension_semantics=sem, vmem_limit_bytes=VMEM_LIMIT)


def _resident(shape):
    return pl.BlockSpec(shape, lambda *_: (0,) * len(shape), pipeline_mode=pl.Buffered(1))


def _rms(x, g):
    return x * lax.rsqrt(jnp.mean(x * x, axis=-1, keepdims=True) + NORM_EPS) * g


def _dot(a, b):
    return jnp.dot(a, b, preferred_element_type=F32)


def _dot_nt(a, b):
    return lax.dot_general(a, b, (((1,), (1,)), ((), ())), preferred_element_type=F32)


def _dot_exact_lhs(sel, x):
    x1 = x.astype(BF16)
    r = x - x1.astype(F32)
    x2 = r.astype(BF16)
    x3 = (r - x2.astype(F32)).astype(BF16)
    return _dot(sel, x1) + _dot(sel, x2) + _dot(sel, x3)


def _ffn_kernel(x_ref, g_ref, wg_ref, wu_ref, wd_ref, fg_ref, o_ref, *, tf, final):
    x = x_ref[...]
    xn = _rms(x, g_ref[...]).astype(BF16)
    acc = jnp.zeros_like(x)
    for c in range(FFN_DIM // tf):
        sl = slice(c * tf, (c + 1) * tf)
        hg = _dot(xn, wg_ref[:, sl])
        hu = _dot(xn, wu_ref[:, sl])
        h = (hg * jax.nn.sigmoid(hg) * hu).astype(BF16)
        acc = acc + _dot(h, wd_ref[sl, :])
    y = x + 0.5 * acc
    if final:
        y = _rms(y, fg_ref[...])
    o_ref[...] = y


def _ffn(x, g, wg, wu, wd, final_g=None, *, tm=512, tf=256):
    m, d = x.shape
    final = final_g is not None
    fg = final_g if final else g
    return pl.pallas_call(
        functools.partial(_ffn_kernel, tf=tf, final=final),
        grid=(m // tm,),
        in_specs=[
            pl.BlockSpec((tm, d), lambda i: (i, 0)),
            _resident((1, d)),
            _resident((d, FFN_DIM)),
            _resident((d, FFN_DIM)),
            _resident((FFN_DIM, d)),
            _resident((1, d)),
        ],
        out_specs=pl.BlockSpec((tm, d), lambda i: (i, 0)),
        out_shape=jax.ShapeDtypeStruct((m, d), F32),
        compiler_params=_cparams(("parallel",)),
        name="ffn",
    )(x, g, wg, wu, wd, fg)


def _proj_kernel(x_ref, g_ref, w_ref, wba_ref, cos_ref, sin_ref, at_ref, pj_ref, ba_ref, xn_ref):
    j = pl.program_id(1)

    @pl.when(j == 0)
    def _():
        xn = _rms(x_ref[...], g_ref[...]).astype(BF16)
        xn_ref[...] = xn
        ba_ref[...] = _dot(xn, wba_ref[...])

    y = _dot(xn_ref[...], w_ref[j])

    @pl.when(j < 2)
    def _():
        cos = cos_ref[...]
        sin = sin_ref[...]
        lane = lax.broadcasted_iota(jnp.int32, (1, 128), 1) % HEAD_DIM
        scale = jnp.where(j == 0, HEAD_DIM ** -0.5, 1.0).astype(F32)
        half = ROT_DIM // 2
        for s in range(PROJ_TN // 128):
            ys = y[:, s * 128:(s + 1) * 128]
            sw = jnp.where(lane < half, pltpu.roll(ys, 128 - half, 1), pltpu.roll(ys, half, 1))
            at_ref[:, s * 128:(s + 1) * 128] = (ys * cos + sw * sin) * scale

    @pl.when(j == 2)
    def _():
        at_ref[...] = y

    @pl.when(j >= ATTN_TILES)
    def _():
        pj_ref[...] = y.astype(BF16)


def _proj(x, g, w_tiles, w_ba, cos_t, sin_t, seq, *, tm=512):
    m, d = x.shape
    nseq = seq // tm
    return pl.pallas_call(
        _proj_kernel,
        grid=(m // tm, PROJ_TILES),
        in_specs=[
            pl.BlockSpec((tm, d), lambda i, j: (i, 0)),
            _resident((1, d)),
            _resident((PROJ_TILES, d, PROJ_TN)),
            _resident((d, BA_W)),
            pl.BlockSpec((tm, 128), lambda i, j: (i % nseq, 0)),
            pl.BlockSpec((tm, 128), lambda i, j: (i % nseq, 0)),
        ],
        out_specs=[
            pl.BlockSpec((tm, PROJ_TN), lambda i, j: (i, jnp.minimum(j, ATTN_TILES - 1))),
            pl.BlockSpec((tm, PROJ_TN), lambda i, j: (i, jnp.maximum(j - ATTN_TILES, 0))),
            pl.BlockSpec((tm, BA_W), lambda i, j: (i, 0)),
        ],
        out_shape=[
            jax.ShapeDtypeStruct((m, 3 * ATTN_WIDTH), F32),
            jax.ShapeDtypeStruct((m, PJ_W), BF16),
            jax.ShapeDtypeStruct((m, BA_W), F32),
        ],
        scratch_shapes=[pltpu.VMEM((tm, d), BF16)],
        compiler_params=_cparams(("parallel", "arbitrary")),
        name="proj",
    )(x, g, w_tiles, w_ba, cos_t, sin_t)


def _rope_tables(seq):
    half = ROT_DIM // 2
    inv_freq = ROPE_THETA ** (-2.0 * jnp.arange(half, dtype=F32) / ROT_DIM)
    ang = jnp.arange(seq, dtype=F32)[:, None] * inv_freq[None, :]
    cos, sin = jnp.cos(ang), jnp.sin(ang)
    ones = jnp.ones((seq, HEAD_DIM - ROT_DIM), F32)
    cos_h = jnp.concatenate([cos, cos, ones], axis=1)
    sin_h = jnp.concatenate([-sin, sin, 0.0 * ones], axis=1)
    return jnp.tile(cos_h, (1, 128 // HEAD_DIM)), jnp.tile(sin_h, (1, 128 // HEAD_DIM))


def _attn_kernel(q_ref, k_ref, v_ref, o_ref, lse_ref, kres_ref, vres_ref, *, dil, tq, tk, half, length):
    i = pl.program_id(2)

    @pl.when(i == 0)
    def _():
        for r in range(dil):
            rows = pl.ds(r, length, stride=dil) if dil > 1 else pl.ds(0, length)
            kres_ref[r * length:(r + 1) * length, :] = k_ref[0, rows, :].astype(BF16)
            vres_ref[r * length:(r + 1) * length, :] = v_ref[0, rows, :].astype(BF16)

    ks = pl.multiple_of(jnp.clip(i * tq - half, 0, length - tk), 64)
    qpos = i * tq + lax.broadcasted_iota(jnp.int32, (tq, 1), 0)
    kpos = ks + lax.broadcasted_iota(jnp.int32, (1, tk), 1)
    mask = jnp.abs(qpos - kpos) <= half
    lane_head = lax.broadcasted_iota(jnp.int32, (1, 128), 1) // HEAD_DIM

    def residue(r, carry):
        rows = pl.ds(r, tq, stride=dil) if dil > 1 else pl.ds(0, tq)
        q = q_ref[0, rows, :].astype(BF16)
        kbase = pl.multiple_of(r * length + ks, 64)
        kw = kres_ref[pl.ds(kbase, tk), :]
        vw = vres_ref[pl.ds(kbase, tk), :]
        o_slab = None
        lse_slab = None
        for hh in range(128 // HEAD_DIM):
            sel = lane_head == hh
            sc = _dot_nt(jnp.where(sel, q, jnp.zeros_like(q)), kw)
            sc = jnp.where(mask, sc, MASK_VALUE)
            mx = jnp.max(sc, axis=-1, keepdims=True)
            p = jnp.exp(sc - mx)
            den = jnp.sum(p, axis=-1, keepdims=True)
            o = _dot(p.astype(BF16), vw) / den
            lse_b = jnp.broadcast_to(mx + jnp.log(den), o.shape)
            o_slab = o if o_slab is None else jnp.where(sel, o, o_slab)
            lse_slab = lse_b if lse_slab is None else jnp.where(sel, lse_b, lse_slab)
        o_ref[0, rows, :] = o_slab
        lse_ref[0, rows, :] = lse_slab
        return carry

    lax.fori_loop(0, dil, residue, 0)


def _attn_group(at, batch, seq, gi):
    window, dil = ATTN_GROUPS[gi]
    half = (window // 2) // dil
    length = seq // dil
    tq = min(256, length // 2)
    tk = tq + 2 * half
    assert half == 64 and seq % dil == 0 and tq % 64 == 0 and tk <= length and length % tq == 0
    tb = tq * dil
    at3 = at.reshape(batch, seq, 3 * ATTN_WIDTH)
    nslab = ATTN_GW // 128
    nb = ATTN_WIDTH // 128
    o, lse = pl.pallas_call(
        functools.partial(_attn_kernel, dil=dil, tq=tq, tk=tk, half=half, length=length),
        grid=(batch, nslab, seq // tb),
        in_specs=[
            pl.BlockSpec((1, tb, 128), lambda b, s, i: (b, i, gi * nslab + s)),
            pl.BlockSpec((1, seq, 128), lambda b, s, i: (b, 0, nb + gi * nslab + s), pipeline_mode=pl.Buffered(1)),
            pl.BlockSpec((1, seq, 128), lambda b, s, i: (b, 0, 2 * nb + gi * nslab + s), pipeline_mode=pl.Buffered(1)),
        ],
        out_specs=[
            pl.BlockSpec((1, tb, 128), lambda b, s, i: (b, i, s)),
            pl.BlockSpec((1, tb, 128), lambda b, s, i: (b, i, s)),
        ],
        out_shape=[jax.ShapeDtypeStruct((batch, seq, ATTN_GW), F32)] * 2,
        scratch_shapes=[pltpu.VMEM((seq, 128), BF16), pltpu.VMEM((seq, 128), BF16)],
        compiler_params=_cparams(("parallel", "parallel", "arbitrary")),
        name=f"attn{gi}",
    )(at3, at3, at3)
    return o.reshape(batch * seq, ATTN_GW), lse.reshape(batch * seq, ATTN_GW)


def _halo_tile(prev_ref, main_ref, next_ref, first, last):
    xm = main_ref[...].astype(F32)
    xp = prev_ref[...].astype(F32)[HALO_BLK - HALO:]
    xn = next_ref[...].astype(F32)[:HALO]
    xp = jnp.where(first, 0.0, xp)
    xn = jnp.where(last, 0.0, xn)
    return jnp.concatenate([xp, xm, xn], axis=0)


def _tap(ext, off, t):
    n = ext.shape[0]
    return pltpu.roll(ext, (-off) % n, 0)[HALO:HALO + t] if off else ext[HALO:HALO + t]


def _seq_edges(i, t, seq):
    p0 = (i * t) % seq
    return p0 == 0, p0 + t == seq


def _dn_prep_kernel(prev_ref, main_ref, next_ref, cw_ref, o_ref, *, t, seq):
    i = pl.program_id(0)
    j = pl.program_id(1)
    first, last = _seq_edges(i, t, seq)
    ext = _halo_tile(prev_ref, main_ref, next_ref, first, last)
    cw = cw_ref[...]
    acc = None
    for tap in range(DN_CONV):
        term = _tap(ext, tap - DN_CONV // 2, t) * cw[tap:tap + 1]
        acc = term if acc is None else acc + term
    y = acc * jax.nn.sigmoid(acc)

    @pl.when(j < 2)
    def _():
        scale = jnp.where(j == 0, DN_HEAD_DIM ** -0.5, 1.0).astype(F32)
        for h in range(DN_HEADS):
            yh = y[:, h * DN_HEAD_DIM:(h + 1) * DN_HEAD_DIM]
            inv = lax.rsqrt(jnp.sum(yh * yh, axis=-1, keepdims=True) + NORM_EPS) * scale
            o_ref[:, h * DN_HEAD_DIM:(h + 1) * DN_HEAD_DIM] = (yh * inv).astype(BF16)

    @pl.when(j == 2)
    def _():
        o_ref[...] = y.astype(BF16)


def _halo_specs(t, m, colblk, width, nj):
    nb = m // HALO_BLK
    r = t // HALO_BLK
    if nj:
        return [
            pl.BlockSpec((HALO_BLK, width), lambda i, j: (jnp.maximum(i * r - 1, 0), colblk + j)),
            pl.BlockSpec((t, width), lambda i, j: (i, colblk + j)),
            pl.BlockSpec((HALO_BLK, width), lambda i, j: (jnp.minimum((i + 1) * r, nb - 1), colblk + j)),
        ]
    return [
        pl.BlockSpec((HALO_BLK, width), lambda i: (jnp.maximum(i * r - 1, 0), colblk)),
        pl.BlockSpec((t, width), lambda i: (i, colblk)),
        pl.BlockSpec((HALO_BLK, width), lambda i: (jnp.minimum((i + 1) * r, nb - 1), colblk)),
    ]


def _dn_prep(pj, dn_conv, seq, *, t=256):
    m = pj.shape[0]
    w = DN_WIDTH
    return pl.pallas_call(
        functools.partial(_dn_prep_kernel, t=t, seq=seq),
        grid=(m // t, 3),
        in_specs=_halo_specs(t, m, PJ_DN // w, w, True) + [pl.BlockSpec((DN_CONV, w), lambda i, j: (0, j))],
        out_specs=pl.BlockSpec((t, w), lambda i, j: (i, j)),
        out_shape=jax.ShapeDtypeStruct((m, 3 * w), BF16),
        compiler_params=_cparams(("parallel", "arbitrary")),
        name="dn_prep",
    )(pj, pj, pj, dn_conv)


def _sc_kernel(b_ref, cp_ref, cm_ref, cn_ref, xp_ref, xm_ref, xn_ref, cw_ref, o_ref, *, t, seq):
    i = pl.program_id(0)
    first, last = _seq_edges(i, t, seq)
    ext = _halo_tile(cp_ref, cm_ref, cn_ref, first, last) * _halo_tile(xp_ref, xm_ref, xn_ref, first, last)
    cw = cw_ref[...]
    acc = None
    for tap in range(SC_CONV):
        term = _tap(ext, tap - SC_CONV // 2, t) * cw[tap:tap + 1]
        acc = term if acc is None else acc + term
    o_ref[...] = (b_ref[...].astype(F32) * acc).astype(BF16)


def _sc_mix(pj, sc_conv, seq, *, t=256):
    m = pj.shape[0]
    w = SC_WIDTH
    cb = PJ_SC // w
    return pl.pallas_call(
        functools.partial(_sc_kernel, t=t, seq=seq),
        grid=(m // t,),
        in_specs=[pl.BlockSpec((t, w), lambda i: (i, cb))]
        + _halo_specs(t, m, cb + 1, w, False)
        + _halo_specs(t, m, cb + 2, w, False)
        + [pl.BlockSpec((SC_CONV, w), lambda i: (0, 0))],
        out_specs=pl.BlockSpec((t, w), lambda i: (i, 0)),
        out_shape=jax.ShapeDtypeStruct((m, w), BF16),
        compiler_params=_cparams(("parallel",)),
        name="sc_mix",
    )(pj, pj, pj, pj, pj, pj, pj, sc_conv)


def _dn_kernel(q_ref, k_ref, v_ref, ba_ref, alog_ref, dtb_ref, tri_ref, o_ref,
               st_ref, beta_ref, gc_ref, gct_ref, u_ref, wq_ref, upd_ref, *, ct, reverse, lane0):
    c = DN_CHUNK
    nchunk = ct // c
    heads = range(DN_HEADS)
    hsl = [slice(h * DN_HEAD_DIM, (h + 1) * DN_HEAD_DIM) for h in heads]

    @pl.when(pl.program_id(1) == 0)
    def _():
        st_ref[...] = jnp.zeros_like(st_ref)

    ba = ba_ref[0]
    beta_all = jax.nn.sigmoid(ba)
    xg = ba + dtb_ref[...]
    softplus = jnp.maximum(xg, 0.0) + jnp.log1p(jnp.exp(-jnp.abs(xg)))
    g_all = -jnp.exp(alog_ref[...]) * softplus
    gc_all = _dot_exact_lhs(tri_ref[...], g_all)
    for h in heads:
        lb = lane0 - BA_G0 + h
        beta_ref[:, hsl[h]] = jnp.broadcast_to(beta_all[:, lb:lb + 1], (ct, DN_HEAD_DIM))
        gc_ref[:, hsl[h]] = jnp.broadcast_to(gc_all[:, lane0 + h:lane0 + h + 1], (ct, DN_HEAD_DIM))
    for n in range(nchunk):
        gct_ref[n] = gc_all[n * c:(n + 1) * c, :].T

    row = lax.broadcasted_iota(jnp.int32, (c, c), 0)
    col = lax.broadcasted_iota(jnp.int32, (c, c), 1)
    incl = (col >= row) if reverse else (col <= row)
    strict = (col > row) if reverse else (col < row)
    eye = (col == row).astype(F32)

    def local(n, slot):
        rows = slice(n * c, (n + 1) * c)
        last = n * c + (0 if reverse else c - 1)
        k16 = [k_ref[0, rows, hsl[h]] for h in heads]
        q = [q_ref[0, rows, hsl[h]].astype(F32) for h in heads]
        v = [v_ref[0, rows, hsl[h]].astype(F32) for h in heads]
        beta = [beta_ref[rows, hsl[h]] for h in heads]
        gcx = [gc_ref[rows, hsl[h]] for h in heads]
        gtot = [gc_ref[last:last + 1, hsl[h]] for h in heads]
        grow = [gct_ref[n, lane0 + h:lane0 + h + 1, :] for h in heads]
        k = [t.astype(F32) for t in k16]
        kb = [k[h] * beta[h] for h in heads]
        kq = [_dot_nt(jnp.concatenate([kb[h], q[h]], axis=0).astype(BF16), k16[h]) for h in heads]
        decay = [jnp.where(incl, jnp.exp(jnp.where(incl, gcx[h][:, :c] - grow[h], 0.0)), 0.0) for h in heads]
        intra = [jnp.where(incl, kq[h][c:] * decay[h], 0.0) for h in heads]
        mk = [jnp.where(strict, -kq[h][:c] * decay[h], 0.0) for h in heads]
        tinv = [eye + mk[h] for h in heads]
        for _ in range(int(math.log2(c)) - 1):
            mkb = [t.astype(BF16) for t in mk]
            mk = [_dot(mkb[h], mkb[h]) for h in heads]
            tinv = [tinv[h] + _dot(tinv[h].astype(BF16), mk[h].astype(BF16)) for h in heads]
        eg = [jnp.exp(gcx[h]) for h in heads]
        rhs = [jnp.concatenate([v[h] * beta[h], kb[h] * eg[h]], axis=1).astype(BF16) for h in heads]
        sol = [_dot(tinv[h].astype(BF16), rhs[h]) for h in heads]
        kdec_t = [(k[h].T * jnp.exp(gtot[h][:, :c] - grow[h])).astype(BF16) for h in heads]
        for h in heads:
            u_ref[slot, h] = sol[h][:, :DN_HEAD_DIM]
            wq_ref[slot, h, :c] = sol[h][:, DN_HEAD_DIM:].astype(BF16)
            wq_ref[slot, h, c:] = (q[h] * eg[h]).astype(BF16)
            upd_ref[slot, h, :c] = intra[h].astype(BF16)
            upd_ref[slot, h, c:] = kdec_t[h]

    def recur(n, slot):
        rows = slice(n * c, (n + 1) * c)
        last = n * c + (0 if reverse else c - 1)
        state = [st_ref[h] for h in heads]
        etot = [jnp.exp(gc_ref[last:last + 1, hsl[h]]) for h in heads]
        wq = [_dot(wq_ref[slot, h], state[h].astype(BF16)) for h in heads]
        v_new = [(u_ref[slot, h] - wq[h][:c]).astype(BF16) for h in heads]
        upd = [_dot(upd_ref[slot, h], v_new[h]) for h in heads]
        for h in heads:
            o_ref[0, rows, hsl[h]] = wq[h][c:] + upd[h][:c]
            st_ref[h] = state[h] * etot[h] + upd[h][c:]

    order = list(range(nchunk))[::-1] if reverse else list(range(nchunk))
    local(order[0], 0)
    for it, n in enumerate(order):
        if it + 1 < nchunk:
            local(order[it + 1], (it + 1) % 2)
        recur(n, it % 2)


def _dn_tri(ct, reverse):
    c = DN_CHUNK
    r = jnp.arange(ct)
    same = (r[:, None] // c) == (r[None, :] // c)
    tri = same & ((r[None, :] >= r[:, None]) if reverse else (r[None, :] <= r[:, None]))
    return tri.astype(BF16)


def _deltanet(qkv, ba, alog_row, dtb_row, batch, seq, direction, *, ct=256):
    reverse = direction == 1
    nstep = seq // ct
    w = DN_WIDTH
    c = DN_CHUNK
    qkv3 = qkv.reshape(batch, seq, 3 * w)
    ba3 = ba.reshape(batch, seq, BA_W)
    step = (lambda n: nstep - 1 - n) if reverse else (lambda n: n)
    out = pl.pallas_call(
        functools.partial(_dn_kernel, ct=ct, reverse=reverse, lane0=BA_G0 + direction * DN_HEADS),
        grid=(batch, nstep),
        in_specs=[
            pl.BlockSpec((1, ct, w), lambda b, n: (b, step(n), 0)),
            pl.BlockSpec((1, ct, w), lambda b, n: (b, step(n), 1)),
            pl.BlockSpec((1, ct, w), lambda b, n: (b, step(n), 2)),
            pl.BlockSpec((1, ct, BA_W), lambda b, n: (b, step(n), 0)),
            _resident((1, BA_W)),
            _resident((1, BA_W)),
            _resident((ct, ct)),
        ],
        out_specs=pl.BlockSpec((1, ct, w), lambda b, n: (b, step(n), 0)),
        out_shape=jax.ShapeDtypeStruct((batch, seq, w), F32),
        scratch_shapes=[
            pltpu.VMEM((DN_HEADS, DN_HEAD_DIM, DN_HEAD_DIM), F32),
            pltpu.VMEM((ct, w), F32),
            pltpu.VMEM((ct, w), F32),
            pltpu.VMEM((ct // c, BA_W, c), F32),
            pltpu.VMEM((2, DN_HEADS, c, DN_HEAD_DIM), F32),
            pltpu.VMEM((2, DN_HEADS, 2 * c, DN_HEAD_DIM), BF16),
            pltpu.VMEM((2, DN_HEADS, c + DN_HEAD_DIM, c), BF16),
        ],
        compiler_params=_cparams(("parallel", "arbitrary")),
        name=f"deltanet{direction}",
    )(qkv3, qkv3, qkv3, ba3, alog_row, dtb_row, _dn_tri(ct, reverse))
    return out.reshape(batch * seq, w)


def _merge_kernel(x_ref, o1_ref, o2_ref, o3_ref, l1_ref, l2_ref, l3_ref, df_ref, db_ref, z_ref, sc_ref, gate_ref,
                  gb_ref, dng_ref, wa_ref, wd_ref, ws_ref, wo_ref, out_ref):
    l1, l2, l3 = l1_ref[...], l2_ref[...], l3_ref[...]
    mx = jnp.maximum(jnp.maximum(l1, l2), l3)
    e1, e2, e3 = jnp.exp(l1 - mx), jnp.exp(l2 - mx), jnp.exp(l3 - mx)
    attn = (o1_ref[...] * e1 + o2_ref[...] * e2 + o3_ref[...] * e3) / (e1 + e2 + e3)
    y_attn = _dot(attn.astype(BF16), wa_ref[...])
    dn = df_ref[...] + db_ref[...]
    dng = dng_ref[...]
    parts = []
    for h in range(DN_HEADS):
        hs = slice(h * DN_HEAD_DIM, (h + 1) * DN_HEAD_DIM)
        oh = dn[:, hs]
        oh = oh * lax.rsqrt(jnp.mean(oh * oh, axis=-1, keepdims=True) + NORM_EPS) * dng
        zh = z_ref[:, hs].astype(F32)
        parts.append((oh * (zh * jax.nn.sigmoid(zh))).astype(BF16))
    y_dn = _dot(jnp.concatenate(parts, axis=1), wd_ref[...])
    y_sc = _dot(sc_ref[...], ws_ref[...])
    gates = jax.nn.sigmoid(gate_ref[...].astype(F32) + gb_ref[...])
    d = D_MODEL
    merged = gates[:, :d] * y_attn + gates[:, d:2 * d] * y_dn + gates[:, 2 * d:] * y_sc
    out_ref[...] = x_ref[...] + _dot(merged.astype(BF16), wo_ref[...])


def _merge(x, attn_o, attn_l, dn_f, dn_b, pj, sc, gate_bias, dn_norm, wa, wd, ws, wo, *, tm=512):
    m, d = x.shape
    row = lambda i: (i, 0)
    gw = N_BRANCH * d
    return pl.pallas_call(
        _merge_kernel,
        grid=(m // tm,),
        in_specs=[pl.BlockSpec((tm, d), row)]
        + [pl.BlockSpec((tm, ATTN_GW), row)] * 6
        + [pl.BlockSpec((tm, DN_WIDTH), row)] * 2
        + [
            pl.BlockSpec((tm, DN_WIDTH), lambda i: (i, PJ_Z // DN_WIDTH)),
            pl.BlockSpec((tm, SC_WIDTH), row),
            pl.BlockSpec((tm, gw), lambda i: (i, PJ_GATE // gw)),
            _resident((1, gw)),
            _resident((1, DN_HEAD_DIM)),
            _resident((ATTN_GW, d)),
            _resident((DN_WIDTH, d)),
            _resident((SC_WIDTH, d)),
            _resident((d, d)),
        ],
        out_specs=pl.BlockSpec((tm, d), row),
        out_shape=jax.ShapeDtypeStruct((m, d), F32),
        compiler_params=_cparams(("parallel",)),
        name="merge",
    )(x, *attn_o, *attn_l, dn_f, dn_b, pj, sc, pj, gate_bias, dn_norm, wa, wd, ws, wo)


def _prep_layer(l, p):
    w_in = p["w_in"][l]
    o_attn, o_dn = 0, 3 * ATTN_WIDTH
    o_z = o_dn + 3 * DN_WIDTH
    o_beta = o_z + DN_WIDTH
    o_sc = o_beta + 4 * DN_HEADS
    o_gate = o_sc + 3 * SC_WIDTH
    w_big = jnp.concatenate(
        [w_in[:, o_attn:o_dn], w_in[:, o_gate:], w_in[:, o_dn:o_z], w_in[:, o_sc:o_gate], w_in[:, o_z:o_beta]], axis=1)
    w_tiles = w_big.astype(BF16).reshape(D_MODEL, PROJ_TILES, PROJ_TN).transpose(1, 0, 2)
    w_ba = jnp.pad(w_in[:, o_beta:o_sc], ((0, 0), (0, BA_W - 4 * DN_HEADS)))
    pad_row = lambda t: jnp.pad(t.reshape(1, 2 * DN_HEADS), ((0, 0), (BA_G0, BA_W - BA_G0 - 2 * DN_HEADS)))
    bf = lambda t: t.astype(BF16)
    row = lambda t: t.reshape(1, -1)
    return dict(
        ffn1=(row(p["ffn1_norm"][l]), bf(p["ffn1_w_gate"][l]), bf(p["ffn1_w_up"][l]), bf(p["ffn1_w_down"][l])),
        ffn2=(row(p["ffn2_norm"][l]), bf(p["ffn2_w_gate"][l]), bf(p["ffn2_w_up"][l]), bf(p["ffn2_w_down"][l])),
        mix_norm=row(p["mix_norm"][l]),
        w_tiles=w_tiles,
        w_ba=bf(w_ba),
        dn_conv=p["dn_conv"][l],
        alog=pad_row(p["dn_a_log"][l]),
        dtb=pad_row(p["dn_dt_bias"][l]),
        dn_norm=row(p["dn_norm"][l]),
        sc_conv=p["sc_conv"][l],
        gate_bias=row(p["gate_bias"][l]),
        wa=bf(p["w_attn_br"][l]),
        wd=bf(p["w_dn_br"][l]),
        ws=bf(p["w_sc_br"][l]),
        wo=bf(p["w_out"][l]),
    )


def _mixer(x, lp, batch, seq, tables):
    at, pj, ba = _proj(x, lp["mix_norm"], lp["w_tiles"], lp["w_ba"], tables[0], tables[1], seq)
    attn = [_attn_group(at, batch, seq, gi) for gi in range(len(ATTN_GROUPS))]
    qkv = _dn_prep(pj, lp["dn_conv"], seq)
    dn_f = _deltanet(qkv, ba, lp["alog"], lp["dtb"], batch, seq, 0)
    dn_b = _deltanet(qkv, ba, lp["alog"], lp["dtb"], batch, seq, 1)
    sc = _sc_mix(pj, lp["sc_conv"], seq)
    return _merge(x, [a[0] for a in attn], [a[1] for a in attn], dn_f, dn_b, pj, sc, lp["gate_bias"],
                  lp["dn_norm"], lp["wa"], lp["wd"], lp["ws"], lp["wo"])


def _trunk(x, layers, final_g):
    batch, seq, d = x.shape
    tables = _rope_tables(seq)
    x = x.reshape(batch * seq, d)
    for l, lp in enumerate(layers):
        x = _ffn(x, *lp["ffn1"])
        x = _mixer(x, lp, batch, seq, tables)
        x = _ffn(x, *lp["ffn2"], final_g=final_g if l == len(layers) - 1 else None)
    return x.reshape(batch, seq, d)


def kernel(x_prompt, x_sample, ffn1_norm, ffn1_w_gate, ffn1_w_up, ffn1_w_down, mix_norm, w_in, dn_conv, dn_a_log, dn_dt_bias, dn_norm, sc_conv, gate_bias, w_attn_br, w_dn_br, w_sc_br, w_out, ffn2_norm, ffn2_w_gate, ffn2_w_up, ffn2_w_down, final_norm):
    p = dict(ffn1_norm=ffn1_norm, ffn1_w_gate=ffn1_w_gate, ffn1_w_up=ffn1_w_up, ffn1_w_down=ffn1_w_down,
             mix_norm=mix_norm, w_in=w_in, dn_conv=dn_conv, dn_a_log=dn_a_log, dn_dt_bias=dn_dt_bias,
             dn_norm=dn_norm, sc_conv=sc_conv, gate_bias=gate_bias, w_attn_br=w_attn_br, w_dn_br=w_dn_br,
             w_sc_br=w_sc_br, w_out=w_out, ffn2_norm=ffn2_norm, ffn2_w_gate=ffn2_w_gate, ffn2_w_up=ffn2_w_up,
             ffn2_w_down=ffn2_w_down)
    layers = [_prep_layer(l, p) for l in range(ffn1_norm.shape[0])]
    final_g = final_norm.reshape(1, -1)
    return _trunk(x_prompt, layers, final_g), _trunk(x_sample, layers, final_g)
```

```python
import functools
import math

import jax
import jax.numpy as jnp
from jax import lax
from jax.experimental import pallas as pl
from jax.experimental.pallas import tpu as pltpu

F32 = jnp.float32
BF16 = jnp.bfloat16

D_MODEL = 1024
HEAD_DIM = 64
ATTN_GROUPS = ((128, 1), (512, 4), (2048, 16))
ATTN_GROUP_HEADS = 4
ATTN_GW = ATTN_GROUP_HEADS * HEAD_DIM
ATTN_WIDTH = 3 * ATTN_GW
ROT_DIM = HEAD_DIM // 4
ROPE_THETA = 500000.0
MASK_VALUE = -1e30
DN_HEADS = 6
DN_HEAD_DIM = 128
DN_WIDTH = DN_HEADS * DN_HEAD_DIM
DN_CONV = 5
DN_CHUNK = 64
SC_WIDTH = 768
SC_CONV = 3
FFN_DIM = 2816
N_BRANCH = 3
NORM_EPS = 1e-6

PROJ_TN = 768
ATTN_TILES = 3 * ATTN_WIDTH // PROJ_TN
PJ_DN = 0
PJ_Z = PJ_DN + 3 * DN_WIDTH
PJ_GATE = PJ_Z + DN_WIDTH
PJ_SC = PJ_GATE + N_BRANCH * D_MODEL
PJ_W = PJ_SC + 3 * SC_WIDTH
PJ_TILES = PJ_W // PROJ_TN
BA_W = 128
BA_G0 = 2 * DN_HEADS

DN_AHEAD = 2
ATTN_UNITS = 4
HALO = 8
HALO_BLK = 16

VMEM_LIMIT = 56 * 1024 * 1024


def _cparams(sem):
    return pltpu.CompilerParams(dimension_semantics=sem, vmem_limit_bytes=VMEM_LIMIT)


def _resident(shape):
    return pl.BlockSpec(shape, lambda *_: (0,) * len(shape), pipeline_mode=pl.Buffered(1))


def _rms(x, g):
    return x * lax.rsqrt(jnp.mean(x * x, axis=-1, keepdims=True) + NORM_EPS) * g


def _dot(a, b):
    return jnp.dot(a, b, preferred_element_type=F32)


def _dot_nt(a, b):
    return lax.dot_general(a, b, (((1,), (1,)), ((), ())), preferred_element_type=F32)


def _dot_exact_lhs(sel, x):
    x1 = x.astype(BF16)
    r = x - x1.astype(F32)
    x2 = r.astype(BF16)
    x3 = (r - x2.astype(F32)).astype(BF16)
    return _dot(sel, x1) + _dot(sel, x2) + _dot(sel, x3)


def _ffn_kernel(x_ref, g_ref, wg_ref, wu_ref, wd_ref, fg_ref, o_ref, *, tf, final):
    x = x_ref[...]
    xn = _rms(x, g_ref[...]).astype(BF16)
    acc = jnp.zeros_like(x)
    for c in range(FFN_DIM // tf):
        sl = slice(c * tf, (c + 1) * tf)
        hg = _dot(xn, wg_ref[:, sl])
        hu = _dot(xn, wu_ref[:, sl])
        h = (hg * jax.nn.sigmoid(hg) * hu).astype(BF16)
        acc = acc + _dot(h, wd_ref[sl, :])
    y = x + 0.5 * acc
    if final:
        y = _rms(y, fg_ref[...])
    o_ref[...] = y


def _ffn(x, g, wg, wu, wd, final_g=None, *, tm=512, tf=256):
    m, d = x.shape
    final = final_g is not None
    fg = final_g if final else g
    return pl.pallas_call(
        functools.partial(_ffn_kernel, tf=tf, final=final),
        grid=(m // tm,),
        in_specs=[
            pl.BlockSpec((tm, d), lambda i: (i, 0)),
            _resident((1, d)),
            _resident((d, FFN_DIM)),
            _resident((d, FFN_DIM)),
            _resident((FFN_DIM, d)),
            _resident((1, d)),
        ],
        out_specs=pl.BlockSpec((tm, d), lambda i: (i, 0)),
        out_shape=jax.ShapeDtypeStruct((m, d), F32),
        compiler_params=_cparams(("parallel",)),
        name="ffn",
    )(x, g, wg, wu, wd, fg)


def _attn_proj_kernel(x_ref, g_ref, w_ref, cos_ref, sin_ref, at_ref, xn_ref):
    j = pl.program_id(1)

    @pl.when(j == 0)
    def _():
        xn_ref[...] = _rms(x_ref[...], g_ref[...]).astype(BF16)

    y = _dot(xn_ref[...], w_ref[j])
    cos = cos_ref[0]
    sin = sin_ref[0]
    lane = lax.broadcasted_iota(jnp.int32, (1, 128), 1) % HEAD_DIM
    half = ROT_DIM // 2
    for s in range(PROJ_TN // 128):
        ys = y[:, s * 128:(s + 1) * 128]
        sw = jnp.where(lane < half, pltpu.roll(ys, 128 - half, 1), pltpu.roll(ys, half, 1))
        at_ref[:, s * 128:(s + 1) * 128] = ys * cos + sw * sin


def _attn_proj(x, g, w_tiles, cos_t, sin_t, seq, *, tm=1024):
    m, d = x.shape
    nseq = seq // tm
    return pl.pallas_call(
        _attn_proj_kernel,
        grid=(m // tm, ATTN_TILES),
        in_specs=[
            pl.BlockSpec((tm, d), lambda i, j: (i, 0)),
            _resident((1, d)),
            _resident((ATTN_TILES, d, PROJ_TN)),
            pl.BlockSpec((1, tm, 128), lambda i, j: (j, i % nseq, 0)),
            pl.BlockSpec((1, tm, 128), lambda i, j: (j, i % nseq, 0)),
        ],
        out_specs=pl.BlockSpec((tm, PROJ_TN), lambda i, j: (i, j)),
        out_shape=jax.ShapeDtypeStruct((m, 3 * ATTN_WIDTH), F32),
        scratch_shapes=[pltpu.VMEM((tm, d), BF16)],
        compiler_params=_cparams(("parallel", "arbitrary")),
        name="attn_proj",
    )(x, g, w_tiles, cos_t, sin_t)


def _proj_kernel(x_ref, g_ref, w_ref, wba_ref, pj_ref, ba_ref, xn_ref):
    j = pl.program_id(1)

    @pl.when(j == 0)
    def _():
        xn = _rms(x_ref[...], g_ref[...]).astype(BF16)
        xn_ref[...] = xn
        ba_ref[...] = _dot(xn, wba_ref[...])

    pj_ref[...] = _dot(xn_ref[...], w_ref[j]).astype(BF16)


def _proj(x, g, w_tiles, w_ba, *, tm=1024):
    m, d = x.shape
    return pl.pallas_call(
        _proj_kernel,
        grid=(m // tm, PJ_TILES),
        in_specs=[
            pl.BlockSpec((tm, d), lambda i, j: (i, 0)),
            _resident((1, d)),
            _resident((PJ_TILES, d, PROJ_TN)),
            _resident((d, BA_W)),
        ],
        out_specs=[
            pl.BlockSpec((tm, PROJ_TN), lambda i, j: (i, j)),
            pl.BlockSpec((tm, BA_W), lambda i, j: (i, 0)),
        ],
        out_shape=[jax.ShapeDtypeStruct((m, PJ_W), BF16), jax.ShapeDtypeStruct((m, BA_W), F32)],
        scratch_shapes=[pltpu.VMEM((tm, d), BF16)],
        compiler_params=_cparams(("parallel", "arbitrary")),
        name="proj",
    )(x, g, w_tiles, w_ba)


def _rope_tables(seq):
    half = ROT_DIM // 2
    inv_freq = ROPE_THETA ** (-2.0 * jnp.arange(half, dtype=F32) / ROT_DIM)
    ang = jnp.arange(seq, dtype=F32)[:, None] * inv_freq[None, :]
    cos, sin = jnp.cos(ang), jnp.sin(ang)
    ones = jnp.ones((seq, HEAD_DIM - ROT_DIM), F32)
    cos_h = jnp.tile(jnp.concatenate([cos, cos, ones], axis=1), (1, 128 // HEAD_DIM))
    sin_h = jnp.tile(jnp.concatenate([-sin, sin, 0.0 * ones], axis=1), (1, 128 // HEAD_DIM))
    qs = HEAD_DIM ** -0.5
    cos_t = jnp.stack([cos_h * qs, cos_h, jnp.ones_like(cos_h)])
    sin_t = jnp.stack([sin_h * qs, sin_h, jnp.zeros_like(sin_h)])
    return cos_t, sin_t


def _attn_kernel(q_ref, k_ref, v_ref, o_ref, lse_ref, kres_ref, vres_ref, *, dil, tq, tk, half, length, nsub, unroll):
    i = pl.program_id(2)

    @pl.when(i == 0)
    def _():
        for r in range(dil):
            rows = pl.ds(r, length, stride=dil) if dil > 1 else pl.ds(0, length)
            kres_ref[r * length:(r + 1) * length, :] = k_ref[0, rows, :].astype(BF16)
            vres_ref[r * length:(r + 1) * length, :] = v_ref[0, rows, :].astype(BF16)

    lane_head = lax.broadcasted_iota(jnp.int32, (1, 128), 1) // HEAD_DIM
    qiota = lax.broadcasted_iota(jnp.int32, (tq, 1), 0)
    kiota = lax.broadcasted_iota(jnp.int32, (1, tk), 1)

    def unit(u, carry):
        r = u % dil
        sub = u // dil
        blk = i * nsub + sub
        ks = pl.multiple_of(jnp.clip(blk * tq - half, 0, length - tk), 64)
        mask = jnp.abs(blk * tq + qiota - ks - kiota) <= half
        if dil > 1:
            rows = pl.ds(r + sub * (tq * dil), tq, stride=dil)
        else:
            rows = pl.ds(pl.multiple_of(sub * tq, tq), tq)
        q = q_ref[0, rows, :].astype(BF16)
        kbase = pl.multiple_of(r * length + ks, 64)
        kw = kres_ref[pl.ds(kbase, tk), :]
        vw = vres_ref[pl.ds(kbase, tk), :]
        o_slab = None
        lse_slab = None
        for hh in range(128 // HEAD_DIM):
            sel = lane_head == hh
            sc = _dot_nt(jnp.where(sel, q, jnp.zeros_like(q)), kw)
            sc = jnp.where(mask, sc, MASK_VALUE)
            mx = jnp.max(sc, axis=-1, keepdims=True)
            p = jnp.exp(sc - mx)
            den = jnp.sum(p, axis=-1, keepdims=True)
            o = _dot(p.astype(BF16), vw) / den
            lse_b = jnp.broadcast_to(mx + jnp.log(den), o.shape)
            o_slab = o if o_slab is None else jnp.where(sel, o, o_slab)
            lse_slab = lse_b if lse_slab is None else jnp.where(sel, lse_b, lse_slab)
        o_ref[0, rows, :] = o_slab
        lse_ref[0, rows, :] = lse_slab
        return carry

    lax.fori_loop(0, dil * nsub, unit, 0, unroll=unroll)


def _attn_group(at, batch, seq, gi):
    window, dil = ATTN_GROUPS[gi]
    half = (window // 2) // dil
    length = seq // dil
    tq = min(256, length // 2)
    tk = tq + 2 * half
    assert half == 64 and seq % dil == 0 and tq % 64 == 0 and tk <= length and length % tq == 0
    nsub = max(1, ATTN_UNITS // dil)
    assert length % (tq * nsub) == 0
    tb = tq * dil * nsub
    at3 = at.reshape(batch, seq, 3 * ATTN_WIDTH)
    nslab = ATTN_GW // 128
    nb = ATTN_WIDTH // 128
    o, lse = pl.pallas_call(
        functools.partial(_attn_kernel, dil=dil, tq=tq, tk=tk, half=half, length=length, nsub=nsub,
                          unroll=ATTN_UNITS),
        grid=(batch, nslab, seq // tb),
        in_specs=[
            pl.BlockSpec((1, tb, 128), lambda b, s, i: (b, i, gi * nslab + s)),
            pl.BlockSpec((1, seq, 128), lambda b, s, i: (b, 0, nb + gi * nslab + s), pipeline_mode=pl.Buffered(1)),
            pl.BlockSpec((1, seq, 128), lambda b, s, i: (b, 0, 2 * nb + gi * nslab + s), pipeline_mode=pl.Buffered(1)),
        ],
        out_specs=[
            pl.BlockSpec((1, tb, 128), lambda b, s, i: (b, i, s)),
            pl.BlockSpec((1, tb, 128), lambda b, s, i: (b, i, s)),
        ],
        out_shape=[jax.ShapeDtypeStruct((batch, seq, ATTN_GW), F32)] * 2,
        scratch_shapes=[pltpu.VMEM((seq, 128), BF16), pltpu.VMEM((seq, 128), BF16)],
        compiler_params=_cparams(("parallel", "parallel", "arbitrary")),
        name=f"attn{gi}",
    )(at3, at3, at3)
    return o.reshape(batch * seq, ATTN_GW), lse.reshape(batch * seq, ATTN_GW)


def _halo_tile(prev_ref, main_ref, next_ref, first, last, sl):
    xm = main_ref[:, sl].astype(F32)
    xp = prev_ref[:, sl].astype(F32)[HALO_BLK - HALO:]
    xn = next_ref[:, sl].astype(F32)[:HALO]
    xp = jnp.where(first, 0.0, xp)
    xn = jnp.where(last, 0.0, xn)
    return jnp.concatenate([xp, xm, xn], axis=0)


def _tap(ext, off, t):
    n = ext.shape[0]
    return pltpu.roll(ext, (-off) % n, 0)[HALO:HALO + t] if off else ext[HALO:HALO + t]


def _seq_edges(i, t, seq):
    p0 = (i * t) % seq
    return p0 == 0, p0 + t == seq


def _dn_prep_kernel(prev_ref, main_ref, next_ref, cw_ref, o_ref, *, t, seq):
    first, last = _seq_edges(pl.program_id(0), t, seq)
    for s in range(3 * DN_HEADS):
        sl = slice(s * DN_HEAD_DIM, (s + 1) * DN_HEAD_DIM)
        ext = _halo_tile(prev_ref, main_ref, next_ref, first, last, sl)
        acc = None
        for tap in range(DN_CONV):
            term = _tap(ext, tap - DN_CONV // 2, t) * cw_ref[tap:tap + 1, sl]
            acc = term if acc is None else acc + term
        y = acc * jax.nn.sigmoid(acc)
        if s < 2 * DN_HEADS:
            scale = DN_HEAD_DIM ** -0.5 if s < DN_HEADS else 1.0
            y = y * (lax.rsqrt(jnp.sum(y * y, axis=-1, keepdims=True) + NORM_EPS) * scale)
        o_ref[:, sl] = y.astype(BF16)


def _halo_specs(t, m, colblk, width):
    nb = m // HALO_BLK
    r = t // HALO_BLK
    return [
        pl.BlockSpec((HALO_BLK, width), lambda i: (jnp.maximum(i * r - 1, 0), colblk)),
        pl.BlockSpec((t, width), lambda i: (i, colblk)),
        pl.BlockSpec((HALO_BLK, width), lambda i: (jnp.minimum((i + 1) * r, nb - 1), colblk)),
    ]


def _dn_prep(pj, dn_conv, seq, *, t=512):
    m = pj.shape[0]
    w = 3 * DN_WIDTH
    return pl.pallas_call(
        functools.partial(_dn_prep_kernel, t=t, seq=seq),
        grid=(m // t,),
        in_specs=_halo_specs(t, m, PJ_DN // w, w) + [_resident((DN_CONV, w))],
        out_specs=pl.BlockSpec((t, w), lambda i: (i, 0)),
        out_shape=jax.ShapeDtypeStruct((m, w), BF16),
        compiler_params=_cparams(("parallel",)),
        name="dn_prep",
    )(pj, pj, pj, dn_conv)


def _sc_kernel(b_ref, cp_ref, cm_ref, cn_ref, xp_ref, xm_ref, xn_ref, cw_ref, o_ref, *, t, seq):
    first, last = _seq_edges(pl.program_id(0), t, seq)
    for s in range(SC_WIDTH // 128):
        sl = slice(s * 128, (s + 1) * 128)
        ext = (_halo_tile(cp_ref, cm_ref, cn_ref, first, last, sl)
               * _halo_tile(xp_ref, xm_ref, xn_ref, first, last, sl))
        acc = None
        for tap in range(SC_CONV):
            term = _tap(ext, tap - SC_CONV // 2, t) * cw_ref[tap:tap + 1, sl]
            acc = term if acc is None else acc + term
        o_ref[:, sl] = (b_ref[:, sl].astype(F32) * acc).astype(BF16)


def _sc_mix(pj, sc_conv, seq, *, t=512):
    m = pj.shape[0]
    w = SC_WIDTH
    cb = PJ_SC // w
    return pl.pallas_call(
        functools.partial(_sc_kernel, t=t, seq=seq),
        grid=(m // t,),
        in_specs=[pl.BlockSpec((t, w), lambda i: (i, cb))]
        + _halo_specs(t, m, cb + 1, w)
        + _halo_specs(t, m, cb + 2, w)
        + [_resident((SC_CONV, w))],
        out_specs=pl.BlockSpec((t, w), lambda i: (i, 0)),
        out_shape=jax.ShapeDtypeStruct((m, w), BF16),
        compiler_params=_cparams(("parallel",)),
        name="sc_mix",
    )(pj, pj, pj, pj, pj, pj, pj, sc_conv)


def _dn_kernel(q_ref, k_ref, v_ref, ba_ref, alog_ref, dtb_ref, tri_ref, o_ref,
               st_ref, beta_ref, gc_ref, gct_ref, u_ref, wq_ref, upd_ref, *, ct, reverse, lane0):
    c = DN_CHUNK
    nchunk = ct // c
    heads = range(DN_HEADS)
    hsl = [slice(h * DN_HEAD_DIM, (h + 1) * DN_HEAD_DIM) for h in heads]

    @pl.when(pl.program_id(1) == 0)
    def _():
        st_ref[...] = jnp.zeros_like(st_ref)

    ba = ba_ref[0]
    beta_all = jax.nn.sigmoid(ba)
    xg = ba + dtb_ref[...]
    softplus = jnp.maximum(xg, 0.0) + jnp.log1p(jnp.exp(-jnp.abs(xg)))
    g_all = -jnp.exp(alog_ref[...]) * softplus
    gc_all = _dot_exact_lhs(tri_ref[...], g_all)
    for h in heads:
        lb = lane0 - BA_G0 + h
        beta_ref[:, hsl[h]] = jnp.broadcast_to(beta_all[:, lb:lb + 1], (ct, DN_HEAD_DIM))
        gc_ref[:, hsl[h]] = jnp.broadcast_to(gc_all[:, lane0 + h:lane0 + h + 1], (ct, DN_HEAD_DIM))
    for n in range(nchunk):
        gct_ref[n] = gc_all[n * c:(n + 1) * c, :].T

    row = lax.broadcasted_iota(jnp.int32, (c, c), 0)
    col = lax.broadcasted_iota(jnp.int32, (c, c), 1)
    incl = (col >= row) if reverse else (col <= row)
    strict = (col > row) if reverse else (col < row)
    upper = lax.broadcasted_iota(jnp.int32, (c, 2 * c), 1) >= c
    eye_hi = (lax.broadcasted_iota(jnp.int32, (c, 2 * c), 1) - c
              == lax.broadcasted_iota(jnp.int32, (c, 2 * c), 0)).astype(F32)

    def local(chunks):
        units = [(n, slot, h) for n, slot in chunks for h in heads]
        idx = range(len(units))
        rows = [slice(n * c, (n + 1) * c) for n, _, _ in units]
        last = [n * c + (0 if reverse else c - 1) for n, _, _ in units]
        hs = [hsl[h] for _, _, h in units]
        k16 = [k_ref[0, rows[i], hs[i]] for i in idx]
        q = [q_ref[0, rows[i], hs[i]].astype(F32) for i in idx]
        v = [v_ref[0, rows[i], hs[i]].astype(F32) for i in idx]
        beta = [beta_ref[rows[i], hs[i]] for i in idx]
        gcx = [gc_ref[rows[i], hs[i]] for i in idx]
        gtot = [gc_ref[last[i]:last[i] + 1, hs[i]] for i in idx]
        grow = [gct_ref[n, lane0 + h:lane0 + h + 1, :] for n, _, h in units]
        k = [t.astype(F32) for t in k16]
        kb = [k[i] * beta[i] for i in idx]
        kq = [_dot_nt(jnp.concatenate([kb[i], q[i]], axis=0).astype(BF16), k16[i]) for i in idx]
        decay = [jnp.where(incl, jnp.exp(jnp.where(incl, gcx[i][:, :c] - grow[i], 0.0)), 0.0) for i in idx]
        intra = [jnp.where(incl, kq[i][c:] * decay[i], 0.0) for i in idx]
        mk = [jnp.where(strict, -kq[i][:c] * decay[i], 0.0) for i in idx]
        z = [jnp.concatenate([mk[i], jnp.zeros_like(mk[i])], axis=1) + eye_hi for i in idx]
        for _ in range(int(math.log2(c))):
            y = [_dot(z[i][:, :c].astype(BF16), z[i].astype(BF16)) for i in idx]
            z = [y[i] + jnp.where(upper, z[i], 0.0) for i in idx]
        tinv = [z[i][:, c:].astype(BF16) for i in idx]
        eg = [jnp.exp(gcx[i]) for i in idx]
        rhs = [jnp.concatenate([v[i] * beta[i], kb[i] * eg[i]], axis=1).astype(BF16) for i in idx]
        sol = [_dot(tinv[i], rhs[i]) for i in idx]
        kdec_t = [(k[i].T * jnp.exp(gtot[i][:, :c] - grow[i])).astype(BF16) for i in idx]
        for i, (_, slot, h) in enumerate(units):
            u_ref[slot, h] = sol[i][:, :DN_HEAD_DIM]
            wq_ref[slot, h, :c] = sol[i][:, DN_HEAD_DIM:].astype(BF16)
            wq_ref[slot, h, c:] = (q[i] * eg[i]).astype(BF16)
            upd_ref[slot, h, :c] = intra[i].astype(BF16)
            upd_ref[slot, h, c:] = kdec_t[i]

    def recur(n, slot, state):
        rows = slice(n * c, (n + 1) * c)
        last = n * c + (0 if reverse else c - 1)
        etot = [jnp.exp(gc_ref[last:last + 1, hsl[h]]) for h in heads]
        wq = [_dot(wq_ref[slot, h], state[h].astype(BF16)) for h in heads]
        v_new = [(u_ref[slot, h] - wq[h][:c]).astype(BF16) for h in heads]
        upd = [_dot(upd_ref[slot, h], v_new[h]) for h in heads]
        for h in heads:
            o_ref[0, rows, hsl[h]] = wq[h][c:] + upd[h][:c]
        return [state[h] * etot[h] + upd[h][c:] for h in heads]

    order = list(range(nchunk))[::-1] if reverse else list(range(nchunk))
    groups = [order[i:i + DN_AHEAD] for i in range(0, nchunk, DN_AHEAD)]
    slots = lambda gi: [(n, (gi % 2) * DN_AHEAD + j) for j, n in enumerate(groups[gi])]
    state = [st_ref[h] for h in heads]
    local(slots(0))
    for gi in range(len(groups)):
        if gi + 1 < len(groups):
            local(slots(gi + 1))
        for n, slot in slots(gi):
            state = recur(n, slot, state)
    for h in heads:
        st_ref[h] = state[h]


def _dn_tri(ct, reverse):
    c = DN_CHUNK
    r = jnp.arange(ct)
    same = (r[:, None] // c) == (r[None, :] // c)
    tri = same & ((r[None, :] >= r[:, None]) if reverse else (r[None, :] <= r[:, None]))
    return tri.astype(BF16)


def _deltanet(qkv, ba, alog_row, dtb_row, batch, seq, direction, *, ct=512):
    reverse = direction == 1
    nstep = seq // ct
    w = DN_WIDTH
    c = DN_CHUNK
    qkv3 = qkv.reshape(batch, seq, 3 * w)
    ba3 = ba.reshape(batch, seq, BA_W)
    step = (lambda n: nstep - 1 - n) if reverse else (lambda n: n)
    out = pl.pallas_call(
        functools.partial(_dn_kernel, ct=ct, reverse=reverse, lane0=BA_G0 + direction * DN_HEADS),
        grid=(batch, nstep),
        in_specs=[
            pl.BlockSpec((1, ct, w), lambda b, n: (b, step(n), 0)),
            pl.BlockSpec((1, ct, w), lambda b, n: (b, step(n), 1)),
            pl.BlockSpec((1, ct, w), lambda b, n: (b, step(n), 2)),
            pl.BlockSpec((1, ct, BA_W), lambda b, n: (b, step(n), 0)),
            _resident((1, BA_W)),
            _resident((1, BA_W)),
            _resident((ct, ct)),
        ],
        out_specs=pl.BlockSpec((1, ct, w), lambda b, n: (b, step(n), 0)),
        out_shape=jax.ShapeDtypeStruct((batch, seq, w), F32),
        scratch_shapes=[
            pltpu.VMEM((DN_HEADS, DN_HEAD_DIM, DN_HEAD_DIM), F32),
            pltpu.VMEM((ct, w), F32),
            pltpu.VMEM((ct, w), F32),
            pltpu.VMEM((ct // c, BA_W, c), F32),
            pltpu.VMEM((2 * DN_AHEAD, DN_HEADS, c, DN_HEAD_DIM), F32),
            pltpu.VMEM((2 * DN_AHEAD, DN_HEADS, 2 * c, DN_HEAD_DIM), BF16),
            pltpu.VMEM((2 * DN_AHEAD, DN_HEADS, c + DN_HEAD_DIM, c), BF16),
        ],
        compiler_params=_cparams(("parallel", "arbitrary")),
        name=f"deltanet{direction}",
    )(qkv3, qkv3, qkv3, ba3, alog_row, dtb_row, _dn_tri(ct, reverse))
    return out.reshape(batch * seq, w)


def _merge_kernel(x_ref, o1_ref, o2_ref, o3_ref, l1_ref, l2_ref, l3_ref, df_ref, db_ref, z_ref, sc_ref, gate_ref,
                  gb_ref, dng_ref, wa_ref, wd_ref, ws_ref, wo_ref, out_ref):
    l1, l2, l3 = l1_ref[...], l2_ref[...], l3_ref[...]
    mx = jnp.maximum(jnp.maximum(l1, l2), l3)
    e1, e2, e3 = jnp.exp(l1 - mx), jnp.exp(l2 - mx), jnp.exp(l3 - mx)
    attn = (o1_ref[...] * e1 + o2_ref[...] * e2 + o3_ref[...] * e3) / (e1 + e2 + e3)
    y_attn = _dot(attn.astype(BF16), wa_ref[...])
    dn = df_ref[...] + db_ref[...]
    dng = dng_ref[...]
    parts = []
    for h in range(DN_HEADS):
        hs = slice(h * DN_HEAD_DIM, (h + 1) * DN_HEAD_DIM)
        oh = dn[:, hs]
        oh = oh * lax.rsqrt(jnp.mean(oh * oh, axis=-1, keepdims=True) + NORM_EPS) * dng
        zh = z_ref[:, hs].astype(F32)
        parts.append((oh * (zh * jax.nn.sigmoid(zh))).astype(BF16))
    y_dn = _dot(jnp.concatenate(parts, axis=1), wd_ref[...])
    y_sc = _dot(sc_ref[...], ws_ref[...])
    gates = jax.nn.sigmoid(gate_ref[...].astype(F32) + gb_ref[...])
    d = D_MODEL
    merged = gates[:, :d] * y_attn + gates[:, d:2 * d] * y_dn + gates[:, 2 * d:] * y_sc
    out_ref[...] = x_ref[...] + _dot(merged.astype(BF16), wo_ref[...])


def _merge(x, attn_o, attn_l, dn_f, dn_b, pj, sc, gate_bias, dn_norm, wa, wd, ws, wo, *, tm=512):
    m, d = x.shape
    row = lambda i: (i, 0)
    gw = N_BRANCH * d
    return pl.pallas_call(
        _merge_kernel,
        grid=(m // tm,),
        in_specs=[pl.BlockSpec((tm, d), row)]
        + [pl.BlockSpec((tm, ATTN_GW), row)] * 6
        + [pl.BlockSpec((tm, DN_WIDTH), row)] * 2
        + [
            pl.BlockSpec((tm, DN_WIDTH), lambda i: (i, PJ_Z // DN_WIDTH)),
            pl.BlockSpec((tm, SC_WIDTH), row),
            pl.BlockSpec((tm, gw), lambda i: (i, PJ_GATE // gw)),
            _resident((1, gw)),
            _resident((1, DN_HEAD_DIM)),
            _resident((ATTN_GW, d)),
            _resident((DN_WIDTH, d)),
            _resident((SC_WIDTH, d)),
            _resident((d, d)),
        ],
        out_specs=pl.BlockSpec((tm, d), row),
        out_shape=jax.ShapeDtypeStruct((m, d), F32),
        compiler_params=_cparams(("parallel",)),
        name="merge",
    )(x, *attn_o, *attn_l, dn_f, dn_b, pj, sc, pj, gate_bias, dn_norm, wa, wd, ws, wo)


def _prep_layer(l, p):
    w_in = p["w_in"][l]
    o_attn, o_dn = 0, 3 * ATTN_WIDTH
    o_z = o_dn + 3 * DN_WIDTH
    o_beta = o_z + DN_WIDTH
    o_sc = o_beta + 4 * DN_HEADS
    o_gate = o_sc + 3 * SC_WIDTH
    tiles = lambda w: w.astype(BF16).reshape(D_MODEL, -1, PROJ_TN).transpose(1, 0, 2)
    w_attn = tiles(w_in[:, o_attn:o_dn])
    w_tiles = tiles(jnp.concatenate([w_in[:, o_dn:o_z], w_in[:, o_z:o_beta], w_in[:, o_gate:], w_in[:, o_sc:o_gate]], axis=1))
    w_ba = jnp.pad(w_in[:, o_beta:o_sc], ((0, 0), (0, BA_W - 4 * DN_HEADS)))
    pad_row = lambda t: jnp.pad(t.reshape(1, 2 * DN_HEADS), ((0, 0), (BA_G0, BA_W - BA_G0 - 2 * DN_HEADS)))
    bf = lambda t: t.astype(BF16)
    row = lambda t: t.reshape(1, -1)
    return dict(
        ffn1=(row(p["ffn1_norm"][l]), bf(p["ffn1_w_gate"][l]), bf(p["ffn1_w_up"][l]), bf(p["ffn1_w_down"][l])),
        ffn2=(row(p["ffn2_norm"][l]), bf(p["ffn2_w_gate"][l]), bf(p["ffn2_w_up"][l]), bf(p["ffn2_w_down"][l])),
        mix_norm=row(p["mix_norm"][l]),
        w_attn=w_attn,
        w_tiles=w_tiles,
        w_ba=bf(w_ba),
        dn_conv=p["dn_conv"][l],
        alog=pad_row(p["dn_a_log"][l]),
        dtb=pad_row(p["dn_dt_bias"][l]),
        dn_norm=row(p["dn_norm"][l]),
        sc_conv=p["sc_conv"][l],
        gate_bias=row(p["gate_bias"][l]),
        wa=bf(p["w_attn_br"][l]),
        wd=bf(p["w_dn_br"][l]),
        ws=bf(p["w_sc_br"][l]),
        wo=bf(p["w_out"][l]),
    )


def _mixer(x, lp, batch, seq, tables):
    at = _attn_proj(x, lp["mix_norm"], lp["w_attn"], tables[0], tables[1], seq)
    pj, ba = _proj(x, lp["mix_norm"], lp["w_tiles"], lp["w_ba"])
    attn = [_attn_group(at, batch, seq, gi) for gi in range(len(ATTN_GROUPS))]
    qkv = _dn_prep(pj, lp["dn_conv"], seq)
    dn_f = _deltanet(qkv, ba, lp["alog"], lp["dtb"], batch, seq, 0)
    dn_b = _deltanet(qkv, ba, lp["alog"], lp["dtb"], batch, seq, 1)
    sc = _sc_mix(pj, lp["sc_conv"], seq)
    return _merge(x, [a[0] for a in attn], [a[1] for a in attn], dn_f, dn_b, pj, sc, lp["gate_bias"],
                  lp["dn_norm"], lp["wa"], lp["wd"], lp["ws"], lp["wo"])


def _trunk(x, layers, final_g):
    batch, seq, d = x.shape
    tables = _rope_tables(seq)
    x = x.reshape(batch * seq, d)
    for l, lp in enumerate(layers):
        x = _ffn(x, *lp["ffn1"])
        x = _mixer(x, lp, batch, seq, tables)
        x = _ffn(x, *lp["ffn2"], final_g=final_g if l == len(layers) - 1 else None)
    return x.reshape(batch, seq, d)


def kernel(x_prompt, x_sample, ffn1_norm, ffn1_w_gate, ffn1_w_up, ffn1_w_down, mix_norm, w_in, dn_conv, dn_a_log, dn_dt_bias, dn_norm, sc_conv, gate_bias, w_attn_br, w_dn_br, w_sc_br, w_out, ffn2_norm, ffn2_w_gate, ffn2_w_up, ffn2_w_down, final_norm):
    p = dict(ffn1_norm=ffn1_norm, ffn1_w_gate=ffn1_w_gate, ffn1_w_up=ffn1_w_up, ffn1_w_down=ffn1_w_down,
             mix_norm=mix_norm, w_in=w_in, dn_conv=dn_conv, dn_a_log=dn_a_log, dn_dt_bias=dn_dt_bias,
             dn_norm=dn_norm, sc_conv=sc_conv, gate_bias=gate_bias, w_attn_br=w_attn_br, w_dn_br=w_dn_br,
             w_sc_br=w_sc_br, w_out=w_out, ffn2_norm=ffn2_norm, ffn2_w_gate=ffn2_w_gate, ffn2_w_up=ffn2_w_up,
             ffn2_w_down=ffn2_w_down)
    layers = [_prep_layer(l, p) for l in range(ffn1_norm.shape[0])]
    final_g = final_norm.reshape(1, -1)
    return _trunk(x_prompt, layers, final_g), _trunk(x_sample, layers, final_g)
```

```python
import functools
import math

import jax
import jax.numpy as jnp
from jax import lax
from jax.experimental import pallas as pl
from jax.experimental.pallas import tpu as pltpu

F32 = jnp.float32
BF16 = jnp.bfloat16

D_MODEL = 1024
HEAD_DIM = 64
ATTN_GROUPS = ((128, 1), (512, 4), (2048, 16))
ATTN_GROUP_HEADS = 4
ATTN_GW = ATTN_GROUP_HEADS * HEAD_DIM
ATTN_WIDTH = 3 * ATTN_GW
ROT_DIM = HEAD_DIM // 4
ROPE_THETA = 500000.0
MASK_VALUE = -1e30
DN_HEADS = 6
DN_HEAD_DIM = 128
DN_WIDTH = DN_HEADS * DN_HEAD_DIM
DN_CONV = 5
DN_CHUNK = 64
SC_WIDTH = 768
SC_CONV = 3
FFN_DIM = 2816
N_BRANCH = 3
NORM_EPS = 1e-6

PROJ_TN = 768
ATTN_TILES = 3 * ATTN_WIDTH // PROJ_TN
PJ_DN = 0
PJ_Z = PJ_DN + 3 * DN_WIDTH
PJ_GATE = PJ_Z + DN_WIDTH
PJ_SC = PJ_GATE + N_BRANCH * D_MODEL
PJ_W = PJ_SC + 3 * SC_WIDTH
PJ_TILES = PJ_W // PROJ_TN
BA_W = 128
BA_G0 = 2 * DN_HEADS

DN_AHEAD = 4
ATTN_UNITS = 4
HALO = 8
HALO_BLK = 16

VMEM_LIMIT = 56 * 1024 * 1024


def _cparams(sem):
    return pltpu.CompilerParams(dimension_semantics=sem, vmem_limit_bytes=VMEM_LIMIT)


def _resident(shape):
    return pl.BlockSpec(shape, lambda *_: (0,) * len(shape), pipeline_mode=pl.Buffered(1))


def _rms(x, g):
    return x * lax.rsqrt(jnp.mean(x * x, axis=-1, keepdims=True) + NORM_EPS) * g


def _dot(a, b):
    return jnp.dot(a, b, preferred_element_type=F32)


def _dot_nt(a, b):
    return lax.dot_general(a, b, (((1,), (1,)), ((), ())), preferred_element_type=F32)


def _dot_exact_lhs(sel, x):
    x1 = x.astype(BF16)
    r = x - x1.astype(F32)
    x2 = r.astype(BF16)
    x3 = (r - x2.astype(F32)).astype(BF16)
    return _dot(sel, x1) + _dot(sel, x2) + _dot(sel, x3)


def _ffn_kernel(x_ref, g_ref, wg_ref, wu_ref, wd_ref, fg_ref, o_ref, *, tf, final):
    x = x_ref[...]
    xn = _rms(x, g_ref[...]).astype(BF16)
    acc = jnp.zeros_like(x)
    for c in range(FFN_DIM // tf):
        sl = slice(c * tf, (c + 1) * tf)
        hg = _dot(xn, wg_ref[:, sl])
        hu = _dot(xn, wu_ref[:, sl])
        h = (hg * jax.nn.sigmoid(hg) * hu).astype(BF16)
        acc = acc + _dot(h, wd_ref[sl, :])
    y = x + 0.5 * acc
    if final:
        y = _rms(y, fg_ref[...])
    o_ref[...] = y


def _ffn(x, g, wg, wu, wd, final_g=None, *, tm=512, tf=256):
    m, d = x.shape
    final = final_g is not None
    fg = final_g if final else g
    return pl.pallas_call(
        functools.partial(_ffn_kernel, tf=tf, final=final),
        grid=(m // tm,),
        in_specs=[
            pl.BlockSpec((tm, d), lambda i: (i, 0)),
            _resident((1, d)),
            _resident((d, FFN_DIM)),
            _resident((d, FFN_DIM)),
            _resident((FFN_DIM, d)),
            _resident((1, d)),
        ],
        out_specs=pl.BlockSpec((tm, d), lambda i: (i, 0)),
        out_shape=jax.ShapeDtypeStruct((m, d), F32),
        compiler_params=_cparams(("parallel",)),
        name="ffn",
    )(x, g, wg, wu, wd, fg)


def _attn_proj_kernel(x_ref, g_ref, w_ref, cos_ref, sin_ref, at_ref, xn_ref):
    j = pl.program_id(1)

    @pl.when(j == 0)
    def _():
        xn_ref[...] = _rms(x_ref[...], g_ref[...]).astype(BF16)

    y = _dot(xn_ref[...], w_ref[j])

    @pl.when(j < 2)
    def _():
        cos = cos_ref[0]
        sin = sin_ref[0]
        for s in range(PROJ_TN // 128):
            ys = y[:, s * 128:(s + 1) * 128]
            at_ref[:, s * 128:(s + 1) * 128] = ys * cos + pltpu.roll(ys, HEAD_DIM, 1) * sin

    @pl.when(j == 2)
    def _():
        at_ref[...] = y


def _attn_proj(x, g, w_tiles, cos_t, sin_t, seq, *, tm=1024):
    m, d = x.shape
    nseq = seq // tm
    return pl.pallas_call(
        _attn_proj_kernel,
        grid=(m // tm, ATTN_TILES),
        in_specs=[
            pl.BlockSpec((tm, d), lambda i, j: (i, 0)),
            _resident((1, d)),
            _resident((ATTN_TILES, d, PROJ_TN)),
            pl.BlockSpec((1, tm, 128), lambda i, j: (jnp.minimum(j, 1), i % nseq, 0)),
            pl.BlockSpec((1, tm, 128), lambda i, j: (jnp.minimum(j, 1), i % nseq, 0)),
        ],
        out_specs=pl.BlockSpec((tm, PROJ_TN), lambda i, j: (i, j)),
        out_shape=jax.ShapeDtypeStruct((m, 3 * ATTN_WIDTH), F32),
        scratch_shapes=[pltpu.VMEM((tm, d), BF16)],
        compiler_params=_cparams(("parallel", "arbitrary")),
        name="attn_proj",
    )(x, g, w_tiles, cos_t, sin_t)


def _proj_kernel(x_ref, g_ref, w_ref, wba_ref, pj_ref, ba_ref, xn_ref):
    j = pl.program_id(1)

    @pl.when(j == 0)
    def _():
        xn = _rms(x_ref[...], g_ref[...]).astype(BF16)
        xn_ref[...] = xn
        ba_ref[...] = _dot(xn, wba_ref[...])

    pj_ref[...] = _dot(xn_ref[...], w_ref[j]).astype(BF16)


def _proj(x, g, w_tiles, w_ba, *, tm=1024):
    m, d = x.shape
    return pl.pallas_call(
        _proj_kernel,
        grid=(m // tm, PJ_TILES),
        in_specs=[
            pl.BlockSpec((tm, d), lambda i, j: (i, 0)),
            _resident((1, d)),
            _resident((PJ_TILES, d, PROJ_TN)),
            _resident((d, BA_W)),
        ],
        out_specs=[
            pl.BlockSpec((tm, PROJ_TN), lambda i, j: (i, j)),
            pl.BlockSpec((tm, BA_W), lambda i, j: (i, 0)),
        ],
        out_shape=[jax.ShapeDtypeStruct((m, PJ_W), BF16), jax.ShapeDtypeStruct((m, BA_W), F32)],
        scratch_shapes=[pltpu.VMEM((tm, d), BF16)],
        compiler_params=_cparams(("parallel", "arbitrary")),
        name="proj",
    )(x, g, w_tiles, w_ba)


def _rope_tables(seq):
    half = ROT_DIM // 2
    inv_freq = ROPE_THETA ** (-2.0 * jnp.arange(half, dtype=F32) / ROT_DIM)
    ang = jnp.arange(seq, dtype=F32)[:, None] * inv_freq[None, :]
    cos, sin = jnp.cos(ang), jnp.sin(ang)
    ones = jnp.ones((seq, HEAD_DIM - ROT_DIM), F32)
    cos_h = jnp.concatenate([cos, cos, ones, cos, cos, ones], axis=1)
    sin_h = jnp.concatenate([-sin, -sin, 0.0 * ones, sin, sin, 0.0 * ones], axis=1)
    qs = HEAD_DIM ** -0.5
    return jnp.stack([cos_h * qs, cos_h]), jnp.stack([sin_h * qs, sin_h])


def _qk_perm():
    a = jnp.arange
    half = ROT_DIM // 2
    rest = (HEAD_DIM - ROT_DIM) // 2
    lo = [a(0, half), a(HEAD_DIM, HEAD_DIM + half), a(ROT_DIM, ROT_DIM + rest),
          a(HEAD_DIM + ROT_DIM, HEAD_DIM + ROT_DIM + rest)]
    hi = [a(half, ROT_DIM), a(HEAD_DIM + half, HEAD_DIM + ROT_DIM), a(ROT_DIM + rest, HEAD_DIM),
          a(HEAD_DIM + ROT_DIM + rest, 2 * HEAD_DIM)]
    return jnp.concatenate(lo + hi)


def _qk_first_head(lane):
    half = ROT_DIM // 2
    rest = (HEAD_DIM - ROT_DIM) // 2
    l = lane % HEAD_DIM
    return (l < half) | ((l >= ROT_DIM) & (l < ROT_DIM + rest))


def _attn_kernel(q_ref, k_ref, v_ref, o_ref, lse_ref, kres_ref, vres_ref, *, dil, tq, tk, half, length, nsub):
    i = pl.program_id(2)

    @pl.when(i == 0)
    def _():
        for r in range(dil):
            rows = pl.ds(r, length, stride=dil) if dil > 1 else pl.ds(0, length)
            kres_ref[r * length:(r + 1) * length, :] = k_ref[0, rows, :].astype(BF16)
            vres_ref[r * length:(r + 1) * length, :] = v_ref[0, rows, :].astype(BF16)

    lane = lax.broadcasted_iota(jnp.int32, (1, 128), 1)
    first_out = lane < HEAD_DIM
    first_qk = _qk_first_head(lane)
    qiota = lax.broadcasted_iota(jnp.int32, (tq, 1), 0)
    kiota = lax.broadcasted_iota(jnp.int32, (1, tk), 1)

    def group(g, carry):
        rows, mask, q, kw, vw = [], [], [], [], []
        for j in range(ATTN_UNITS):
            u = g * ATTN_UNITS + j
            r = u % dil
            sub = u // dil
            blk = i * nsub + sub
            ks = pl.multiple_of(jnp.clip(blk * tq - half, 0, length - tk), 64)
            mask.append(jnp.abs(blk * tq + qiota - ks - kiota) <= half)
            if dil > 1:
                rows.append(pl.ds(r + sub * (tq * dil), tq, stride=dil))
            else:
                rows.append(pl.ds(pl.multiple_of(sub * tq, tq), tq))
            q.append(q_ref[0, rows[j], :].astype(BF16))
            kbase = pl.multiple_of(r * length + ks, 64)
            kw.append(kres_ref[pl.ds(kbase, tk), :])
            vw.append(vres_ref[pl.ds(kbase, tk), :])
        chains = [(j, sel) for j in range(ATTN_UNITS) for sel in (first_qk, ~first_qk)]
        sc = [_dot_nt(jnp.where(sel, q[j], jnp.zeros_like(q[j])), kw[j]) for j, sel in chains]
        sc = [jnp.where(mask[j], s, MASK_VALUE) for (j, _), s in zip(chains, sc)]
        mx = [jnp.max(s, axis=-1, keepdims=True) for s in sc]
        p = [jnp.exp(s - m) for s, m in zip(sc, mx)]
        den = [jnp.sum(t, axis=-1, keepdims=True) for t in p]
        o = [_dot(t.astype(BF16), vw[j]) / d for (j, _), t, d in zip(chains, p, den)]
        lse = [m + jnp.log(d) for m, d in zip(mx, den)]
        for j in range(ATTN_UNITS):
            o_ref[0, rows[j], :] = jnp.where(first_out, o[2 * j], o[2 * j + 1])
            lse_ref[0, rows[j], :] = jnp.where(first_out, lse[2 * j], lse[2 * j + 1])
        return carry

    lax.fori_loop(0, dil * nsub // ATTN_UNITS, group, 0)


def _attn_group(at, batch, seq, gi):
    window, dil = ATTN_GROUPS[gi]
    half = (window // 2) // dil
    length = seq // dil
    tq = 2 * half
    tk = tq + 2 * half
    nsub = max(1, 2 * ATTN_UNITS // dil)
    assert half == 64 and seq % dil == 0 and tk <= length and length % (tq * nsub) == 0
    assert (dil * nsub) % ATTN_UNITS == 0
    tb = tq * dil * nsub
    at3 = at.reshape(batch, seq, 3 * ATTN_WIDTH)
    nslab = ATTN_GW // 128
    nb = ATTN_WIDTH // 128
    o, lse = pl.pallas_call(
        functools.partial(_attn_kernel, dil=dil, tq=tq, tk=tk, half=half, length=length, nsub=nsub),
        grid=(batch, nslab, seq // tb),
        in_specs=[
            pl.BlockSpec((1, tb, 128), lambda b, s, i: (b, i, gi * nslab + s)),
            pl.BlockSpec((1, seq, 128), lambda b, s, i: (b, 0, nb + gi * nslab + s), pipeline_mode=pl.Buffered(1)),
            pl.BlockSpec((1, seq, 128), lambda b, s, i: (b, 0, 2 * nb + gi * nslab + s), pipeline_mode=pl.Buffered(1)),
        ],
        out_specs=[
            pl.BlockSpec((1, tb, 128), lambda b, s, i: (b, i, s)),
            pl.BlockSpec((1, tb, 128), lambda b, s, i: (b, i, s)),
        ],
        out_shape=[jax.ShapeDtypeStruct((batch, seq, ATTN_GW), F32)] * 2,
        scratch_shapes=[pltpu.VMEM((seq, 128), BF16), pltpu.VMEM((seq, 128), BF16)],
        compiler_params=_cparams(("parallel", "parallel", "arbitrary")),
        name=f"attn{gi}",
    )(at3, at3, at3)
    return o.reshape(batch * seq, ATTN_GW), lse.reshape(batch * seq, ATTN_GW)


def _halo_tile(prev_ref, main_ref, next_ref, first, last, sl):
    xm = main_ref[:, sl].astype(F32)
    xp = prev_ref[:, sl].astype(F32)[HALO_BLK - HALO:]
    xn = next_ref[:, sl].astype(F32)[:HALO]
    xp = jnp.where(first, 0.0, xp)
    xn = jnp.where(last, 0.0, xn)
    return jnp.concatenate([xp, xm, xn], axis=0)


def _tap(ext, off, t):
    n = ext.shape[0]
    return pltpu.roll(ext, (-off) % n, 0)[HALO:HALO + t] if off else ext[HALO:HALO + t]


def _seq_edges(i, t, seq):
    p0 = (i * t) % seq
    return p0 == 0, p0 + t == seq


def _dn_prep_kernel(prev_ref, main_ref, next_ref, cw_ref, o_ref, *, t, seq):
    first, last = _seq_edges(pl.program_id(0), t, seq)
    for s in range(3 * DN_HEADS):
        sl = slice(s * DN_HEAD_DIM, (s + 1) * DN_HEAD_DIM)
        ext = _halo_tile(prev_ref, main_ref, next_ref, first, last, sl)
        acc = None
        for tap in range(DN_CONV):
            term = _tap(ext, tap - DN_CONV // 2, t) * cw_ref[tap:tap + 1, sl]
            acc = term if acc is None else acc + term
        y = acc * jax.nn.sigmoid(acc)
        if s < 2 * DN_HEADS:
            scale = DN_HEAD_DIM ** -0.5 if s < DN_HEADS else 1.0
            y = y * (lax.rsqrt(jnp.sum(y * y, axis=-1, keepdims=True) + NORM_EPS) * scale)
        o_ref[:, sl] = y.astype(BF16)


def _halo_specs(t, m, colblk, width):
    nb = m // HALO_BLK
    r = t // HALO_BLK
    return [
        pl.BlockSpec((HALO_BLK, width), lambda i: (jnp.maximum(i * r - 1, 0), colblk)),
        pl.BlockSpec((t, width), lambda i: (i, colblk)),
        pl.BlockSpec((HALO_BLK, width), lambda i: (jnp.minimum((i + 1) * r, nb - 1), colblk)),
    ]


def _dn_prep(pj, dn_conv, seq, *, t=512):
    m = pj.shape[0]
    w = 3 * DN_WIDTH
    return pl.pallas_call(
        functools.partial(_dn_prep_kernel, t=t, seq=seq),
        grid=(m // t,),
        in_specs=_halo_specs(t, m, PJ_DN // w, w) + [_resident((DN_CONV, w))],
        out_specs=pl.BlockSpec((t, w), lambda i: (i, 0)),
        out_shape=jax.ShapeDtypeStruct((m, w), BF16),
        compiler_params=_cparams(("parallel",)),
        name="dn_prep",
    )(pj, pj, pj, dn_conv)


def _sc_kernel(b_ref, cp_ref, cm_ref, cn_ref, xp_ref, xm_ref, xn_ref, cw_ref, o_ref, *, t, seq):
    first, last = _seq_edges(pl.program_id(0), t, seq)
    for s in range(SC_WIDTH // 128):
        sl = slice(s * 128, (s + 1) * 128)
        ext = (_halo_tile(cp_ref, cm_ref, cn_ref, first, last, sl)
               * _halo_tile(xp_ref, xm_ref, xn_ref, first, last, sl))
        acc = None
        for tap in range(SC_CONV):
            term = _tap(ext, tap - SC_CONV // 2, t) * cw_ref[tap:tap + 1, sl]
            acc = term if acc is None else acc + term
        o_ref[:, sl] = (b_ref[:, sl].astype(F32) * acc).astype(BF16)


def _sc_mix(pj, sc_conv, seq, *, t=512):
    m = pj.shape[0]
    w = SC_WIDTH
    cb = PJ_SC // w
    return pl.pallas_call(
        functools.partial(_sc_kernel, t=t, seq=seq),
        grid=(m // t,),
        in_specs=[pl.BlockSpec((t, w), lambda i: (i, cb))]
        + _halo_specs(t, m, cb + 1, w)
        + _halo_specs(t, m, cb + 2, w)
        + [_resident((SC_CONV, w))],
        out_specs=pl.BlockSpec((t, w), lambda i: (i, 0)),
        out_shape=jax.ShapeDtypeStruct((m, w), BF16),
        compiler_params=_cparams(("parallel",)),
        name="sc_mix",
    )(pj, pj, pj, pj, pj, pj, pj, sc_conv)


def _dn_kernel(q_ref, k_ref, v_ref, ba_ref, alog_ref, dtb_ref, tri_ref, o_ref,
               st_ref, beta_ref, gc_ref, gct_ref, *slot_refs, ct, reverse, lane0):
    c = DN_CHUNK
    nchunk = ct // c
    heads = range(DN_HEADS)
    hsl = [slice(h * DN_HEAD_DIM, (h + 1) * DN_HEAD_DIM) for h in heads]
    nslot = 2 * DN_AHEAD
    u_refs, wq_refs, upd_refs = slot_refs[:nslot], slot_refs[nslot:2 * nslot], slot_refs[2 * nslot:]

    @pl.when(pl.program_id(1) == 0)
    def _():
        st_ref[...] = jnp.zeros_like(st_ref)

    ba = ba_ref[0]
    beta_all = jax.nn.sigmoid(ba)
    xg = ba + dtb_ref[...]
    softplus = jnp.maximum(xg, 0.0) + jnp.log1p(jnp.exp(-jnp.abs(xg)))
    g_all = -jnp.exp(alog_ref[...]) * softplus
    gc_all = _dot_exact_lhs(tri_ref[...], g_all)
    for h in heads:
        lb = lane0 - BA_G0 + h
        beta_ref[:, hsl[h]] = jnp.broadcast_to(beta_all[:, lb:lb + 1], (ct, DN_HEAD_DIM))
        gc_ref[:, hsl[h]] = jnp.broadcast_to(gc_all[:, lane0 + h:lane0 + h + 1], (ct, DN_HEAD_DIM))
    for n in range(nchunk):
        gct_ref[n] = gc_all[n * c:(n + 1) * c, :].T

    row = lax.broadcasted_iota(jnp.int32, (c, c), 0)
    col = lax.broadcasted_iota(jnp.int32, (c, c), 1)
    incl = (col >= row) if reverse else (col <= row)
    strict = (col > row) if reverse else (col < row)
    upper = lax.broadcasted_iota(jnp.int32, (c, 2 * c), 1) >= c
    eye_hi = (lax.broadcasted_iota(jnp.int32, (c, 2 * c), 1) - c
              == lax.broadcasted_iota(jnp.int32, (c, 2 * c), 0)).astype(F32)

    def local(chunks):
        units = [(n, slot, h) for n, slot in chunks for h in heads]
        idx = range(len(units))
        rows = [slice(n * c, (n + 1) * c) for n, _, _ in units]
        last = [n * c + (0 if reverse else c - 1) for n, _, _ in units]
        hs = [hsl[h] for _, _, h in units]
        k16 = [k_ref[0, rows[i], hs[i]] for i in idx]
        q = [q_ref[0, rows[i], hs[i]].astype(F32) for i in idx]
        v = [v_ref[0, rows[i], hs[i]].astype(F32) for i in idx]
        beta = [beta_ref[rows[i], hs[i]] for i in idx]
        gcx = [gc_ref[rows[i], hs[i]] for i in idx]
        gtot = [gc_ref[last[i]:last[i] + 1, hs[i]] for i in idx]
        grow = [gct_ref[n, lane0 + h:lane0 + h + 1, :] for n, _, h in units]
        k = [t.astype(F32) for t in k16]
        kb = [k[i] * beta[i] for i in idx]
        kq = [_dot_nt(jnp.concatenate([kb[i], q[i]], axis=0).astype(BF16), k16[i]) for i in idx]
        decay = [jnp.where(incl, jnp.exp(jnp.where(incl, gcx[i][:, :c] - grow[i], 0.0)), 0.0) for i in idx]
        intra = [jnp.where(incl, kq[i][c:] * decay[i], 0.0) for i in idx]
        mk = [jnp.where(strict, -kq[i][:c] * decay[i], 0.0) for i in idx]
        z = [jnp.concatenate([mk[i], jnp.zeros_like(mk[i])], axis=1) + eye_hi for i in idx]
        for _ in range(int(math.log2(c))):
            y = [_dot(z[i][:, :c].astype(BF16), z[i].astype(BF16)) for i in idx]
            z = [y[i] + jnp.where(upper, z[i], 0.0) for i in idx]
        eg = [jnp.exp(gcx[i]) for i in idx]
        rhs = [jnp.concatenate([v[i] * beta[i], kb[i] * eg[i]], axis=1).astype(BF16) for i in idx]
        rhs = [jnp.concatenate([jnp.zeros_like(t), t], axis=0) for t in rhs]
        sol = [_dot(z[i].astype(BF16), rhs[i]) for i in idx]
        kdec_t = [(k[i].T * jnp.exp(gtot[i][:, :c] - grow[i])).astype(BF16) for i in idx]
        for i, (_, slot, h) in enumerate(units):
            u_refs[slot][h] = sol[i][:, :DN_HEAD_DIM]
            wq_refs[slot][h, :c] = sol[i][:, DN_HEAD_DIM:].astype(BF16)
            wq_refs[slot][h, c:] = (q[i] * eg[i]).astype(BF16)
            upd_refs[slot][h, :c] = intra[i].astype(BF16)
            upd_refs[slot][h, c:] = kdec_t[i]

    def recur(n, slot, state):
        rows = slice(n * c, (n + 1) * c)
        last = n * c + (0 if reverse else c - 1)
        etot = [jnp.exp(gc_ref[last:last + 1, hsl[h]]) for h in heads]
        wq = [_dot(wq_refs[slot][h], state[h].astype(BF16)) for h in heads]
        v_new = [(u_refs[slot][h] - wq[h][:c]).astype(BF16) for h in heads]
        upd = [_dot(upd_refs[slot][h], v_new[h]) for h in heads]
        for h in heads:
            o_ref[0, rows, hsl[h]] = wq[h][c:] + upd[h][:c]
        return [state[h] * etot[h] + upd[h][c:] for h in heads]

    order = list(range(nchunk))[::-1] if reverse else list(range(nchunk))
    groups = [order[i:i + DN_AHEAD] for i in range(0, nchunk, DN_AHEAD)]
    slots = lambda gi: [(n, (gi % 2) * DN_AHEAD + j) for j, n in enumerate(groups[gi])]
    state = [st_ref[h] for h in heads]
    local(slots(0))
    for gi in range(len(groups)):
        if gi + 1 < len(groups):
            local(slots(gi + 1))
        for n, slot in slots(gi):
            state = recur(n, slot, state)
    for h in heads:
        st_ref[h] = state[h]


def _dn_tri(ct, reverse):
    c = DN_CHUNK
    r = jnp.arange(ct)
    same = (r[:, None] // c) == (r[None, :] // c)
    tri = same & ((r[None, :] >= r[:, None]) if reverse else (r[None, :] <= r[:, None]))
    return tri.astype(BF16)


def _deltanet(qkv, ba, alog_row, dtb_row, batch, seq, direction, *, ct=512):
    reverse = direction == 1
    nstep = seq // ct
    w = DN_WIDTH
    c = DN_CHUNK
    qkv3 = qkv.reshape(batch, seq, 3 * w)
    ba3 = ba.reshape(batch, seq, BA_W)
    step = (lambda n: nstep - 1 - n) if reverse else (lambda n: n)
    out = pl.pallas_call(
        functools.partial(_dn_kernel, ct=ct, reverse=reverse, lane0=BA_G0 + direction * DN_HEADS),
        grid=(batch, nstep),
        in_specs=[
            pl.BlockSpec((1, ct, w), lambda b, n: (b, step(n), 0)),
            pl.BlockSpec((1, ct, w), lambda b, n: (b, step(n), 1)),
            pl.BlockSpec((1, ct, w), lambda b, n: (b, step(n), 2)),
            pl.BlockSpec((1, ct, BA_W), lambda b, n: (b, step(n), 0)),
            _resident((1, BA_W)),
            _resident((1, BA_W)),
            _resident((ct, ct)),
        ],
        out_specs=pl.BlockSpec((1, ct, w), lambda b, n: (b, step(n), 0)),
        out_shape=jax.ShapeDtypeStruct((batch, seq, w), F32),
        scratch_shapes=[
            pltpu.VMEM((DN_HEADS, DN_HEAD_DIM, DN_HEAD_DIM), F32),
            pltpu.VMEM((ct, w), F32),
            pltpu.VMEM((ct, w), F32),
            pltpu.VMEM((ct // c, BA_W, c), F32),
        ]
        + [pltpu.VMEM((DN_HEADS, c, DN_HEAD_DIM), F32)] * (2 * DN_AHEAD)
        + [pltpu.VMEM((DN_HEADS, 2 * c, DN_HEAD_DIM), BF16)] * (2 * DN_AHEAD)
        + [pltpu.VMEM((DN_HEADS, c + DN_HEAD_DIM, c), BF16)] * (2 * DN_AHEAD),
        compiler_params=_cparams(("parallel", "arbitrary")),
        name=f"deltanet{direction}",
    )(qkv3, qkv3, qkv3, ba3, alog_row, dtb_row, _dn_tri(ct, reverse))
    return out.reshape(batch * seq, w)


def _merge_kernel(x_ref, o1_ref, o2_ref, o3_ref, l1_ref, l2_ref, l3_ref, df_ref, db_ref, z_ref, sc_ref, gate_ref,
                  gb_ref, dng_ref, wa_ref, wd_ref, ws_ref, wo_ref, out_ref):
    l1, l2, l3 = l1_ref[...], l2_ref[...], l3_ref[...]
    mx = jnp.maximum(jnp.maximum(l1, l2), l3)
    e1, e2, e3 = jnp.exp(l1 - mx), jnp.exp(l2 - mx), jnp.exp(l3 - mx)
    attn = (o1_ref[...] * e1 + o2_ref[...] * e2 + o3_ref[...] * e3) / (e1 + e2 + e3)
    y_attn = _dot(attn.astype(BF16), wa_ref[...])
    dn = df_ref[...] + db_ref[...]
    dng = dng_ref[...]
    parts = []
    for h in range(DN_HEADS):
        hs = slice(h * DN_HEAD_DIM, (h + 1) * DN_HEAD_DIM)
        oh = dn[:, hs]
        oh = oh * lax.rsqrt(jnp.mean(oh * oh, axis=-1, keepdims=True) + NORM_EPS) * dng
        zh = z_ref[:, hs].astype(F32)
        parts.append((oh * (zh * jax.nn.sigmoid(zh))).astype(BF16))
    y_dn = _dot(jnp.concatenate(parts, axis=1), wd_ref[...])
    y_sc = _dot(sc_ref[...], ws_ref[...])
    gates = jax.nn.sigmoid(gate_ref[...].astype(F32) + gb_ref[...])
    d = D_MODEL
    merged = gates[:, :d] * y_attn + gates[:, d:2 * d] * y_dn + gates[:, 2 * d:] * y_sc
    out_ref[...] = x_ref[...] + _dot(merged.astype(BF16), wo_ref[...])


def _merge(x, attn_o, attn_l, dn_f, dn_b, pj, sc, gate_bias, dn_norm, wa, wd, ws, wo, *, tm=512):
    m, d = x.shape
    row = lambda i: (i, 0)
    gw = N_BRANCH * d
    return pl.pallas_call(
        _merge_kernel,
        grid=(m // tm,),
        in_specs=[pl.BlockSpec((tm, d), row)]
        + [pl.BlockSpec((tm, ATTN_GW), row)] * 6
        + [pl.BlockSpec((tm, DN_WIDTH), row)] * 2
        + [
            pl.BlockSpec((tm, DN_WIDTH), lambda i: (i, PJ_Z // DN_WIDTH)),
            pl.BlockSpec((tm, SC_WIDTH), row),
            pl.BlockSpec((tm, gw), lambda i: (i, PJ_GATE // gw)),
            _resident((1, gw)),
            _resident((1, DN_HEAD_DIM)),
            _resident((ATTN_GW, d)),
            _resident((DN_WIDTH, d)),
            _resident((SC_WIDTH, d)),
            _resident((d, d)),
        ],
        out_specs=pl.BlockSpec((tm, d), row),
        out_shape=jax.ShapeDtypeStruct((m, d), F32),
        compiler_params=_cparams(("parallel",)),
        name="merge",
    )(x, *attn_o, *attn_l, dn_f, dn_b, pj, sc, pj, gate_bias, dn_norm, wa, wd, ws, wo)


def _prep_layer(l, p):
    w_in = p["w_in"][l]
    o_attn, o_dn = 0, 3 * ATTN_WIDTH
    o_z = o_dn + 3 * DN_WIDTH
    o_beta = o_z + DN_WIDTH
    o_sc = o_beta + 4 * DN_HEADS
    o_gate = o_sc + 3 * SC_WIDTH
    tiles = lambda w: w.astype(BF16).reshape(D_MODEL, -1, PROJ_TN).transpose(1, 0, 2)
    w_qk = w_in[:, o_attn:o_attn + 2 * ATTN_WIDTH].reshape(D_MODEL, -1, 128)[:, :, _qk_perm()]
    w_attn = tiles(jnp.concatenate([w_qk.reshape(D_MODEL, -1), w_in[:, o_attn + 2 * ATTN_WIDTH:o_dn]], axis=1))
    w_tiles = tiles(jnp.concatenate([w_in[:, o_dn:o_z], w_in[:, o_z:o_beta], w_in[:, o_gate:], w_in[:, o_sc:o_gate]], axis=1))
    w_ba = jnp.pad(w_in[:, o_beta:o_sc], ((0, 0), (0, BA_W - 4 * DN_HEADS)))
    pad_row = lambda t: jnp.pad(t.reshape(1, 2 * DN_HEADS), ((0, 0), (BA_G0, BA_W - BA_G0 - 2 * DN_HEADS)))
    bf = lambda t: t.astype(BF16)
    row = lambda t: t.reshape(1, -1)
    return dict(
        ffn1=(row(p["ffn1_norm"][l]), bf(p["ffn1_w_gate"][l]), bf(p["ffn1_w_up"][l]), bf(p["ffn1_w_down"][l])),
        ffn2=(row(p["ffn2_norm"][l]), bf(p["ffn2_w_gate"][l]), bf(p["ffn2_w_up"][l]), bf(p["ffn2_w_down"][l])),
        mix_norm=row(p["mix_norm"][l]),
        w_attn=w_attn,
        w_tiles=w_tiles,
        w_ba=bf(w_ba),
        dn_conv=p["dn_conv"][l],
        alog=pad_row(p["dn_a_log"][l]),
        dtb=pad_row(p["dn_dt_bias"][l]),
        dn_norm=row(p["dn_norm"][l]),
        sc_conv=p["sc_conv"][l],
        gate_bias=row(p["gate_bias"][l]),
        wa=bf(p["w_attn_br"][l]),
        wd=bf(p["w_dn_br"][l]),
        ws=bf(p["w_sc_br"][l]),
        wo=bf(p["w_out"][l]),
    )


def _mixer(x, lp, batch, seq, tables):
    at = _attn_proj(x, lp["mix_norm"], lp["w_attn"], tables[0], tables[1], seq)
    pj, ba = _proj(x, lp["mix_norm"], lp["w_tiles"], lp["w_ba"])
    attn = [_attn_group(at, batch, seq, gi) for gi in range(len(ATTN_GROUPS))]
    qkv = _dn_prep(pj, lp["dn_conv"], seq)
    dn_f = _deltanet(qkv, ba, lp["alog"], lp["dtb"], batch, seq, 0)
    dn_b = _deltanet(qkv, ba, lp["alog"], lp["dtb"], batch, seq, 1)
    sc = _sc_mix(pj, lp["sc_conv"], seq)
    return _merge(x, [a[0] for a in attn], [a[1] for a in attn], dn_f, dn_b, pj, sc, lp["gate_bias"],
                  lp["dn_norm"], lp["wa"], lp["wd"], lp["ws"], lp["wo"])


def _trunk(x, layers, final_g):
    batch, seq, d = x.shape
    tables = _rope_tables(seq)
    x = x.reshape(batch * seq, d)
    for l, lp in enumerate(layers):
        x = _ffn(x, *lp["ffn1"])
        x = _mixer(x, lp, batch, seq, tables)
        x = _ffn(x, *lp["ffn2"], final_g=final_g if l == len(layers) - 1 else None)
    return x.reshape(batch, seq, d)


def kernel(x_prompt, x_sample, ffn1_norm, ffn1_w_gate, ffn1_w_up, ffn1_w_down, mix_norm, w_in, dn_conv, dn_a_log, dn_dt_bias, dn_norm, sc_conv, gate_bias, w_attn_br, w_dn_br, w_sc_br, w_out, ffn2_norm, ffn2_w_gate, ffn2_w_up, ffn2_w_down, final_norm):
    p = dict(ffn1_norm=ffn1_norm, ffn1_w_gate=ffn1_w_gate, ffn1_w_up=ffn1_w_up, ffn1_w_down=ffn1_w_down,
             mix_norm=mix_norm, w_in=w_in, dn_conv=dn_conv, dn_a_log=dn_a_log, dn_dt_bias=dn_dt_bias,
             dn_norm=dn_norm, sc_conv=sc_conv, gate_bias=gate_bias, w_attn_br=w_attn_br, w_dn_br=w_dn_br,
             w_sc_br=w_sc_br, w_out=w_out, ffn2_norm=ffn2_norm, ffn2_w_gate=ffn2_w_gate, ffn2_w_up=ffn2_w_up,
             ffn2_w_down=ffn2_w_down)
    layers = [_prep_layer(l, p) for l in range(ffn1_norm.shape[0])]
    final_g = final_norm.reshape(1, -1)
    return _trunk(x_prompt, layers, final_g), _trunk(x_sample, layers, final_g)
```

```python
import functools
import math

import jax
import jax.numpy as jnp
from jax import lax
from jax.experimental import pallas as pl
from jax.experimental.pallas import tpu as pltpu

F32 = jnp.float32
BF16 = jnp.bfloat16

D_MODEL = 1024
HEAD_DIM = 64
ATTN_GROUPS = ((128, 1), (512, 4), (2048, 16))
ATTN_GROUP_HEADS = 4
ATTN_GW = ATTN_GROUP_HEADS * HEAD_DIM
ATTN_WIDTH = 3 * ATTN_GW
ROT_DIM = HEAD_DIM // 4
ROPE_THETA = 500000.0
MASK_VALUE = -1e30
DN_HEADS = 6
DN_HEAD_DIM = 128
DN_WIDTH = DN_HEADS * DN_HEAD_DIM
DN_CONV = 5
DN_CHUNK = 64
SC_WIDTH = 768
SC_CONV = 3
FFN_DIM = 2816
N_BRANCH = 3
NORM_EPS = 1e-6

MXU_N = 256
PJ_DN = 0
PJ_Z = PJ_DN + 3 * DN_WIDTH
PJ_GATE = PJ_Z + DN_WIDTH
PJ_SC = PJ_GATE + N_BRANCH * D_MODEL
PJ_W = PJ_SC + 3 * SC_WIDTH
BA_W = 128
BA_G0 = 2 * DN_HEADS

DN_AHEAD = 4
ATTN_UNITS = 4
HALO = 8
HALO_BLK = 16

VMEM_LIMIT = 56 * 1024 * 1024


def _cparams(sem):
    return pltpu.CompilerParams(dimension_semantics=sem, vmem_limit_bytes=VMEM_LIMIT)


def _resident(shape):
    return pl.BlockSpec(shape, lambda *_: (0,) * len(shape), pipeline_mode=pl.Buffered(1))


def _rms(x, g):
    return x * lax.rsqrt(jnp.mean(x * x, axis=-1, keepdims=True) + NORM_EPS) * g


def _dot(a, b):
    return jnp.dot(a, b, preferred_element_type=F32)


def _dot_nt(a, b):
    return lax.dot_general(a, b, (((1,), (1,)), ((), ())), preferred_element_type=F32)


def _dot_exact_lhs(sel, x):
    x1 = x.astype(BF16)
    r = x - x1.astype(F32)
    x2 = r.astype(BF16)
    x3 = (r - x2.astype(F32)).astype(BF16)
    return _dot(sel, x1) + _dot(sel, x2) + _dot(sel, x3)


def _ffn_kernel(x_ref, g_ref, wg_ref, wu_ref, wd_ref, fg_ref, o_ref, *, tf, final):
    x = x_ref[...]
    xn = _rms(x, g_ref[...]).astype(BF16)
    acc = jnp.zeros_like(x)
    for c in range(FFN_DIM // tf):
        sl = slice(c * tf, (c + 1) * tf)
        hg = _dot(xn, wg_ref[:, sl])
        hu = _dot(xn, wu_ref[:, sl])
        h = (hg * jax.nn.sigmoid(hg) * hu).astype(BF16)
        acc = acc + _dot(h, wd_ref[sl, :])
    y = x + 0.5 * acc
    if final:
        y = _rms(y, fg_ref[...])
    o_ref[...] = y


def _ffn(x, g, wg, wu, wd, final_g=None, *, tm=512, tf=256):
    m, d = x.shape
    final = final_g is not None
    fg = final_g if final else g
    return pl.pallas_call(
        functools.partial(_ffn_kernel, tf=tf, final=final),
        grid=(m // tm,),
        in_specs=[
            pl.BlockSpec((tm, d), lambda i: (i, 0)),
            _resident((1, d)),
            _resident((d, FFN_DIM)),
            _resident((d, FFN_DIM)),
            _resident((FFN_DIM, d)),
            _resident((1, d)),
        ],
        out_specs=pl.BlockSpec((tm, d), lambda i: (i, 0)),
        out_shape=jax.ShapeDtypeStruct((m, d), F32),
        compiler_params=_cparams(("parallel",)),
        name="ffn",
    )(x, g, wg, wu, wd, fg)


def _attn_proj_kernel(x_ref, g_ref, w_ref, cos_ref, sin_ref, at_ref):
    xn = _rms(x_ref[...], g_ref[...]).astype(BF16)
    for c in range(3 * ATTN_WIDTH // MXU_N):
        y = _dot(xn, w_ref[:, c * MXU_N:(c + 1) * MXU_N])
        sect = c * MXU_N // ATTN_WIDTH
        for s in range(MXU_N // 128):
            sl = slice(c * MXU_N + s * 128, c * MXU_N + (s + 1) * 128)
            ys = y[:, s * 128:(s + 1) * 128]
            if sect < 2:
                ys = ys * cos_ref[sect] + pltpu.roll(ys, HEAD_DIM, 1) * sin_ref[sect]
            at_ref[:, sl] = ys


def _attn_proj(x, g, w, cos_t, sin_t, seq, *, tm=512):
    m, d = x.shape
    nseq = seq // tm
    n = 3 * ATTN_WIDTH
    return pl.pallas_call(
        _attn_proj_kernel,
        grid=(m // tm,),
        in_specs=[
            pl.BlockSpec((tm, d), lambda i: (i, 0)),
            _resident((1, d)),
            _resident((d, n)),
            pl.BlockSpec((2, tm, 128), lambda i: (0, i % nseq, 0)),
            pl.BlockSpec((2, tm, 128), lambda i: (0, i % nseq, 0)),
        ],
        out_specs=pl.BlockSpec((tm, n), lambda i: (i, 0)),
        out_shape=jax.ShapeDtypeStruct((m, n), F32),
        compiler_params=_cparams(("parallel",)),
        name="attn_proj",
    )(x, g, w, cos_t, sin_t)


def _proj_kernel(x_ref, g_ref, w_ref, wba_ref, pj_ref, ba_ref):
    xn = _rms(x_ref[...], g_ref[...]).astype(BF16)
    ba_ref[...] = _dot(xn, wba_ref[...])
    for c in range(PJ_W // MXU_N):
        cs = slice(c * MXU_N, (c + 1) * MXU_N)
        pj_ref[:, cs] = _dot(xn, w_ref[:, cs]).astype(BF16)


def _proj(x, g, w, w_ba, *, tm=512):
    m, d = x.shape
    return pl.pallas_call(
        _proj_kernel,
        grid=(m // tm,),
        in_specs=[
            pl.BlockSpec((tm, d), lambda i: (i, 0)),
            _resident((1, d)),
            _resident((d, PJ_W)),
            _resident((d, BA_W)),
        ],
        out_specs=[
            pl.BlockSpec((tm, PJ_W), lambda i: (i, 0)),
            pl.BlockSpec((tm, BA_W), lambda i: (i, 0)),
        ],
        out_shape=[jax.ShapeDtypeStruct((m, PJ_W), BF16), jax.ShapeDtypeStruct((m, BA_W), F32)],
        compiler_params=_cparams(("parallel",)),
        name="proj",
    )(x, g, w, w_ba)


def _rope_tables(seq):
    half = ROT_DIM // 2
    inv_freq = ROPE_THETA ** (-2.0 * jnp.arange(half, dtype=F32) / ROT_DIM)
    ang = jnp.arange(seq, dtype=F32)[:, None] * inv_freq[None, :]
    cos, sin = jnp.cos(ang), jnp.sin(ang)
    ones = jnp.ones((seq, HEAD_DIM - ROT_DIM), F32)
    cos_h = jnp.concatenate([cos, cos, ones, cos, cos, ones], axis=1)
    sin_h = jnp.concatenate([-sin, -sin, 0.0 * ones, sin, sin, 0.0 * ones], axis=1)
    qs = HEAD_DIM ** -0.5
    return jnp.stack([cos_h * qs, cos_h]), jnp.stack([sin_h * qs, sin_h])


def _qk_perm():
    a = jnp.arange
    half = ROT_DIM // 2
    rest = (HEAD_DIM - ROT_DIM) // 2
    lo = [a(0, half), a(HEAD_DIM, HEAD_DIM + half), a(ROT_DIM, ROT_DIM + rest),
          a(HEAD_DIM + ROT_DIM, HEAD_DIM + ROT_DIM + rest)]
    hi = [a(half, ROT_DIM), a(HEAD_DIM + half, HEAD_DIM + ROT_DIM), a(ROT_DIM + rest, HEAD_DIM),
          a(HEAD_DIM + ROT_DIM + rest, 2 * HEAD_DIM)]
    return jnp.concatenate(lo + hi)


def _qk_first_head(lane):
    half = ROT_DIM // 2
    rest = (HEAD_DIM - ROT_DIM) // 2
    l = lane % HEAD_DIM
    return (l < half) | ((l >= ROT_DIM) & (l < ROT_DIM + rest))


def _attn_kernel(q_ref, k_ref, v_ref, o_ref, lse_ref, kres_ref, vres_ref, *, dil, tq, tk, half, length, nsub):
    i = pl.program_id(2)

    @pl.when(i == 0)
    def _():
        for r in range(dil):
            rows = pl.ds(r, length, stride=dil) if dil > 1 else pl.ds(0, length)
            kres_ref[r * length:(r + 1) * length, :] = k_ref[0, rows, :].astype(BF16)
            vres_ref[r * length:(r + 1) * length, :] = v_ref[0, rows, :].astype(BF16)

    lane = lax.broadcasted_iota(jnp.int32, (1, 128), 1)
    first_out = lane < HEAD_DIM
    first_qk = _qk_first_head(lane)
    qiota = lax.broadcasted_iota(jnp.int32, (tq, 1), 0)
    kiota = lax.broadcasted_iota(jnp.int32, (1, tk), 1)

    def group(g, carry):
        rows, mask, q, kw, vw = [], [], [], [], []
        for j in range(ATTN_UNITS):
            u = g * ATTN_UNITS + j
            r = u % dil
            sub = u // dil
            blk = i * nsub + sub
            ks = pl.multiple_of(jnp.clip(blk * tq - half, 0, length - tk), 64)
            mask.append(jnp.abs(blk * tq + qiota - ks - kiota) <= half)
            if dil > 1:
                rows.append(pl.ds(r + sub * (tq * dil), tq, stride=dil))
            else:
                rows.append(pl.ds(pl.multiple_of(sub * tq, tq), tq))
            q.append(q_ref[0, rows[j], :].astype(BF16))
            kbase = pl.multiple_of(r * length + ks, 64)
            kw.append(kres_ref[pl.ds(kbase, tk), :])
            vw.append(vres_ref[pl.ds(kbase, tk), :])
        chains = [(j, sel) for j in range(ATTN_UNITS) for sel in (first_qk, ~first_qk)]
        sc = [_dot_nt(jnp.where(sel, q[j], jnp.zeros_like(q[j])), kw[j]) for j, sel in chains]
        sc = [jnp.where(mask[j], s, MASK_VALUE) for (j, _), s in zip(chains, sc)]
        mx = [jnp.max(s, axis=-1, keepdims=True) for s in sc]
        p = [jnp.exp(s - m) for s, m in zip(sc, mx)]
        den = [jnp.sum(t, axis=-1, keepdims=True) for t in p]
        o = [_dot(t.astype(BF16), vw[j]) / d for (j, _), t, d in zip(chains, p, den)]
        lse = [m + jnp.log(d) for m, d in zip(mx, den)]
        for j in range(ATTN_UNITS):
            o_ref[0, rows[j], :] = jnp.where(first_out, o[2 * j], o[2 * j + 1])
            lse_ref[0, rows[j], :] = jnp.where(first_out, lse[2 * j], lse[2 * j + 1])
        return carry

    lax.fori_loop(0, dil * nsub // ATTN_UNITS, group, 0)


def _attn_group(at, batch, seq, gi):
    window, dil = ATTN_GROUPS[gi]
    half = (window // 2) // dil
    length = seq // dil
    tq = 2 * half
    tk = tq + 2 * half
    nsub = max(1, 2 * ATTN_UNITS // dil)
    assert half == 64 and seq % dil == 0 and tk <= length and length % (tq * nsub) == 0
    assert (dil * nsub) % ATTN_UNITS == 0
    tb = tq * dil * nsub
    at3 = at.reshape(batch, seq, 3 * ATTN_WIDTH)
    nslab = ATTN_GW // 128
    nb = ATTN_WIDTH // 128
    o, lse = pl.pallas_call(
        functools.partial(_attn_kernel, dil=dil, tq=tq, tk=tk, half=half, length=length, nsub=nsub),
        grid=(batch, nslab, seq // tb),
        in_specs=[
            pl.BlockSpec((1, tb, 128), lambda b, s, i: (b, i, gi * nslab + s)),
            pl.BlockSpec((1, seq, 128), lambda b, s, i: (b, 0, nb + gi * nslab + s), pipeline_mode=pl.Buffered(1)),
            pl.BlockSpec((1, seq, 128), lambda b, s, i: (b, 0, 2 * nb + gi * nslab + s), pipeline_mode=pl.Buffered(1)),
        ],
        out_specs=[
            pl.BlockSpec((1, tb, 128), lambda b, s, i: (b, i, s)),
            pl.BlockSpec((1, tb, 128), lambda b, s, i: (b, i, s)),
        ],
        out_shape=[jax.ShapeDtypeStruct((batch, seq, ATTN_GW), F32)] * 2,
        scratch_shapes=[pltpu.VMEM((seq, 128), BF16), pltpu.VMEM((seq, 128), BF16)],
        compiler_params=_cparams(("parallel", "parallel", "arbitrary")),
        name=f"attn{gi}",
    )(at3, at3, at3)
    return o.reshape(batch * seq, ATTN_GW), lse.reshape(batch * seq, ATTN_GW)


def _halo_tile(prev_ref, main_ref, next_ref, first, last, sl):
    xm = main_ref[:, sl].astype(F32)
    xp = prev_ref[:, sl].astype(F32)[HALO_BLK - HALO:]
    xn = next_ref[:, sl].astype(F32)[:HALO]
    xp = jnp.where(first, 0.0, xp)
    xn = jnp.where(last, 0.0, xn)
    return jnp.concatenate([xp, xm, xn], axis=0)


def _tap(ext, off, t):
    n = ext.shape[0]
    return pltpu.roll(ext, (-off) % n, 0)[HALO:HALO + t] if off else ext[HALO:HALO + t]


def _seq_edges(i, t, seq):
    p0 = (i * t) % seq
    return p0 == 0, p0 + t == seq


def _dn_prep_kernel(prev_ref, main_ref, next_ref, cw_ref, o_ref, *, t, seq):
    first, last = _seq_edges(pl.program_id(0), t, seq)
    for s in range(3 * DN_HEADS):
        sl = slice(s * DN_HEAD_DIM, (s + 1) * DN_HEAD_DIM)
        ext = _halo_tile(prev_ref, main_ref, next_ref, first, last, sl)
        acc = None
        for tap in range(DN_CONV):
            term = _tap(ext, tap - DN_CONV // 2, t) * cw_ref[tap:tap + 1, sl]
            acc = term if acc is None else acc + term
        y = acc * jax.nn.sigmoid(acc)
        if s < 2 * DN_HEADS:
            scale = DN_HEAD_DIM ** -0.5 if s < DN_HEADS else 1.0
            y = y * (lax.rsqrt(jnp.sum(y * y, axis=-1, keepdims=True) + NORM_EPS) * scale)
        o_ref[:, sl] = y.astype(BF16)


def _halo_specs(t, m, colblk, width):
    nb = m // HALO_BLK
    r = t // HALO_BLK
    return [
        pl.BlockSpec((HALO_BLK, width), lambda i: (jnp.maximum(i * r - 1, 0), colblk)),
        pl.BlockSpec((t, width), lambda i: (i, colblk)),
        pl.BlockSpec((HALO_BLK, width), lambda i: (jnp.minimum((i + 1) * r, nb - 1), colblk)),
    ]


def _dn_prep(pj, dn_conv, seq, *, t=512):
    m = pj.shape[0]
    w = 3 * DN_WIDTH
    return pl.pallas_call(
        functools.partial(_dn_prep_kernel, t=t, seq=seq),
        grid=(m // t,),
        in_specs=_halo_specs(t, m, PJ_DN // w, w) + [_resident((DN_CONV, w))],
        out_specs=pl.BlockSpec((t, w), lambda i: (i, 0)),
        out_shape=jax.ShapeDtypeStruct((m, w), BF16),
        compiler_params=_cparams(("parallel",)),
        name="dn_prep",
    )(pj, pj, pj, dn_conv)


def _sc_kernel(b_ref, cp_ref, cm_ref, cn_ref, xp_ref, xm_ref, xn_ref, cw_ref, o_ref, *, t, seq):
    first, last = _seq_edges(pl.program_id(0), t, seq)
    for s in range(SC_WIDTH // 128):
        sl = slice(s * 128, (s + 1) * 128)
        ext = (_halo_tile(cp_ref, cm_ref, cn_ref, first, last, sl)
               * _halo_tile(xp_ref, xm_ref, xn_ref, first, last, sl))
        acc = None
        for tap in range(SC_CONV):
            term = _tap(ext, tap - SC_CONV // 2, t) * cw_ref[tap:tap + 1, sl]
            acc = term if acc is None else acc + term
        o_ref[:, sl] = (b_ref[:, sl].astype(F32) * acc).astype(BF16)


def _sc_mix(pj, sc_conv, seq, *, t=512):
    m = pj.shape[0]
    w = SC_WIDTH
    cb = PJ_SC // w
    return pl.pallas_call(
        functools.partial(_sc_kernel, t=t, seq=seq),
        grid=(m // t,),
        in_specs=[pl.BlockSpec((t, w), lambda i: (i, cb))]
        + _halo_specs(t, m, cb + 1, w)
        + _halo_specs(t, m, cb + 2, w)
        + [_resident((SC_CONV, w))],
        out_specs=pl.BlockSpec((t, w), lambda i: (i, 0)),
        out_shape=jax.ShapeDtypeStruct((m, w), BF16),
        compiler_params=_cparams(("parallel",)),
        name="sc_mix",
    )(pj, pj, pj, pj, pj, pj, pj, sc_conv)


def _dn_kernel(q_ref, k_ref, v_ref, ba_ref, alog_ref, dtb_ref, tri_ref, o_ref,
               st_ref, beta_ref, gc_ref, gct_ref, *slot_refs, ct, reverse, lane0):
    c = DN_CHUNK
    nchunk = ct // c
    heads = range(DN_HEADS)
    hsl = [slice(h * DN_HEAD_DIM, (h + 1) * DN_HEAD_DIM) for h in heads]
    nslot = 2 * DN_AHEAD
    u_refs, wq_refs, upd_refs = slot_refs[:nslot], slot_refs[nslot:2 * nslot], slot_refs[2 * nslot:]

    @pl.when(pl.program_id(1) == 0)
    def _():
        st_ref[...] = jnp.zeros_like(st_ref)

    ba = ba_ref[0]
    beta_all = jax.nn.sigmoid(ba)
    xg = ba + dtb_ref[...]
    softplus = jnp.maximum(xg, 0.0) + jnp.log1p(jnp.exp(-jnp.abs(xg)))
    g_all = -jnp.exp(alog_ref[...]) * softplus
    gc_all = _dot_exact_lhs(tri_ref[...], g_all)
    for h in heads:
        lb = lane0 - BA_G0 + h
        beta_ref[:, hsl[h]] = jnp.broadcast_to(beta_all[:, lb:lb + 1], (ct, DN_HEAD_DIM))
        gc_ref[:, hsl[h]] = jnp.broadcast_to(gc_all[:, lane0 + h:lane0 + h + 1], (ct, DN_HEAD_DIM))
    for n in range(nchunk):
        gct_ref[n] = gc_all[n * c:(n + 1) * c, :].T

    row = lax.broadcasted_iota(jnp.int32, (c, c), 0)
    col = lax.broadcasted_iota(jnp.int32, (c, c), 1)
    incl = (col >= row) if reverse else (col <= row)
    strict = (col > row) if reverse else (col < row)
    upper = lax.broadcasted_iota(jnp.int32, (c, 2 * c), 1) >= c
    eye_hi = (lax.broadcasted_iota(jnp.int32, (c, 2 * c), 1) - c
              == lax.broadcasted_iota(jnp.int32, (c, 2 * c), 0)).astype(F32)

    def local(chunks):
        units = [(n, slot, h) for n, slot in chunks for h in heads]
        idx = range(len(units))
        rows = [slice(n * c, (n + 1) * c) for n, _, _ in units]
        last = [n * c + (0 if reverse else c - 1) for n, _, _ in units]
        hs = [hsl[h] for _, _, h in units]
        k16 = [k_ref[0, rows[i], hs[i]] for i in idx]
        q = [q_ref[0, rows[i], hs[i]].astype(F32) for i in idx]
        v = [v_ref[0, rows[i], hs[i]].astype(F32) for i in idx]
        beta = [beta_ref[rows[i], hs[i]] for i in idx]
        gcx = [gc_ref[rows[i], hs[i]] for i in idx]
        gtot = [gc_ref[last[i]:last[i] + 1, hs[i]] for i in idx]
        grow = [gct_ref[n, lane0 + h:lane0 + h + 1, :] for n, _, h in units]
        k = [t.astype(F32) for t in k16]
        kb = [k[i] * beta[i] for i in idx]
        kq = [_dot_nt(jnp.concatenate([kb[i], q[i]], axis=0).astype(BF16), k16[i]) for i in idx]
        decay = [jnp.where(incl, jnp.exp(jnp.where(incl, gcx[i][:, :c] - grow[i], 0.0)), 0.0) for i in idx]
        intra = [jnp.where(incl, kq[i][c:] * decay[i], 0.0) for i in idx]
        mk = [jnp.where(strict, -kq[i][:c] * decay[i], 0.0) for i in idx]
        z = [jnp.concatenate([mk[i], jnp.zeros_like(mk[i])], axis=1) + eye_hi for i in idx]
        for _ in range(int(math.log2(c))):
            y = [_dot(z[i][:, :c].astype(BF16), z[i].astype(BF16)) for i in idx]
            z = [y[i] + jnp.where(upper, z[i], 0.0) for i in idx]
        eg = [jnp.exp(gcx[i]) for i in idx]
        rhs = [jnp.concatenate([v[i] * beta[i], kb[i] * eg[i]], axis=1).astype(BF16) for i in idx]
        rhs = [jnp.concatenate([jnp.zeros_like(t), t], axis=0) for t in rhs]
        sol = [_dot(z[i].astype(BF16), rhs[i]) for i in idx]
        kdec_t = [(k[i].T * jnp.exp(gtot[i][:, :c] - grow[i])).astype(BF16) for i in idx]
        for i, (_, slot, h) in enumerate(units):
            u_refs[slot][h] = sol[i][:, :DN_HEAD_DIM]
            wq_refs[slot][h, :c] = sol[i][:, DN_HEAD_DIM:].astype(BF16)
            wq_refs[slot][h, c:] = (q[i] * eg[i]).astype(BF16)
            upd_refs[slot][h, :c] = intra[i].astype(BF16)
            upd_refs[slot][h, c:] = kdec_t[i]

    def recur(n, slot, state):
        rows = slice(n * c, (n + 1) * c)
        last = n * c + (0 if reverse else c - 1)
        etot = [jnp.exp(gc_ref[last:last + 1, hsl[h]]) for h in heads]
        wq = [_dot(wq_refs[slot][h], state[h].astype(BF16)) for h in heads]
        v_new = [(u_refs[slot][h] - wq[h][:c]).astype(BF16) for h in heads]
        upd = [_dot(upd_refs[slot][h], v_new[h]) for h in heads]
        for h in heads:
            o_ref[0, rows, hsl[h]] = wq[h][c:] + upd[h][:c]
        return [state[h] * etot[h] + upd[h][c:] for h in heads]

    order = list(range(nchunk))[::-1] if reverse else list(range(nchunk))
    groups = [order[i:i + DN_AHEAD] for i in range(0, nchunk, DN_AHEAD)]
    slots = lambda gi: [(n, (gi % 2) * DN_AHEAD + j) for j, n in enumerate(groups[gi])]
    state = [st_ref[h] for h in heads]
    local(slots(0))
    for gi in range(len(groups)):
        if gi + 1 < len(groups):
            local(slots(gi + 1))
        for n, slot in slots(gi):
            state = recur(n, slot, state)
    for h in heads:
        st_ref[h] = state[h]


def _dn_tri(ct, reverse):
    c = DN_CHUNK
    r = jnp.arange(ct)
    same = (r[:, None] // c) == (r[None, :] // c)
    tri = same & ((r[None, :] >= r[:, None]) if reverse else (r[None, :] <= r[:, None]))
    return tri.astype(BF16)


def _deltanet(qkv, ba, alog_row, dtb_row, batch, seq, direction, *, ct=512):
    reverse = direction == 1
    nstep = seq // ct
    w = DN_WIDTH
    c = DN_CHUNK
    qkv3 = qkv.reshape(batch, seq, 3 * w)
    ba3 = ba.reshape(batch, seq, BA_W)
    step = (lambda n: nstep - 1 - n) if reverse else (lambda n: n)
    out = pl.pallas_call(
        functools.partial(_dn_kernel, ct=ct, reverse=reverse, lane0=BA_G0 + direction * DN_HEADS),
        grid=(batch, nstep),
        in_specs=[
            pl.BlockSpec((1, ct, w), lambda b, n: (b, step(n), 0)),
            pl.BlockSpec((1, ct, w), lambda b, n: (b, step(n), 1)),
            pl.BlockSpec((1, ct, w), lambda b, n: (b, step(n), 2)),
            pl.BlockSpec((1, ct, BA_W), lambda b, n: (b, step(n), 0)),
            _resident((1, BA_W)),
            _resident((1, BA_W)),
            _resident((ct, ct)),
        ],
        out_specs=pl.BlockSpec((1, ct, w), lambda b, n: (b, step(n), 0)),
        out_shape=jax.ShapeDtypeStruct((batch, seq, w), F32),
        scratch_shapes=[
            pltpu.VMEM((DN_HEADS, DN_HEAD_DIM, DN_HEAD_DIM), F32),
            pltpu.VMEM((ct, w), F32),
            pltpu.VMEM((ct, w), F32),
            pltpu.VMEM((ct // c, BA_W, c), F32),
        ]
        + [pltpu.VMEM((DN_HEADS, c, DN_HEAD_DIM), F32)] * (2 * DN_AHEAD)
        + [pltpu.VMEM((DN_HEADS, 2 * c, DN_HEAD_DIM), BF16)] * (2 * DN_AHEAD)
        + [pltpu.VMEM((DN_HEADS, c + DN_HEAD_DIM, c), BF16)] * (2 * DN_AHEAD),
        compiler_params=_cparams(("parallel", "arbitrary")),
        name=f"deltanet{direction}",
    )(qkv3, qkv3, qkv3, ba3, alog_row, dtb_row, _dn_tri(ct, reverse))
    return out.reshape(batch * seq, w)


def _merge_kernel(x_ref, o1_ref, o2_ref, o3_ref, l1_ref, l2_ref, l3_ref, df_ref, db_ref, z_ref, sc_ref, gate_ref,
                  gb_ref, dng_ref, wa_ref, wd_ref, ws_ref, wo_ref, out_ref):
    l1, l2, l3 = l1_ref[...], l2_ref[...], l3_ref[...]
    mx = jnp.maximum(jnp.maximum(l1, l2), l3)
    e1, e2, e3 = jnp.exp(l1 - mx), jnp.exp(l2 - mx), jnp.exp(l3 - mx)
    attn = (o1_ref[...] * e1 + o2_ref[...] * e2 + o3_ref[...] * e3) / (e1 + e2 + e3)
    y_attn = _dot(attn.astype(BF16), wa_ref[...])
    dn = df_ref[...] + db_ref[...]
    dng = dng_ref[...]
    parts = []
    for h in range(DN_HEADS):
        hs = slice(h * DN_HEAD_DIM, (h + 1) * DN_HEAD_DIM)
        oh = dn[:, hs]
        oh = oh * lax.rsqrt(jnp.mean(oh * oh, axis=-1, keepdims=True) + NORM_EPS) * dng
        zh = z_ref[:, hs].astype(F32)
        parts.append((oh * (zh * jax.nn.sigmoid(zh))).astype(BF16))
    y_dn = _dot(jnp.concatenate(parts, axis=1), wd_ref[...])
    y_sc = _dot(sc_ref[...], ws_ref[...])
    gates = jax.nn.sigmoid(gate_ref[...].astype(F32) + gb_ref[...])
    d = D_MODEL
    merged = gates[:, :d] * y_attn + gates[:, d:2 * d] * y_dn + gates[:, 2 * d:] * y_sc
    out_ref[...] = x_ref[...] + _dot(merged.astype(BF16), wo_ref[...])


def _merge(x, attn_o, attn_l, dn_f, dn_b, pj, sc, gate_bias, dn_norm, wa, wd, ws, wo, *, tm=512):
    m, d = x.shape
    row = lambda i: (i, 0)
    gw = N_BRANCH * d
    return pl.pallas_call(
        _merge_kernel,
        grid=(m // tm,),
        in_specs=[pl.BlockSpec((tm, d), row)]
        + [pl.BlockSpec((tm, ATTN_GW), row)] * 6
        + [pl.BlockSpec((tm, DN_WIDTH), row)] * 2
        + [
            pl.BlockSpec((tm, DN_WIDTH), lambda i: (i, PJ_Z // DN_WIDTH)),
            pl.BlockSpec((tm, SC_WIDTH), row),
            pl.BlockSpec((tm, gw), lambda i: (i, PJ_GATE // gw)),
            _resident((1, gw)),
            _resident((1, DN_HEAD_DIM)),
            _resident((ATTN_GW, d)),
            _resident((DN_WIDTH, d)),
            _resident((SC_WIDTH, d)),
            _resident((d, d)),
        ],
        out_specs=pl.BlockSpec((tm, d), row),
        out_shape=jax.ShapeDtypeStruct((m, d), F32),
        compiler_params=_cparams(("parallel",)),
        name="merge",
    )(x, *attn_o, *attn_l, dn_f, dn_b, pj, sc, pj, gate_bias, dn_norm, wa, wd, ws, wo)


def _prep_layer(l, p):
    w_in = p["w_in"][l]
    o_attn, o_dn = 0, 3 * ATTN_WIDTH
    o_z = o_dn + 3 * DN_WIDTH
    o_beta = o_z + DN_WIDTH
    o_sc = o_beta + 4 * DN_HEADS
    o_gate = o_sc + 3 * SC_WIDTH
    tiles = lambda w: w.astype(BF16)
    w_qk = w_in[:, o_attn:o_attn + 2 * ATTN_WIDTH].reshape(D_MODEL, -1, 128)[:, :, _qk_perm()]
    w_attn = tiles(jnp.concatenate([w_qk.reshape(D_MODEL, -1), w_in[:, o_attn + 2 * ATTN_WIDTH:o_dn]], axis=1))
    w_tiles = tiles(jnp.concatenate([w_in[:, o_dn:o_z], w_in[:, o_z:o_beta], w_in[:, o_gate:], w_in[:, o_sc:o_gate]], axis=1))
    w_ba = jnp.pad(w_in[:, o_beta:o_sc], ((0, 0), (0, BA_W - 4 * DN_HEADS)))
    pad_row = lambda t: jnp.pad(t.reshape(1, 2 * DN_HEADS), ((0, 0), (BA_G0, BA_W - BA_G0 - 2 * DN_HEADS)))
    bf = lambda t: t.astype(BF16)
    row = lambda t: t.reshape(1, -1)
    return dict(
        ffn1=(row(p["ffn1_norm"][l]), bf(p["ffn1_w_gate"][l]), bf(p["ffn1_w_up"][l]), bf(p["ffn1_w_down"][l])),
        ffn2=(row(p["ffn2_norm"][l]), bf(p["ffn2_w_gate"][l]), bf(p["ffn2_w_up"][l]), bf(p["ffn2_w_down"][l])),
        mix_norm=row(p["mix_norm"][l]),
        w_attn=w_attn,
        w_tiles=w_tiles,
        w_ba=bf(w_ba),
        dn_conv=p["dn_conv"][l],
        alog=pad_row(p["dn_a_log"][l]),
        dtb=pad_row(p["dn_dt_bias"][l]),
        dn_norm=row(p["dn_norm"][l]),
        sc_conv=p["sc_conv"][l],
        gate_bias=row(p["gate_bias"][l]),
        wa=bf(p["w_attn_br"][l]),
        wd=bf(p["w_dn_br"][l]),
        ws=bf(p["w_sc_br"][l]),
        wo=bf(p["w_out"][l]),
    )


def _mixer(x, lp, batch, seq, tables):
    at = _attn_proj(x, lp["mix_norm"], lp["w_attn"], tables[0], tables[1], seq)
    pj, ba = _proj(x, lp["mix_norm"], lp["w_tiles"], lp["w_ba"])
    attn = [_attn_group(at, batch, seq, gi) for gi in range(len(ATTN_GROUPS))]
    qkv = _dn_prep(pj, lp["dn_conv"], seq)
    dn_f = _deltanet(qkv, ba, lp["alog"], lp["dtb"], batch, seq, 0)
    dn_b = _deltanet(qkv, ba, lp["alog"], lp["dtb"], batch, seq, 1)
    sc = _sc_mix(pj, lp["sc_conv"], seq)
    return _merge(x, [a[0] for a in attn], [a[1] for a in attn], dn_f, dn_b, pj, sc, lp["gate_bias"],
                  lp["dn_norm"], lp["wa"], lp["wd"], lp["ws"], lp["wo"])


def _trunk(x, layers, final_g):
    batch, seq, d = x.shape
    tables = _rope_tables(seq)
    x = x.reshape(batch * seq, d)
    for l, lp in enumerate(layers):
        x = _ffn(x, *lp["ffn1"])
        x = _mixer(x, lp, batch, seq, tables)
        x = _ffn(x, *lp["ffn2"], final_g=final_g if l == len(layers) - 1 else None)
    return x.reshape(batch, seq, d)


def kernel(x_prompt, x_sample, ffn1_norm, ffn1_w_gate, ffn1_w_up, ffn1_w_down, mix_norm, w_in, dn_conv, dn_a_log, dn_dt_bias, dn_norm, sc_conv, gate_bias, w_attn_br, w_dn_br, w_sc_br, w_out, ffn2_norm, ffn2_w_gate, ffn2_w_up, ffn2_w_down, final_norm):
    p = dict(ffn1_norm=ffn1_norm, ffn1_w_gate=ffn1_w_gate, ffn1_w_up=ffn1_w_up, ffn1_w_down=ffn1_w_down,
             mix_norm=mix_norm, w_in=w_in, dn_conv=dn_conv, dn_a_log=dn_a_log, dn_dt_bias=dn_dt_bias,
             dn_norm=dn_norm, sc_conv=sc_conv, gate_bias=gate_bias, w_attn_br=w_attn_br, w_dn_br=w_dn_br,
             w_sc_br=w_sc_br, w_out=w_out, ffn2_norm=ffn2_norm, ffn2_w_gate=ffn2_w_gate, ffn2_w_up=ffn2_w_up,
             ffn2_w_down=ffn2_w_down)
    layers = [_prep_layer(l, p) for l in range(ffn1_norm.shape[0])]
    final_g = final_norm.reshape(1, -1)
    return _trunk(x_prompt, layers, final_g), _trunk(x_sample, layers, final_g)
```

```python
import functools
import math

import jax
import jax.numpy as jnp
from jax import lax
from jax.experimental import pallas as pl
from jax.experimental.pallas import tpu as pltpu

F32 = jnp.float32
BF16 = jnp.bfloat16

D_MODEL = 1024
HEAD_DIM = 64
ATTN_GROUPS = ((128, 1), (512, 4), (2048, 16))
ATTN_GROUP_HEADS = 4
ATTN_GW = ATTN_GROUP_HEADS * HEAD_DIM
ATTN_WIDTH = 3 * ATTN_GW
ROT_DIM = HEAD_DIM // 4
ROPE_THETA = 500000.0
MASK_VALUE = -1e30
DN_HEADS = 6
DN_HEAD_DIM = 128
DN_WIDTH = DN_HEADS * DN_HEAD_DIM
DN_CONV = 5
DN_CHUNK = 64
SC_WIDTH = 768
SC_CONV = 3
FFN_DIM = 2816
N_BRANCH = 3
NORM_EPS = 1e-6

MXU_N = 256
PJ_GATE = 0
PJ_Z = PJ_GATE + N_BRANCH * D_MODEL
PJ_SC = PJ_Z + DN_WIDTH
PJ_W = PJ_SC + 3 * SC_WIDTH
BA_W = 128
BA_G0 = 2 * DN_HEADS

DN_AHEAD = 4
ATTN_UNITS = 4
HALO = 8
HALO_BLK = 16

VMEM_LIMIT = 56 * 1024 * 1024


def _cparams(sem):
    return pltpu.CompilerParams(dimension_semantics=sem, vmem_limit_bytes=VMEM_LIMIT)


def _resident(shape):
    return pl.BlockSpec(shape, lambda *_: (0,) * len(shape), pipeline_mode=pl.Buffered(1))


def _rms(x, g):
    return x * lax.rsqrt(jnp.mean(x * x, axis=-1, keepdims=True) + NORM_EPS) * g


def _dot(a, b):
    return jnp.dot(a, b, preferred_element_type=F32)


def _dot_nt(a, b):
    return lax.dot_general(a, b, (((1,), (1,)), ((), ())), preferred_element_type=F32)


def _dot_exact_lhs(sel, x):
    x1 = x.astype(BF16)
    r = x - x1.astype(F32)
    x2 = r.astype(BF16)
    x3 = (r - x2.astype(F32)).astype(BF16)
    return _dot(sel, x1) + _dot(sel, x2) + _dot(sel, x3)


def _ffn_kernel(x_ref, g_ref, wg_ref, wu_ref, wd_ref, fg_ref, o_ref, *, tf, final):
    x = x_ref[...]
    xn = _rms(x, g_ref[...]).astype(BF16)
    acc = jnp.zeros_like(x)
    for c in range(FFN_DIM // tf):
        sl = slice(c * tf, (c + 1) * tf)
        hg = _dot(xn, wg_ref[:, sl])
        hu = _dot(xn, wu_ref[:, sl])
        h = (hg * jax.nn.sigmoid(hg) * hu).astype(BF16)
        acc = acc + _dot(h, wd_ref[sl, :])
    y = x + 0.5 * acc
    if final:
        y = _rms(y, fg_ref[...])
    o_ref[...] = y


def _ffn(x, g, wg, wu, wd, final_g=None, *, tm=512, tf=256):
    m, d = x.shape
    final = final_g is not None
    fg = final_g if final else g
    return pl.pallas_call(
        functools.partial(_ffn_kernel, tf=tf, final=final),
        grid=(m // tm,),
        in_specs=[
            pl.BlockSpec((tm, d), lambda i: (i, 0)),
            _resident((1, d)),
            _resident((d, FFN_DIM)),
            _resident((d, FFN_DIM)),
            _resident((FFN_DIM, d)),
            _resident((1, d)),
        ],
        out_specs=pl.BlockSpec((tm, d), lambda i: (i, 0)),
        out_shape=jax.ShapeDtypeStruct((m, d), F32),
        compiler_params=_cparams(("parallel",)),
        name="ffn",
    )(x, g, wg, wu, wd, fg)


def _attn_proj_kernel(x_ref, g_ref, w_ref, cos_ref, sin_ref, at_ref):
    xn = _rms(x_ref[...], g_ref[...]).astype(BF16)
    for c in range(3 * ATTN_WIDTH // MXU_N):
        y = _dot(xn, w_ref[:, c * MXU_N:(c + 1) * MXU_N])
        sect = c * MXU_N // ATTN_WIDTH
        for s in range(MXU_N // 128):
            sl = slice(c * MXU_N + s * 128, c * MXU_N + (s + 1) * 128)
            ys = y[:, s * 128:(s + 1) * 128]
            if sect < 2:
                ys = ys * cos_ref[sect] + pltpu.roll(ys, HEAD_DIM, 1) * sin_ref[sect]
            at_ref[:, sl] = ys


def _attn_proj(x, g, w, cos_t, sin_t, seq, *, tm=512):
    m, d = x.shape
    nseq = seq // tm
    n = 3 * ATTN_WIDTH
    return pl.pallas_call(
        _attn_proj_kernel,
        grid=(m // tm,),
        in_specs=[
            pl.BlockSpec((tm, d), lambda i: (i, 0)),
            _resident((1, d)),
            _resident((d, n)),
            pl.BlockSpec((2, tm, 128), lambda i: (0, i % nseq, 0)),
            pl.BlockSpec((2, tm, 128), lambda i: (0, i % nseq, 0)),
        ],
        out_specs=pl.BlockSpec((tm, n), lambda i: (i, 0)),
        out_shape=jax.ShapeDtypeStruct((m, n), F32),
        compiler_params=_cparams(("parallel",)),
        name="attn_proj",
    )(x, g, w, cos_t, sin_t)


def _proj_kernel(x_ref, g_ref, w_ref, wba_ref, cw_ref, gb_ref, pj_ref, ba_ref, qkv_ref, new_ref, old_ref, *, t, seq):
    i = pl.program_id(0)
    dnw = 3 * DN_WIDTH

    @pl.when(i == 0)
    def _():
        new_ref[...] = jnp.zeros_like(new_ref)
        old_ref[...] = jnp.zeros_like(old_ref)

    first, last = _seq_edges(i - 1, t, seq)
    xn = _rms(x_ref[...], g_ref[...]).astype(BF16)
    ba_ref[...] = _dot(xn, wba_ref[...])
    nslab = dnw // DN_HEAD_DIM
    npj = PJ_W // MXU_N
    for s in range(nslab):
        if s % (MXU_N // DN_HEAD_DIM) == 0:
            cs = slice(s * DN_HEAD_DIM, s * DN_HEAD_DIM + MXU_N)
            new_ref[:, cs] = _dot(xn, w_ref[:, cs])
        for c in range(npj * s // nslab, npj * (s + 1) // nslab):
            cs = slice(c * MXU_N, (c + 1) * MXU_N)
            y = _dot(xn, w_ref[:, dnw + c * MXU_N:dnw + (c + 1) * MXU_N])
            if cs.stop <= PJ_Z:
                y = jax.nn.sigmoid(y + gb_ref[:, cs])
            elif cs.stop <= PJ_SC:
                y = y * jax.nn.sigmoid(y)
            pj_ref[:, cs] = y.astype(BF16)
        sl = slice(s * DN_HEAD_DIM, (s + 1) * DN_HEAD_DIM)
        ext = jnp.concatenate([jnp.where(first, 0.0, old_ref[:HALO, sl]), old_ref[HALO:, sl],
                               jnp.where(last, 0.0, new_ref[:HALO, sl])], axis=0)
        y = _conv_taps(ext, cw_ref, sl, DN_CONV, t)
        y = y * jax.nn.sigmoid(y)
        if s < 2 * DN_HEADS:
            scale = DN_HEAD_DIM ** -0.5 if s < DN_HEADS else 1.0
            y = y * (lax.rsqrt(jnp.sum(y * y, axis=-1, keepdims=True) + NORM_EPS) * scale)
        qkv_ref[:, sl] = y.astype(BF16)
        old_ref[:HALO, sl] = old_ref[t:, sl]
        old_ref[HALO:, sl] = new_ref[:, sl]


def _proj(x, g, w, w_ba, dn_conv, gate_bias, seq, *, tm=512):
    m, d = x.shape
    nt = m // tm
    dnw = 3 * DN_WIDTH
    cur_tile = lambda i: (jnp.minimum(i, nt - 1), 0)
    return pl.pallas_call(
        functools.partial(_proj_kernel, t=tm, seq=seq),
        grid=(nt + 1,),
        in_specs=[
            pl.BlockSpec((tm, d), cur_tile),
            _resident((1, d)),
            _resident((d, dnw + PJ_W)),
            _resident((d, BA_W)),
            _resident((DN_CONV, dnw)),
            _resident((1, N_BRANCH * d)),
        ],
        out_specs=[
            pl.BlockSpec((tm, PJ_W), cur_tile),
            pl.BlockSpec((tm, BA_W), cur_tile),
            pl.BlockSpec((tm, dnw), lambda i: (jnp.maximum(i - 1, 0), 0)),
        ],
        out_shape=[
            jax.ShapeDtypeStruct((m, PJ_W), BF16),
            jax.ShapeDtypeStruct((m, BA_W), F32),
            jax.ShapeDtypeStruct((m, dnw), BF16),
        ],
        scratch_shapes=[pltpu.VMEM((tm, dnw), F32), pltpu.VMEM((HALO + tm, dnw), F32)],
        compiler_params=_cparams(("arbitrary",)),
        name="proj",
    )(x, g, w, w_ba, dn_conv, gate_bias)


def _rope_tables(seq):
    half = ROT_DIM // 2
    inv_freq = ROPE_THETA ** (-2.0 * jnp.arange(half, dtype=F32) / ROT_DIM)
    ang = jnp.arange(seq, dtype=F32)[:, None] * inv_freq[None, :]
    cos, sin = jnp.cos(ang), jnp.sin(ang)
    ones = jnp.ones((seq, HEAD_DIM - ROT_DIM), F32)
    cos_h = jnp.concatenate([cos, cos, ones, cos, cos, ones], axis=1)
    sin_h = jnp.concatenate([-sin, -sin, 0.0 * ones, sin, sin, 0.0 * ones], axis=1)
    qs = HEAD_DIM ** -0.5
    return jnp.stack([cos_h * qs, cos_h]), jnp.stack([sin_h * qs, sin_h])


def _qk_perm():
    a = jnp.arange
    half = ROT_DIM // 2
    rest = (HEAD_DIM - ROT_DIM) // 2
    lo = [a(0, half), a(HEAD_DIM, HEAD_DIM + half), a(ROT_DIM, ROT_DIM + rest),
          a(HEAD_DIM + ROT_DIM, HEAD_DIM + ROT_DIM + rest)]
    hi = [a(half, ROT_DIM), a(HEAD_DIM + half, HEAD_DIM + ROT_DIM), a(ROT_DIM + rest, HEAD_DIM),
          a(HEAD_DIM + ROT_DIM + rest, 2 * HEAD_DIM)]
    return jnp.concatenate(lo + hi)


def _qk_first_head(lane):
    half = ROT_DIM // 2
    rest = (HEAD_DIM - ROT_DIM) // 2
    l = lane % HEAD_DIM
    return (l < half) | ((l >= ROT_DIM) & (l < ROT_DIM + rest))


def _attn_kernel(q_ref, k_ref, v_ref, o_ref, lse_ref, kres_ref, vres_ref, *, dil, tq, tk, half, length, nsub):
    i = pl.program_id(2)

    @pl.when(i == 0)
    def _():
        for r in range(dil):
            rows = pl.ds(r, length, stride=dil) if dil > 1 else pl.ds(0, length)
            kres_ref[r * length:(r + 1) * length, :] = k_ref[0, rows, :].astype(BF16)
            vres_ref[r * length:(r + 1) * length, :] = v_ref[0, rows, :].astype(BF16)

    lane = lax.broadcasted_iota(jnp.int32, (1, 128), 1)
    first_out = lane < HEAD_DIM
    first_qk = _qk_first_head(lane)
    qiota = lax.broadcasted_iota(jnp.int32, (tq, 1), 0)
    kiota = lax.broadcasted_iota(jnp.int32, (1, tk), 1)

    def group(g, carry):
        rows, mask, q, kw, vw = [], [], [], [], []
        for j in range(ATTN_UNITS):
            u = g * ATTN_UNITS + j
            r = u % dil
            sub = u // dil
            blk = i * nsub + sub
            ks = pl.multiple_of(jnp.clip(blk * tq - half, 0, length - tk), 64)
            mask.append(jnp.abs(blk * tq + qiota - ks - kiota) <= half)
            if dil > 1:
                rows.append(pl.ds(r + sub * (tq * dil), tq, stride=dil))
            else:
                rows.append(pl.ds(pl.multiple_of(sub * tq, tq), tq))
            q.append(q_ref[0, rows[j], :].astype(BF16))
            kbase = pl.multiple_of(r * length + ks, 64)
            kw.append(kres_ref[pl.ds(kbase, tk), :])
            vw.append(vres_ref[pl.ds(kbase, tk), :])
        chains = [(j, sel) for j in range(ATTN_UNITS) for sel in (first_qk, ~first_qk)]
        sc = [_dot_nt(jnp.where(sel, q[j], jnp.zeros_like(q[j])), kw[j]) for j, sel in chains]
        sc = [jnp.where(mask[j], s, MASK_VALUE) for (j, _), s in zip(chains, sc)]
        mx = [jnp.max(s, axis=-1, keepdims=True) for s in sc]
        p = [jnp.exp(s - m) for s, m in zip(sc, mx)]
        den = [jnp.sum(t, axis=-1, keepdims=True) for t in p]
        o = [_dot(t.astype(BF16), vw[j]) / d for (j, _), t, d in zip(chains, p, den)]
        lse = [m + jnp.log(d) for m, d in zip(mx, den)]
        for j in range(ATTN_UNITS):
            o_ref[0, rows[j], :] = jnp.where(first_out, o[2 * j], o[2 * j + 1])
            lse_ref[0, rows[j], :] = jnp.where(first_out, lse[2 * j], lse[2 * j + 1])
        return carry

    lax.fori_loop(0, dil * nsub // ATTN_UNITS, group, 0)


def _attn_group(at, batch, seq, gi):
    window, dil = ATTN_GROUPS[gi]
    half = (window // 2) // dil
    length = seq // dil
    tq = 2 * half
    tk = tq + 2 * half
    nsub = max(1, 2 * ATTN_UNITS // dil)
    assert half == 64 and seq % dil == 0 and tk <= length and length % (tq * nsub) == 0
    assert (dil * nsub) % ATTN_UNITS == 0
    tb = tq * dil * nsub
    at3 = at.reshape(batch, seq, 3 * ATTN_WIDTH)
    nslab = ATTN_GW // 128
    nb = ATTN_WIDTH // 128
    o, lse = pl.pallas_call(
        functools.partial(_attn_kernel, dil=dil, tq=tq, tk=tk, half=half, length=length, nsub=nsub),
        grid=(batch, nslab, seq // tb),
        in_specs=[
            pl.BlockSpec((1, tb, 128), lambda b, s, i: (b, i, gi * nslab + s)),
            pl.BlockSpec((1, seq, 128), lambda b, s, i: (b, 0, nb + gi * nslab + s), pipeline_mode=pl.Buffered(1)),
            pl.BlockSpec((1, seq, 128), lambda b, s, i: (b, 0, 2 * nb + gi * nslab + s), pipeline_mode=pl.Buffered(1)),
        ],
        out_specs=[
            pl.BlockSpec((1, tb, 128), lambda b, s, i: (b, i, s)),
            pl.BlockSpec((1, tb, 128), lambda b, s, i: (b, i, s)),
        ],
        out_shape=[jax.ShapeDtypeStruct((batch, seq, ATTN_GW), F32)] * 2,
        scratch_shapes=[pltpu.VMEM((seq, 128), BF16), pltpu.VMEM((seq, 128), BF16)],
        compiler_params=_cparams(("parallel", "parallel", "arbitrary")),
        name=f"attn{gi}",
    )(at3, at3, at3)
    return o.reshape(batch * seq, ATTN_GW), lse.reshape(batch * seq, ATTN_GW)


def _halo_tile(prev_ref, main_ref, next_ref, first, last, sl):
    xm = main_ref[:, sl].astype(F32)
    xp = prev_ref[:, sl].astype(F32)[HALO_BLK - HALO:]
    xn = next_ref[:, sl].astype(F32)[:HALO]
    xp = jnp.where(first, 0.0, xp)
    xn = jnp.where(last, 0.0, xn)
    return jnp.concatenate([xp, xm, xn], axis=0)


def _tap(ext, off, t):
    n = ext.shape[0]
    return pltpu.roll(ext, (-off) % n, 0)[HALO:HALO + t] if off else ext[HALO:HALO + t]


def _seq_edges(i, t, seq):
    p0 = (i * t) % seq
    return p0 == 0, p0 + t == seq


def _conv_taps(ext, cw_ref, sl, ntap, t):
    acc = None
    for tap in range(ntap):
        term = _tap(ext, tap - ntap // 2, t) * cw_ref[tap:tap + 1, sl]
        acc = term if acc is None else acc + term
    return acc


def _halo_specs(t, m, colblk, width):
    nb = m // HALO_BLK
    r = t // HALO_BLK
    return [
        pl.BlockSpec((HALO_BLK, width), lambda i: (jnp.maximum(i * r - 1, 0), colblk)),
        pl.BlockSpec((t, width), lambda i: (i, colblk)),
        pl.BlockSpec((HALO_BLK, width), lambda i: (jnp.minimum((i + 1) * r, nb - 1), colblk)),
    ]


def _sc_kernel(b_ref, cp_ref, cm_ref, cn_ref, xp_ref, xm_ref, xn_ref, cw_ref, o_ref, *, t, seq):
    first, last = _seq_edges(pl.program_id(0), t, seq)
    for s in range(SC_WIDTH // 128):
        sl = slice(s * 128, (s + 1) * 128)
        ext = (_halo_tile(cp_ref, cm_ref, cn_ref, first, last, sl)
               * _halo_tile(xp_ref, xm_ref, xn_ref, first, last, sl))
        o_ref[:, sl] = (b_ref[:, sl].astype(F32) * _conv_taps(ext, cw_ref, sl, SC_CONV, t)).astype(BF16)


def _sc_mix(pj, sc_conv, seq, *, t=512):
    m = pj.shape[0]
    w = SC_WIDTH
    cb = PJ_SC // w
    return pl.pallas_call(
        functools.partial(_sc_kernel, t=t, seq=seq),
        grid=(m // t,),
        in_specs=[pl.BlockSpec((t, w), lambda i: (i, cb))]
        + _halo_specs(t, m, cb + 1, w)
        + _halo_specs(t, m, cb + 2, w)
        + [_resident((SC_CONV, w))],
        out_specs=pl.BlockSpec((t, w), lambda i: (i, 0)),
        out_shape=jax.ShapeDtypeStruct((m, w), BF16),
        compiler_params=_cparams(("parallel",)),
        name="sc_mix",
    )(pj, pj, pj, pj, pj, pj, pj, sc_conv)


def _dn_kernel(q_ref, k_ref, v_ref, ba_ref, alog_ref, dtb_ref, tri_ref, o_ref,
               st_ref, beta_ref, gc_ref, gct_ref, *slot_refs, ct, reverse, lane0):
    c = DN_CHUNK
    nchunk = ct // c
    heads = range(DN_HEADS)
    hsl = [slice(h * DN_HEAD_DIM, (h + 1) * DN_HEAD_DIM) for h in heads]
    nslot = 2 * DN_AHEAD
    u_refs, wq_refs, upd_refs = slot_refs[:nslot], slot_refs[nslot:2 * nslot], slot_refs[2 * nslot:]

    @pl.when(pl.program_id(1) == 0)
    def _():
        st_ref[...] = jnp.zeros_like(st_ref)

    ba = ba_ref[0]
    beta_all = jax.nn.sigmoid(ba)
    xg = ba + dtb_ref[...]
    softplus = jnp.maximum(xg, 0.0) + jnp.log1p(jnp.exp(-jnp.abs(xg)))
    g_all = -jnp.exp(alog_ref[...]) * softplus
    gc_all = _dot_exact_lhs(tri_ref[...], g_all)
    for h in heads:
        lb = lane0 - BA_G0 + h
        beta_ref[:, hsl[h]] = jnp.broadcast_to(beta_all[:, lb:lb + 1], (ct, DN_HEAD_DIM))
        gc_ref[:, hsl[h]] = jnp.broadcast_to(gc_all[:, lane0 + h:lane0 + h + 1], (ct, DN_HEAD_DIM))
    for n in range(nchunk):
        gct_ref[n] = gc_all[n * c:(n + 1) * c, :].T

    row = lax.broadcasted_iota(jnp.int32, (c, c), 0)
    col = lax.broadcasted_iota(jnp.int32, (c, c), 1)
    incl = (col >= row) if reverse else (col <= row)
    strict = (col > row) if reverse else (col < row)
    upper = lax.broadcasted_iota(jnp.int32, (c, 2 * c), 1) >= c
    eye_hi = (lax.broadcasted_iota(jnp.int32, (c, 2 * c), 1) - c
              == lax.broadcasted_iota(jnp.int32, (c, 2 * c), 0)).astype(F32)

    def local(chunks):
        units = [(n, slot, h) for n, slot in chunks for h in heads]
        idx = range(len(units))
        rows = [slice(n * c, (n + 1) * c) for n, _, _ in units]
        last = [n * c + (0 if reverse else c - 1) for n, _, _ in units]
        hs = [hsl[h] for _, _, h in units]
        k16 = [k_ref[0, rows[i], hs[i]] for i in idx]
        q = [q_ref[0, rows[i], hs[i]].astype(F32) for i in idx]
        v = [v_ref[0, rows[i], hs[i]].astype(F32) for i in idx]
        beta = [beta_ref[rows[i], hs[i]] for i in idx]
        gcx = [gc_ref[rows[i], hs[i]] for i in idx]
        gtot = [gc_ref[last[i]:last[i] + 1, hs[i]] for i in idx]
        grow = [gct_ref[n, lane0 + h:lane0 + h + 1, :] for n, _, h in units]
        k = [t.astype(F32) for t in k16]
        kb = [k[i] * beta[i] for i in idx]
        kq = [_dot_nt(jnp.concatenate([kb[i], q[i]], axis=0).astype(BF16), k16[i]) for i in idx]
        decay = [jnp.where(incl, jnp.exp(jnp.where(incl, gcx[i][:, :c] - grow[i], 0.0)), 0.0) for i in idx]
        intra = [jnp.where(incl, kq[i][c:] * decay[i], 0.0) for i in idx]
        mk = [jnp.where(strict, -kq[i][:c] * decay[i], 0.0) for i in idx]
        z = [jnp.concatenate([mk[i], jnp.zeros_like(mk[i])], axis=1) + eye_hi for i in idx]
        for _ in range(int(math.log2(c))):
            y = [_dot(z[i][:, :c].astype(BF16), z[i].astype(BF16)) for i in idx]
            z = [y[i] + jnp.where(upper, z[i], 0.0) for i in idx]
        eg = [jnp.exp(gcx[i]) for i in idx]
        rhs = [jnp.concatenate([v[i] * beta[i], kb[i] * eg[i]], axis=1).astype(BF16) for i in idx]
        rhs = [jnp.concatenate([jnp.zeros_like(t), t], axis=0) for t in rhs]
        sol = [_dot(z[i].astype(BF16), rhs[i]) for i in idx]
        kdec_t = [(k[i].T * jnp.exp(gtot[i][:, :c] - grow[i])).astype(BF16) for i in idx]
        for i, (_, slot, h) in enumerate(units):
            u_refs[slot][h] = sol[i][:, :DN_HEAD_DIM]
            wq_refs[slot][h, :c] = sol[i][:, DN_HEAD_DIM:].astype(BF16)
            wq_refs[slot][h, c:] = (q[i] * eg[i]).astype(BF16)
            upd_refs[slot][h, :c] = intra[i].astype(BF16)
            upd_refs[slot][h, c:] = kdec_t[i]

    def recur(n, slot, state):
        rows = slice(n * c, (n + 1) * c)
        last = n * c + (0 if reverse else c - 1)
        etot = [jnp.exp(gc_ref[last:last + 1, hsl[h]]) for h in heads]
        wq = [_dot(wq_refs[slot][h], state[h].astype(BF16)) for h in heads]
        v_new = [(u_refs[slot][h] - wq[h][:c]).astype(BF16) for h in heads]
        upd = [_dot(upd_refs[slot][h], v_new[h]) for h in heads]
        for h in heads:
            o_ref[0, rows, hsl[h]] = wq[h][c:] + upd[h][:c]
        return [state[h] * etot[h] + upd[h][c:] for h in heads]

    order = list(range(nchunk))[::-1] if reverse else list(range(nchunk))
    groups = [order[i:i + DN_AHEAD] for i in range(0, nchunk, DN_AHEAD)]
    slots = lambda gi: [(n, (gi % 2) * DN_AHEAD + j) for j, n in enumerate(groups[gi])]
    state = [st_ref[h] for h in heads]
    local(slots(0))
    for gi in range(len(groups)):
        if gi + 1 < len(groups):
            local(slots(gi + 1))
        for n, slot in slots(gi):
            state = recur(n, slot, state)
    for h in heads:
        st_ref[h] = state[h]


def _dn_tri(ct, reverse):
    c = DN_CHUNK
    r = jnp.arange(ct)
    same = (r[:, None] // c) == (r[None, :] // c)
    tri = same & ((r[None, :] >= r[:, None]) if reverse else (r[None, :] <= r[:, None]))
    return tri.astype(BF16)


def _deltanet(qkv, ba, alog_row, dtb_row, batch, seq, direction, *, ct=512):
    reverse = direction == 1
    nstep = seq // ct
    w = DN_WIDTH
    c = DN_CHUNK
    qkv3 = qkv.reshape(batch, seq, 3 * w)
    ba3 = ba.reshape(batch, seq, BA_W)
    step = (lambda n: nstep - 1 - n) if reverse else (lambda n: n)
    out = pl.pallas_call(
        functools.partial(_dn_kernel, ct=ct, reverse=reverse, lane0=BA_G0 + direction * DN_HEADS),
        grid=(batch, nstep),
        in_specs=[
            pl.BlockSpec((1, ct, w), lambda b, n: (b, step(n), 0)),
            pl.BlockSpec((1, ct, w), lambda b, n: (b, step(n), 1)),
            pl.BlockSpec((1, ct, w), lambda b, n: (b, step(n), 2)),
            pl.BlockSpec((1, ct, BA_W), lambda b, n: (b, step(n), 0)),
            _resident((1, BA_W)),
            _resident((1, BA_W)),
            _resident((ct, ct)),
        ],
        out_specs=pl.BlockSpec((1, ct, w), lambda b, n: (b, step(n), 0)),
        out_shape=jax.ShapeDtypeStruct((batch, seq, w), F32),
        scratch_shapes=[
            pltpu.VMEM((DN_HEADS, DN_HEAD_DIM, DN_HEAD_DIM), F32),
            pltpu.VMEM((ct, w), F32),
            pltpu.VMEM((ct, w), F32),
            pltpu.VMEM((ct // c, BA_W, c), F32),
        ]
        + [pltpu.VMEM((DN_HEADS, c, DN_HEAD_DIM), F32)] * (2 * DN_AHEAD)
        + [pltpu.VMEM((DN_HEADS, 2 * c, DN_HEAD_DIM), BF16)] * (2 * DN_AHEAD)
        + [pltpu.VMEM((DN_HEADS, c + DN_HEAD_DIM, c), BF16)] * (2 * DN_AHEAD),
        compiler_params=_cparams(("parallel", "arbitrary")),
        name=f"deltanet{direction}",
    )(qkv3, qkv3, qkv3, ba3, alog_row, dtb_row, _dn_tri(ct, reverse))
    return out.reshape(batch * seq, w)


def _merge_kernel(x_ref, o1_ref, o2_ref, o3_ref, l1_ref, l2_ref, l3_ref, df_ref, db_ref, z_ref, sc_ref, gate_ref,
                  dng_ref, wa_ref, wd_ref, ws_ref, wo_ref, out_ref):
    l1, l2, l3 = l1_ref[...], l2_ref[...], l3_ref[...]
    mx = jnp.maximum(jnp.maximum(l1, l2), l3)
    e1, e2, e3 = jnp.exp(l1 - mx), jnp.exp(l2 - mx), jnp.exp(l3 - mx)
    attn = (o1_ref[...] * e1 + o2_ref[...] * e2 + o3_ref[...] * e3) / (e1 + e2 + e3)
    y_attn = _dot(attn.astype(BF16), wa_ref[...])
    dn = df_ref[...] + db_ref[...]
    dng = dng_ref[...]
    parts = []
    for h in range(DN_HEADS):
        hs = slice(h * DN_HEAD_DIM, (h + 1) * DN_HEAD_DIM)
        oh = dn[:, hs]
        oh = oh * lax.rsqrt(jnp.mean(oh * oh, axis=-1, keepdims=True) + NORM_EPS) * dng
        parts.append((oh * z_ref[:, hs].astype(F32)).astype(BF16))
    y_dn = _dot(jnp.concatenate(parts, axis=1), wd_ref[...])
    y_sc = _dot(sc_ref[...], ws_ref[...])
    gates = gate_ref[...].astype(F32)
    d = D_MODEL
    merged = gates[:, :d] * y_attn + gates[:, d:2 * d] * y_dn + gates[:, 2 * d:] * y_sc
    out_ref[...] = x_ref[...] + _dot(merged.astype(BF16), wo_ref[...])


def _merge(x, attn_o, attn_l, dn_f, dn_b, pj, sc, dn_norm, wa, wd, ws, wo, *, tm=512):
    m, d = x.shape
    row = lambda i: (i, 0)
    gw = N_BRANCH * d
    return pl.pallas_call(
        _merge_kernel,
        grid=(m // tm,),
        in_specs=[pl.BlockSpec((tm, d), row)]
        + [pl.BlockSpec((tm, ATTN_GW), row)] * 6
        + [pl.BlockSpec((tm, DN_WIDTH), row)] * 2
        + [
            pl.BlockSpec((tm, DN_WIDTH), lambda i: (i, PJ_Z // DN_WIDTH)),
            pl.BlockSpec((tm, SC_WIDTH), row),
            pl.BlockSpec((tm, gw), lambda i: (i, PJ_GATE // gw)),
            _resident((1, DN_HEAD_DIM)),
            _resident((ATTN_GW, d)),
            _resident((DN_WIDTH, d)),
            _resident((SC_WIDTH, d)),
            _resident((d, d)),
        ],
        out_specs=pl.BlockSpec((tm, d), row),
        out_shape=jax.ShapeDtypeStruct((m, d), F32),
        compiler_params=_cparams(("parallel",)),
        name="merge",
    )(x, *attn_o, *attn_l, dn_f, dn_b, pj, sc, pj, dn_norm, wa, wd, ws, wo)


def _prep_layer(l, p):
    w_in = p["w_in"][l]
    o_attn, o_dn = 0, 3 * ATTN_WIDTH
    o_z = o_dn + 3 * DN_WIDTH
    o_beta = o_z + DN_WIDTH
    o_sc = o_beta + 4 * DN_HEADS
    o_gate = o_sc + 3 * SC_WIDTH
    bf = lambda t: t.astype(BF16)
    w_qk = w_in[:, o_attn:o_attn + 2 * ATTN_WIDTH].reshape(D_MODEL, -1, 128)[:, :, _qk_perm()]
    w_attn = bf(jnp.concatenate([w_qk.reshape(D_MODEL, -1), w_in[:, o_attn + 2 * ATTN_WIDTH:o_dn]], axis=1))
    w_proj = bf(jnp.concatenate([w_in[:, o_dn:o_z], w_in[:, o_gate:], w_in[:, o_z:o_beta], w_in[:, o_sc:o_gate]], axis=1))
    w_ba = jnp.pad(w_in[:, o_beta:o_sc], ((0, 0), (0, BA_W - 4 * DN_HEADS)))
    pad_row = lambda t: jnp.pad(t.reshape(1, 2 * DN_HEADS), ((0, 0), (BA_G0, BA_W - BA_G0 - 2 * DN_HEADS)))
    row = lambda t: t.reshape(1, -1)
    return dict(
        ffn1=(row(p["ffn1_norm"][l]), bf(p["ffn1_w_gate"][l]), bf(p["ffn1_w_up"][l]), bf(p["ffn1_w_down"][l])),
        ffn2=(row(p["ffn2_norm"][l]), bf(p["ffn2_w_gate"][l]), bf(p["ffn2_w_up"][l]), bf(p["ffn2_w_down"][l])),
        mix_norm=row(p["mix_norm"][l]),
        w_attn=w_attn,
        w_proj=w_proj,
        w_ba=bf(w_ba),
        dn_conv=p["dn_conv"][l],
        alog=pad_row(p["dn_a_log"][l]),
        dtb=pad_row(p["dn_dt_bias"][l]),
        dn_norm=row(p["dn_norm"][l]),
        sc_conv=p["sc_conv"][l],
        gate_bias=row(p["gate_bias"][l]),
        wa=bf(p["w_attn_br"][l]),
        wd=bf(p["w_dn_br"][l]),
        ws=bf(p["w_sc_br"][l]),
        wo=bf(p["w_out"][l]),
    )


def _mixer(x, lp, batch, seq, tables):
    at = _attn_proj(x, lp["mix_norm"], lp["w_attn"], tables[0], tables[1], seq)
    pj, ba, qkv = _proj(x, lp["mix_norm"], lp["w_proj"], lp["w_ba"], lp["dn_conv"], lp["gate_bias"], seq)
    attn = [_attn_group(at, batch, seq, gi) for gi in range(len(ATTN_GROUPS))]
    dn_f = _deltanet(qkv, ba, lp["alog"], lp["dtb"], batch, seq, 0)
    dn_b = _deltanet(qkv, ba, lp["alog"], lp["dtb"], batch, seq, 1)
    sc = _sc_mix(pj, lp["sc_conv"], seq)
    return _merge(x, [a[0] for a in attn], [a[1] for a in attn], dn_f, dn_b, pj, sc,
                  lp["dn_norm"], lp["wa"], lp["wd"], lp["ws"], lp["wo"])


def _trunk(x, layers, final_g):
    batch, seq, d = x.shape
    tables = _rope_tables(seq)
    x = x.reshape(batch * seq, d)
    for l, lp in enumerate(layers):
        x = _ffn(x, *lp["ffn1"])
        x = _mixer(x, lp, batch, seq, tables)
        x = _ffn(x, *lp["ffn2"], final_g=final_g if l == len(layers) - 1 else None)
    return x.reshape(batch, seq, d)


def kernel(x_prompt, x_sample, ffn1_norm, ffn1_w_gate, ffn1_w_up, ffn1_w_down, mix_norm, w_in, dn_conv, dn_a_log, dn_dt_bias, dn_norm, sc_conv, gate_bias, w_attn_br, w_dn_br, w_sc_br, w_out, ffn2_norm, ffn2_w_gate, ffn2_w_up, ffn2_w_down, final_norm):
    p = dict(ffn1_norm=ffn1_norm, ffn1_w_gate=ffn1_w_gate, ffn1_w_up=ffn1_w_up, ffn1_w_down=ffn1_w_down,
             mix_norm=mix_norm, w_in=w_in, dn_conv=dn_conv, dn_a_log=dn_a_log, dn_dt_bias=dn_dt_bias,
             dn_norm=dn_norm, sc_conv=sc_conv, gate_bias=gate_bias, w_attn_br=w_attn_br, w_dn_br=w_dn_br,
             w_sc_br=w_sc_br, w_out=w_out, ffn2_norm=ffn2_norm, ffn2_w_gate=ffn2_w_gate, ffn2_w_up=ffn2_w_up,
             ffn2_w_down=ffn2_w_down)
    layers = [_prep_layer(l, p) for l in range(ffn1_norm.shape[0])]
    final_g = final_norm.reshape(1, -1)
    return _trunk(x_prompt, layers, final_g), _trunk(x_sample, layers, final_g)
```

```python
import functools
import math

import jax
import jax.numpy as jnp
from jax import lax
from jax.experimental import pallas as pl
from jax.experimental.pallas import tpu as pltpu

F32 = jnp.float32
BF16 = jnp.bfloat16

D_MODEL = 1024
HEAD_DIM = 64
ATTN_GROUPS = ((128, 1), (512, 4), (2048, 16))
ATTN_GROUP_HEADS = 4
ATTN_GW = ATTN_GROUP_HEADS * HEAD_DIM
ATTN_WIDTH = 3 * ATTN_GW
ROT_DIM = HEAD_DIM // 4
ROPE_THETA = 500000.0
MASK_VALUE = -1e30
DN_HEADS = 6
DN_HEAD_DIM = 128
DN_WIDTH = DN_HEADS * DN_HEAD_DIM
DN_CONV = 5
DN_CHUNK = 64
SC_WIDTH = 768
SC_CONV = 3
FFN_DIM = 2816
N_BRANCH = 3
NORM_EPS = 1e-6

MXU_N = 256
PJ_GATE = 0
PJ_Z = PJ_GATE + N_BRANCH * D_MODEL
PJ_W = PJ_Z + DN_WIDTH
BA_W = 128
BA_G0 = 2 * DN_HEADS

DN_AHEAD = 4
ATTN_UNITS = 4
HALO = 8

VMEM_LIMIT = 56 * 1024 * 1024


def _cparams(sem):
    return pltpu.CompilerParams(dimension_semantics=sem, vmem_limit_bytes=VMEM_LIMIT)


def _resident(shape):
    return pl.BlockSpec(shape, lambda *_: (0,) * len(shape), pipeline_mode=pl.Buffered(1))


def _rms(x, g):
    return x * lax.rsqrt(jnp.mean(x * x, axis=-1, keepdims=True) + NORM_EPS) * g


def _dot(a, b):
    return jnp.dot(a, b, preferred_element_type=F32)


def _dot_nt(a, b):
    return lax.dot_general(a, b, (((1,), (1,)), ((), ())), preferred_element_type=F32)


def _dot_exact_lhs(sel, x):
    x1 = x.astype(BF16)
    r = x - x1.astype(F32)
    x2 = r.astype(BF16)
    x3 = (r - x2.astype(F32)).astype(BF16)
    return _dot(sel, x1) + _dot(sel, x2) + _dot(sel, x3)


def _ffn_kernel(x_ref, g_ref, wg_ref, wu_ref, wd_ref, fg_ref, o_ref, *, tf, final):
    x = x_ref[...]
    xn = _rms(x, g_ref[...]).astype(BF16)
    acc = jnp.zeros_like(x)
    for c in range(FFN_DIM // tf):
        sl = slice(c * tf, (c + 1) * tf)
        hg = _dot(xn, wg_ref[:, sl])
        hu = _dot(xn, wu_ref[:, sl])
        h = (hg * jax.nn.sigmoid(hg) * hu).astype(BF16)
        acc = acc + _dot(h, wd_ref[sl, :])
    y = x + 0.5 * acc
    if final:
        y = _rms(y, fg_ref[...])
    o_ref[...] = y


def _ffn(x, g, wg, wu, wd, final_g=None, *, tm=512, tf=256):
    m, d = x.shape
    final = final_g is not None
    fg = final_g if final else g
    return pl.pallas_call(
        functools.partial(_ffn_kernel, tf=tf, final=final),
        grid=(m // tm,),
        in_specs=[
            pl.BlockSpec((tm, d), lambda i: (i, 0)),
            _resident((1, d)),
            _resident((d, FFN_DIM)),
            _resident((d, FFN_DIM)),
            _resident((FFN_DIM, d)),
            _resident((1, d)),
        ],
        out_specs=pl.BlockSpec((tm, d), lambda i: (i, 0)),
        out_shape=jax.ShapeDtypeStruct((m, d), F32),
        compiler_params=_cparams(("parallel",)),
        name="ffn",
    )(x, g, wg, wu, wd, fg)


def _attn_proj_kernel(x_ref, g_ref, w_ref, cos_ref, sin_ref, cw_ref, at_ref, sc_ref, new_ref, old_ref, *, t, seq):
    i = pl.program_id(0)
    aw = 3 * ATTN_WIDTH

    @pl.when(i == 0)
    def _():
        new_ref[...] = jnp.zeros_like(new_ref)
        old_ref[...] = jnp.zeros_like(old_ref)

    first, last = _seq_edges(i - 1, t, seq)
    xn = _rms(x_ref[...], g_ref[...]).astype(BF16)

    def attn_chunk(c):
        y = _dot(xn, w_ref[:, c * MXU_N:(c + 1) * MXU_N])
        sect = c * MXU_N // ATTN_WIDTH
        for s in range(MXU_N // 128):
            ys = y[:, s * 128:(s + 1) * 128]
            if sect < 2:
                ys = ys * cos_ref[sect] + pltpu.roll(ys, HEAD_DIM, 1) * sin_ref[sect]
            at_ref[:, c * MXU_N + s * 128:c * MXU_N + (s + 1) * 128] = ys

    def sc_chunk(c):
        new_ref[:, c * MXU_N:(c + 1) * MXU_N] = _dot(xn, w_ref[:, aw + c * MXU_N:aw + (c + 1) * MXU_N])

    def sc_slab(s):
        lanes = [slice(part * SC_WIDTH + s * 128, part * SC_WIDTH + (s + 1) * 128) for part in range(3)]
        ext = [jnp.concatenate([jnp.where(first, 0.0, old_ref[:HALO, sl]), old_ref[HALO:, sl],
                                jnp.where(last, 0.0, new_ref[:HALO, sl])], axis=0) for sl in lanes[1:]]
        out = slice(s * 128, (s + 1) * 128)
        y = old_ref[HALO:, lanes[0]] * _conv_taps(ext[0] * ext[1], cw_ref, out, SC_CONV, t)
        sc_ref[:, out] = y.astype(BF16)
        for sl in lanes:
            old_ref[:HALO, sl] = old_ref[t:, sl]
            old_ref[HALO:, sl] = new_ref[:, sl]

    per = MXU_N // 128
    for grp in range(SC_WIDTH // MXU_N):
        for part in range(3):
            sc_chunk(part * (SC_WIDTH // MXU_N) + grp)
        for j in range(aw // MXU_N // (SC_WIDTH // MXU_N)):
            attn_chunk(grp * 3 + j)
            if j < per:
                sc_slab(grp * per + j)


def _attn_proj(x, g, w, cos_t, sin_t, sc_conv, seq, *, tm=512):
    m, d = x.shape
    nt = m // tm
    nseq = seq // tm
    aw, scw = 3 * ATTN_WIDTH, 3 * SC_WIDTH
    cur_tile = lambda i: (jnp.minimum(i, nt - 1), 0)
    table = lambda i: (0, jnp.minimum(i, nt - 1) % nseq, 0)
    return pl.pallas_call(
        functools.partial(_attn_proj_kernel, t=tm, seq=seq),
        grid=(nt + 1,),
        in_specs=[
            pl.BlockSpec((tm, d), cur_tile),
            _resident((1, d)),
            _resident((d, aw + scw)),
            pl.BlockSpec((2, tm, 128), table),
            pl.BlockSpec((2, tm, 128), table),
            _resident((SC_CONV, SC_WIDTH)),
        ],
        out_specs=[
            pl.BlockSpec((tm, aw), cur_tile),
            pl.BlockSpec((tm, SC_WIDTH), lambda i: (jnp.maximum(i - 1, 0), 0)),
        ],
        out_shape=[jax.ShapeDtypeStruct((m, aw), F32), jax.ShapeDtypeStruct((m, SC_WIDTH), BF16)],
        scratch_shapes=[pltpu.VMEM((tm, scw), F32), pltpu.VMEM((HALO + tm, scw), F32)],
        compiler_params=_cparams(("arbitrary",)),
        name="attn_proj",
    )(x, g, w, cos_t, sin_t, sc_conv)


def _proj_kernel(x_ref, g_ref, w_ref, wba_ref, cw_ref, gb_ref, pj_ref, ba_ref, qkv_ref, new_ref, old_ref, *, t, seq):
    i = pl.program_id(0)
    dnw = 3 * DN_WIDTH

    @pl.when(i == 0)
    def _():
        new_ref[...] = jnp.zeros_like(new_ref)
        old_ref[...] = jnp.zeros_like(old_ref)

    first, last = _seq_edges(i - 1, t, seq)
    xn = _rms(x_ref[...], g_ref[...]).astype(BF16)
    ba_ref[...] = _dot(xn, wba_ref[...])
    nslab = dnw // DN_HEAD_DIM
    npj = PJ_W // MXU_N
    for s in range(nslab):
        if s % (MXU_N // DN_HEAD_DIM) == 0:
            cs = slice(s * DN_HEAD_DIM, s * DN_HEAD_DIM + MXU_N)
            new_ref[:, cs] = _dot(xn, w_ref[:, cs])
        for c in range(npj * s // nslab, npj * (s + 1) // nslab):
            cs = slice(c * MXU_N, (c + 1) * MXU_N)
            y = _dot(xn, w_ref[:, dnw + c * MXU_N:dnw + (c + 1) * MXU_N])
            if cs.stop <= PJ_Z:
                y = jax.nn.sigmoid(y + gb_ref[:, cs])
            else:
                y = y * jax.nn.sigmoid(y)
            pj_ref[:, cs] = y.astype(BF16)
        sl = slice(s * DN_HEAD_DIM, (s + 1) * DN_HEAD_DIM)
        ext = jnp.concatenate([jnp.where(first, 0.0, old_ref[:HALO, sl]), old_ref[HALO:, sl],
                               jnp.where(last, 0.0, new_ref[:HALO, sl])], axis=0)
        y = _conv_taps(ext, cw_ref, sl, DN_CONV, t)
        y = y * jax.nn.sigmoid(y)
        if s < 2 * DN_HEADS:
            scale = DN_HEAD_DIM ** -0.5 if s < DN_HEADS else 1.0
            y = y * (lax.rsqrt(jnp.sum(y * y, axis=-1, keepdims=True) + NORM_EPS) * scale)
        qkv_ref[:, sl] = y.astype(BF16)
        old_ref[:HALO, sl] = old_ref[t:, sl]
        old_ref[HALO:, sl] = new_ref[:, sl]


def _proj(x, g, w, w_ba, dn_conv, gate_bias, seq, *, tm=512):
    m, d = x.shape
    nt = m // tm
    dnw = 3 * DN_WIDTH
    cur_tile = lambda i: (jnp.minimum(i, nt - 1), 0)
    return pl.pallas_call(
        functools.partial(_proj_kernel, t=tm, seq=seq),
        grid=(nt + 1,),
        in_specs=[
            pl.BlockSpec((tm, d), cur_tile),
            _resident((1, d)),
            _resident((d, dnw + PJ_W)),
            _resident((d, BA_W)),
            _resident((DN_CONV, dnw)),
            _resident((1, N_BRANCH * d)),
        ],
        out_specs=[
            pl.BlockSpec((tm, PJ_W), cur_tile),
            pl.BlockSpec((tm, BA_W), cur_tile),
            pl.BlockSpec((tm, dnw), lambda i: (jnp.maximum(i - 1, 0), 0)),
        ],
        out_shape=[
            jax.ShapeDtypeStruct((m, PJ_W), BF16),
            jax.ShapeDtypeStruct((m, BA_W), F32),
            jax.ShapeDtypeStruct((m, dnw), BF16),
        ],
        scratch_shapes=[pltpu.VMEM((tm, dnw), F32), pltpu.VMEM((HALO + tm, dnw), F32)],
        compiler_params=_cparams(("arbitrary",)),
        name="proj",
    )(x, g, w, w_ba, dn_conv, gate_bias)


def _rope_tables(seq):
    half = ROT_DIM // 2
    inv_freq = ROPE_THETA ** (-2.0 * jnp.arange(half, dtype=F32) / ROT_DIM)
    ang = jnp.arange(seq, dtype=F32)[:, None] * inv_freq[None, :]
    cos, sin = jnp.cos(ang), jnp.sin(ang)
    ones = jnp.ones((seq, HEAD_DIM - ROT_DIM), F32)
    cos_h = jnp.concatenate([cos, cos, ones, cos, cos, ones], axis=1)
    sin_h = jnp.concatenate([-sin, -sin, 0.0 * ones, sin, sin, 0.0 * ones], axis=1)
    qs = HEAD_DIM ** -0.5
    return jnp.stack([cos_h * qs, cos_h]), jnp.stack([sin_h * qs, sin_h])


def _qk_perm():
    a = jnp.arange
    half = ROT_DIM // 2
    rest = (HEAD_DIM - ROT_DIM) // 2
    lo = [a(0, half), a(HEAD_DIM, HEAD_DIM + half), a(ROT_DIM, ROT_DIM + rest),
          a(HEAD_DIM + ROT_DIM, HEAD_DIM + ROT_DIM + rest)]
    hi = [a(half, ROT_DIM), a(HEAD_DIM + half, HEAD_DIM + ROT_DIM), a(ROT_DIM + rest, HEAD_DIM),
          a(HEAD_DIM + ROT_DIM + rest, 2 * HEAD_DIM)]
    return jnp.concatenate(lo + hi)


def _qk_first_head(lane):
    half = ROT_DIM // 2
    rest = (HEAD_DIM - ROT_DIM) // 2
    l = lane % HEAD_DIM
    return (l < half) | ((l >= ROT_DIM) & (l < ROT_DIM + rest))


def _attn_kernel(q_ref, k_ref, v_ref, o_ref, lse_ref, kres_ref, vres_ref, *, dil, tq, tk, half, length, nsub):
    i = pl.program_id(2)

    @pl.when(i == 0)
    def _():
        for r in range(dil):
            rows = pl.ds(r, length, stride=dil) if dil > 1 else pl.ds(0, length)
            kres_ref[r * length:(r + 1) * length, :] = k_ref[0, rows, :].astype(BF16)
            vres_ref[r * length:(r + 1) * length, :] = v_ref[0, rows, :].astype(BF16)

    lane = lax.broadcasted_iota(jnp.int32, (1, 128), 1)
    first_out = lane < HEAD_DIM
    first_qk = _qk_first_head(lane)
    qiota = lax.broadcasted_iota(jnp.int32, (tq, 1), 0)
    kiota = lax.broadcasted_iota(jnp.int32, (1, tk), 1)

    def group(g, carry):
        rows, mask, q, kw, vw = [], [], [], [], []
        for j in range(ATTN_UNITS):
            u = g * ATTN_UNITS + j
            r = u % dil
            sub = u // dil
            blk = i * nsub + sub
            ks = pl.multiple_of(jnp.clip(blk * tq - half, 0, length - tk), 64)
            mask.append(jnp.abs(blk * tq + qiota - ks - kiota) <= half)
            if dil > 1:
                rows.append(pl.ds(r + sub * (tq * dil), tq, stride=dil))
            else:
                rows.append(pl.ds(pl.multiple_of(sub * tq, tq), tq))
            q.append(q_ref[0, rows[j], :].astype(BF16))
            kbase = pl.multiple_of(r * length + ks, 64)
            kw.append(kres_ref[pl.ds(kbase, tk), :])
            vw.append(vres_ref[pl.ds(kbase, tk), :])
        chains = [(j, sel) for j in range(ATTN_UNITS) for sel in (first_qk, ~first_qk)]
        sc = [_dot_nt(jnp.where(sel, q[j], jnp.zeros_like(q[j])), kw[j]) for j, sel in chains]
        sc = [jnp.where(mask[j], s, MASK_VALUE) for (j, _), s in zip(chains, sc)]
        mx = [jnp.max(s, axis=-1, keepdims=True) for s in sc]
        p = [jnp.exp(s - m) for s, m in zip(sc, mx)]
        den = [jnp.sum(t, axis=-1, keepdims=True) for t in p]
        o = [_dot(t.astype(BF16), vw[j]) / d for (j, _), t, d in zip(chains, p, den)]
        lse = [m + jnp.log(d) for m, d in zip(mx, den)]
        for j in range(ATTN_UNITS):
            o_ref[0, rows[j], :] = jnp.where(first_out, o[2 * j], o[2 * j + 1])
            lse_ref[0, rows[j], :] = jnp.where(first_out, lse[2 * j], lse[2 * j + 1])
        return carry

    lax.fori_loop(0, dil * nsub // ATTN_UNITS, group, 0)


def _attn_group(at, batch, seq, gi):
    window, dil = ATTN_GROUPS[gi]
    half = (window // 2) // dil
    length = seq // dil
    tq = 2 * half
    tk = tq + 2 * half
    nsub = max(1, 2 * ATTN_UNITS // dil)
    assert half == 64 and seq % dil == 0 and tk <= length and length % (tq * nsub) == 0
    assert (dil * nsub) % ATTN_UNITS == 0
    tb = tq * dil * nsub
    at3 = at.reshape(batch, seq, 3 * ATTN_WIDTH)
    nslab = ATTN_GW // 128
    nb = ATTN_WIDTH // 128
    o, lse = pl.pallas_call(
        functools.partial(_attn_kernel, dil=dil, tq=tq, tk=tk, half=half, length=length, nsub=nsub),
        grid=(batch, nslab, seq // tb),
        in_specs=[
            pl.BlockSpec((1, tb, 128), lambda b, s, i: (b, i, gi * nslab + s)),
            pl.BlockSpec((1, seq, 128), lambda b, s, i: (b, 0, nb + gi * nslab + s), pipeline_mode=pl.Buffered(1)),
            pl.BlockSpec((1, seq, 128), lambda b, s, i: (b, 0, 2 * nb + gi * nslab + s), pipeline_mode=pl.Buffered(1)),
        ],
        out_specs=[
            pl.BlockSpec((1, tb, 128), lambda b, s, i: (b, i, s)),
            pl.BlockSpec((1, tb, 128), lambda b, s, i: (b, i, s)),
        ],
        out_shape=[jax.ShapeDtypeStruct((batch, seq, ATTN_GW), F32)] * 2,
        scratch_shapes=[pltpu.VMEM((seq, 128), BF16), pltpu.VMEM((seq, 128), BF16)],
        compiler_params=_cparams(("parallel", "parallel", "arbitrary")),
        name=f"attn{gi}",
    )(at3, at3, at3)
    return o.reshape(batch * seq, ATTN_GW), lse.reshape(batch * seq, ATTN_GW)


def _tap(ext, off, t):
    n = ext.shape[0]
    return pltpu.roll(ext, (-off) % n, 0)[HALO:HALO + t] if off else ext[HALO:HALO + t]


def _seq_edges(i, t, seq):
    p0 = (i * t) % seq
    return p0 == 0, p0 + t == seq


def _conv_taps(ext, cw_ref, sl, ntap, t):
    acc = None
    for tap in range(ntap):
        term = _tap(ext, tap - ntap // 2, t) * cw_ref[tap:tap + 1, sl]
        acc = term if acc is None else acc + term
    return acc


def _dn_kernel(q_ref, k_ref, v_ref, ba_ref, alog_ref, dtb_ref, tri_ref, *refs, ct, reverse, lane0, add_prev):
    prev_ref = refs[0] if add_prev else None
    o_ref, st_ref, beta_ref, gc_ref, gct_ref, *slot_refs = refs[1:] if add_prev else refs
    _dn_body(q_ref, k_ref, v_ref, ba_ref, alog_ref, dtb_ref, tri_ref, prev_ref, o_ref, st_ref, beta_ref, gc_ref,
             gct_ref, slot_refs, ct=ct, reverse=reverse, lane0=lane0)


def _dn_body(q_ref, k_ref, v_ref, ba_ref, alog_ref, dtb_ref, tri_ref, prev_ref, o_ref,
             st_ref, beta_ref, gc_ref, gct_ref, slot_refs, *, ct, reverse, lane0):
    c = DN_CHUNK
    nchunk = ct // c
    heads = range(DN_HEADS)
    hsl = [slice(h * DN_HEAD_DIM, (h + 1) * DN_HEAD_DIM) for h in heads]
    nslot = 2 * DN_AHEAD
    u_refs, wq_refs, upd_refs = slot_refs[:nslot], slot_refs[nslot:2 * nslot], slot_refs[2 * nslot:]

    @pl.when(pl.program_id(1) == 0)
    def _():
        st_ref[...] = jnp.zeros_like(st_ref)

    ba = ba_ref[0]
    beta_all = jax.nn.sigmoid(ba)
    xg = ba + dtb_ref[...]
    softplus = jnp.maximum(xg, 0.0) + jnp.log1p(jnp.exp(-jnp.abs(xg)))
    g_all = -jnp.exp(alog_ref[...]) * softplus
    gc_all = _dot_exact_lhs(tri_ref[...], g_all)
    for h in heads:
        lb = lane0 - BA_G0 + h
        beta_ref[:, hsl[h]] = jnp.broadcast_to(beta_all[:, lb:lb + 1], (ct, DN_HEAD_DIM))
        gc_ref[:, hsl[h]] = jnp.broadcast_to(gc_all[:, lane0 + h:lane0 + h + 1], (ct, DN_HEAD_DIM))
    for n in range(nchunk):
        gct_ref[n] = gc_all[n * c:(n + 1) * c, :].T

    row = lax.broadcasted_iota(jnp.int32, (c, c), 0)
    col = lax.broadcasted_iota(jnp.int32, (c, c), 1)
    incl = (col >= row) if reverse else (col <= row)
    strict = (col > row) if reverse else (col < row)
    upper = lax.broadcasted_iota(jnp.int32, (c, 2 * c), 1) >= c
    eye_hi = (lax.broadcasted_iota(jnp.int32, (c, 2 * c), 1) - c
              == lax.broadcasted_iota(jnp.int32, (c, 2 * c), 0)).astype(F32)

    def local(chunks):
        units = [(n, slot, h) for n, slot in chunks for h in heads]
        idx = range(len(units))
        rows = [slice(n * c, (n + 1) * c) for n, _, _ in units]
        last = [n * c + (0 if reverse else c - 1) for n, _, _ in units]
        hs = [hsl[h] for _, _, h in units]
        k16 = [k_ref[0, rows[i], hs[i]] for i in idx]
        q = [q_ref[0, rows[i], hs[i]].astype(F32) for i in idx]
        v = [v_ref[0, rows[i], hs[i]].astype(F32) for i in idx]
        beta = [beta_ref[rows[i], hs[i]] for i in idx]
        gcx = [gc_ref[rows[i], hs[i]] for i in idx]
        gtot = [gc_ref[last[i]:last[i] + 1, hs[i]] for i in idx]
        grow = [gct_ref[n, lane0 + h:lane0 + h + 1, :] for n, _, h in units]
        k = [t.astype(F32) for t in k16]
        kb = [k[i] * beta[i] for i in idx]
        kq = [_dot_nt(jnp.concatenate([kb[i], q[i]], axis=0).astype(BF16), k16[i]) for i in idx]
        decay = [jnp.where(incl, jnp.exp(jnp.where(incl, gcx[i][:, :c] - grow[i], 0.0)), 0.0) for i in idx]
        intra = [jnp.where(incl, kq[i][c:] * decay[i], 0.0) for i in idx]
        mk = [jnp.where(strict, -kq[i][:c] * decay[i], 0.0) for i in idx]
        z = [jnp.concatenate([mk[i], jnp.zeros_like(mk[i])], axis=1) + eye_hi for i in idx]
        for _ in range(int(math.log2(c))):
            y = [_dot(z[i][:, :c].astype(BF16), z[i].astype(BF16)) for i in idx]
            z = [y[i] + jnp.where(upper, z[i], 0.0) for i in idx]
        eg = [jnp.exp(gcx[i]) for i in idx]
        rhs = [jnp.concatenate([v[i] * beta[i], kb[i] * eg[i]], axis=1).astype(BF16) for i in idx]
        rhs = [jnp.concatenate([jnp.zeros_like(t), t], axis=0) for t in rhs]
        sol = [_dot(z[i].astype(BF16), rhs[i]) for i in idx]
        kdec_t = [(k[i].T * jnp.exp(gtot[i][:, :c] - grow[i])).astype(BF16) for i in idx]
        for i, (_, slot, h) in enumerate(units):
            u_refs[slot][h] = sol[i][:, :DN_HEAD_DIM]
            wq_refs[slot][h, :c] = sol[i][:, DN_HEAD_DIM:].astype(BF16)
            wq_refs[slot][h, c:] = (q[i] * eg[i]).astype(BF16)
            upd_refs[slot][h, :c] = intra[i].astype(BF16)
            upd_refs[slot][h, c:] = kdec_t[i]

    def recur(n, slot, state):
        rows = slice(n * c, (n + 1) * c)
        last = n * c + (0 if reverse else c - 1)
        etot = [jnp.exp(gc_ref[last:last + 1, hsl[h]]) for h in heads]
        wq = [_dot(wq_refs[slot][h], state[h].astype(BF16)) for h in heads]
        v_new = [(u_refs[slot][h] - wq[h][:c]).astype(BF16) for h in heads]
        upd = [_dot(upd_refs[slot][h], v_new[h]) for h in heads]
        for h in heads:
            o = wq[h][c:] + upd[h][:c]
            if prev_ref is not None:
                o = o + prev_ref[0, rows, hsl[h]].astype(F32)
            o_ref[0, rows, hsl[h]] = o.astype(o_ref.dtype)
        return [state[h] * etot[h] + upd[h][c:] for h in heads]

    order = list(range(nchunk))[::-1] if reverse else list(range(nchunk))
    groups = [order[i:i + DN_AHEAD] for i in range(0, nchunk, DN_AHEAD)]
    slots = lambda gi: [(n, (gi % 2) * DN_AHEAD + j) for j, n in enumerate(groups[gi])]
    state = [st_ref[h] for h in heads]
    local(slots(0))
    for gi in range(len(groups)):
        if gi + 1 < len(groups):
            local(slots(gi + 1))
        for n, slot in slots(gi):
            state = recur(n, slot, state)
    for h in heads:
        st_ref[h] = state[h]


def _dn_tri(ct, reverse):
    c = DN_CHUNK
    r = jnp.arange(ct)
    same = (r[:, None] // c) == (r[None, :] // c)
    tri = same & ((r[None, :] >= r[:, None]) if reverse else (r[None, :] <= r[:, None]))
    return tri.astype(BF16)


def _deltanet(qkv, ba, alog_row, dtb_row, batch, seq, direction, prev=None, *, ct=512):
    reverse = direction == 1
    nstep = seq // ct
    w = DN_WIDTH
    c = DN_CHUNK
    qkv3 = qkv.reshape(batch, seq, 3 * w)
    ba3 = ba.reshape(batch, seq, BA_W)
    step = (lambda n: nstep - 1 - n) if reverse else (lambda n: n)
    out = pl.pallas_call(
        functools.partial(_dn_kernel, ct=ct, reverse=reverse, lane0=BA_G0 + direction * DN_HEADS,
                          add_prev=prev is not None),
        grid=(batch, nstep),
        in_specs=[
            pl.BlockSpec((1, ct, w), lambda b, n: (b, step(n), 0)),
            pl.BlockSpec((1, ct, w), lambda b, n: (b, step(n), 1)),
            pl.BlockSpec((1, ct, w), lambda b, n: (b, step(n), 2)),
            pl.BlockSpec((1, ct, BA_W), lambda b, n: (b, step(n), 0)),
            _resident((1, BA_W)),
            _resident((1, BA_W)),
            _resident((ct, ct)),
        ] + ([] if prev is None else [pl.BlockSpec((1, ct, w), lambda b, n: (b, step(n), 0))]),
        out_specs=pl.BlockSpec((1, ct, w), lambda b, n: (b, step(n), 0)),
        out_shape=jax.ShapeDtypeStruct((batch, seq, w), BF16),
        scratch_shapes=[
            pltpu.VMEM((DN_HEADS, DN_HEAD_DIM, DN_HEAD_DIM), F32),
            pltpu.VMEM((ct, w), F32),
            pltpu.VMEM((ct, w), F32),
            pltpu.VMEM((ct // c, BA_W, c), F32),
        ]
        + [pltpu.VMEM((DN_HEADS, c, DN_HEAD_DIM), F32)] * (2 * DN_AHEAD)
        + [pltpu.VMEM((DN_HEADS, 2 * c, DN_HEAD_DIM), BF16)] * (2 * DN_AHEAD)
        + [pltpu.VMEM((DN_HEADS, c + DN_HEAD_DIM, c), BF16)] * (2 * DN_AHEAD),
        compiler_params=_cparams(("parallel", "arbitrary")),
        name=f"deltanet{direction}",
    )(qkv3, qkv3, qkv3, ba3, alog_row, dtb_row, _dn_tri(ct, reverse),
      *([] if prev is None else [prev.reshape(batch, seq, w)]))
    return out.reshape(batch * seq, w)


def _merge_kernel(x_ref, o1_ref, o2_ref, o3_ref, l1_ref, l2_ref, l3_ref, dn_ref, z_ref, sc_ref, gate_ref,
                  dng_ref, wa_ref, wd_ref, ws_ref, wo_ref, out_ref):
    l1, l2, l3 = l1_ref[...], l2_ref[...], l3_ref[...]
    mx = jnp.maximum(jnp.maximum(l1, l2), l3)
    e1, e2, e3 = jnp.exp(l1 - mx), jnp.exp(l2 - mx), jnp.exp(l3 - mx)
    attn = (o1_ref[...] * e1 + o2_ref[...] * e2 + o3_ref[...] * e3) / (e1 + e2 + e3)
    y_attn = _dot(attn.astype(BF16), wa_ref[...])
    dn = dn_ref[...].astype(F32)
    dng = dng_ref[...]
    parts = []
    for h in range(DN_HEADS):
        hs = slice(h * DN_HEAD_DIM, (h + 1) * DN_HEAD_DIM)
        oh = dn[:, hs]
        oh = oh * lax.rsqrt(jnp.mean(oh * oh, axis=-1, keepdims=True) + NORM_EPS) * dng
        parts.append((oh * z_ref[:, hs].astype(F32)).astype(BF16))
    y_dn = _dot(jnp.concatenate(parts, axis=1), wd_ref[...])
    y_sc = _dot(sc_ref[...], ws_ref[...])
    gates = gate_ref[...].astype(F32)
    d = D_MODEL
    merged = gates[:, :d] * y_attn + gates[:, d:2 * d] * y_dn + gates[:, 2 * d:] * y_sc
    out_ref[...] = x_ref[...] + _dot(merged.astype(BF16), wo_ref[...])


def _merge(x, attn_o, attn_l, dn, pj, sc, dn_norm, wa, wd, ws, wo, *, tm=512):
    m, d = x.shape
    row = lambda i: (i, 0)
    gw = N_BRANCH * d
    return pl.pallas_call(
        _merge_kernel,
        grid=(m // tm,),
        in_specs=[pl.BlockSpec((tm, d), row)]
        + [pl.BlockSpec((tm, ATTN_GW), row)] * 6
        + [pl.BlockSpec((tm, DN_WIDTH), row)]
        + [
            pl.BlockSpec((tm, DN_WIDTH), lambda i: (i, PJ_Z // DN_WIDTH)),
            pl.BlockSpec((tm, SC_WIDTH), row),
            pl.BlockSpec((tm, gw), lambda i: (i, PJ_GATE // gw)),
            _resident((1, DN_HEAD_DIM)),
            _resident((ATTN_GW, d)),
            _resident((DN_WIDTH, d)),
            _resident((SC_WIDTH, d)),
            _resident((d, d)),
        ],
        out_specs=pl.BlockSpec((tm, d), row),
        out_shape=jax.ShapeDtypeStruct((m, d), F32),
        compiler_params=_cparams(("parallel",)),
        name="merge",
    )(x, *attn_o, *attn_l, dn, pj, sc, pj, dn_norm, wa, wd, ws, wo)


def _prep_layer(l, p):
    w_in = p["w_in"][l]
    o_attn, o_dn = 0, 3 * ATTN_WIDTH
    o_z = o_dn + 3 * DN_WIDTH
    o_beta = o_z + DN_WIDTH
    o_sc = o_beta + 4 * DN_HEADS
    o_gate = o_sc + 3 * SC_WIDTH
    bf = lambda t: t.astype(BF16)
    w_qk = w_in[:, o_attn:o_attn + 2 * ATTN_WIDTH].reshape(D_MODEL, -1, 128)[:, :, _qk_perm()]
    w_attn = bf(jnp.concatenate([w_qk.reshape(D_MODEL, -1), w_in[:, o_attn + 2 * ATTN_WIDTH:o_dn], w_in[:, o_sc:o_gate]], axis=1))
    w_proj = bf(jnp.concatenate([w_in[:, o_dn:o_z], w_in[:, o_gate:], w_in[:, o_z:o_beta]], axis=1))
    w_ba = jnp.pad(w_in[:, o_beta:o_sc], ((0, 0), (0, BA_W - 4 * DN_HEADS)))
    pad_row = lambda t: jnp.pad(t.reshape(1, 2 * DN_HEADS), ((0, 0), (BA_G0, BA_W - BA_G0 - 2 * DN_HEADS)))
    row = lambda t: t.reshape(1, -1)
    return dict(
        ffn1=(row(p["ffn1_norm"][l]), bf(p["ffn1_w_gate"][l]), bf(p["ffn1_w_up"][l]), bf(p["ffn1_w_down"][l])),
        ffn2=(row(p["ffn2_norm"][l]), bf(p["ffn2_w_gate"][l]), bf(p["ffn2_w_up"][l]), bf(p["ffn2_w_down"][l])),
        mix_norm=row(p["mix_norm"][l]),
        w_attn=w_attn,
        w_proj=w_proj,
        w_ba=bf(w_ba),
        dn_conv=p["dn_conv"][l],
        alog=pad_row(p["dn_a_log"][l]),
        dtb=pad_row(p["dn_dt_bias"][l]),
        dn_norm=row(p["dn_norm"][l]),
        sc_conv=p["sc_conv"][l],
        gate_bias=row(p["gate_bias"][l]),
        wa=bf(p["w_attn_br"][l]),
        wd=bf(p["w_dn_br"][l]),
        ws=bf(p["w_sc_br"][l]),
        wo=bf(p["w_out"][l]),
    )


def _mixer(x, lp, batch, seq, tables):
    at, sc = _attn_proj(x, lp["mix_norm"], lp["w_attn"], tables[0], tables[1], lp["sc_conv"], seq)
    pj, ba, qkv = _proj(x, lp["mix_norm"], lp["w_proj"], lp["w_ba"], lp["dn_conv"], lp["gate_bias"], seq)
    attn = [_attn_group(at, batch, seq, gi) for gi in range(len(ATTN_GROUPS))]
    dn = _deltanet(qkv, ba, lp["alog"], lp["dtb"], batch, seq, 0)
    dn = _deltanet(qkv, ba, lp["alog"], lp["dtb"], batch, seq, 1, prev=dn)
    return _merge(x, [a[0] for a in attn], [a[1] for a in attn], dn, pj, sc,
                  lp["dn_norm"], lp["wa"], lp["wd"], lp["ws"], lp["wo"])


def _trunk(x, layers, final_g):
    batch, seq, d = x.shape
    tables = _rope_tables(seq)
    x = x.reshape(batch * seq, d)
    for l, lp in enumerate(layers):
        x = _ffn(x, *lp["ffn1"])
        x = _mixer(x, lp, batch, seq, tables)
        x = _ffn(x, *lp["ffn2"], final_g=final_g if l == len(layers) - 1 else None)
    return x.reshape(batch, seq, d)


def kernel(x_prompt, x_sample, ffn1_norm, ffn1_w_gate, ffn1_w_up, ffn1_w_down, mix_norm, w_in, dn_conv, dn_a_log, dn_dt_bias, dn_norm, sc_conv, gate_bias, w_attn_br, w_dn_br, w_sc_br, w_out, ffn2_norm, ffn2_w_gate, ffn2_w_up, ffn2_w_down, final_norm):
    p = dict(ffn1_norm=ffn1_norm, ffn1_w_gate=ffn1_w_gate, ffn1_w_up=ffn1_w_up, ffn1_w_down=ffn1_w_down,
             mix_norm=mix_norm, w_in=w_in, dn_conv=dn_conv, dn_a_log=dn_a_log, dn_dt_bias=dn_dt_bias,
             dn_norm=dn_norm, sc_conv=sc_conv, gate_bias=gate_bias, w_attn_br=w_attn_br, w_dn_br=w_dn_br,
             w_sc_br=w_sc_br, w_out=w_out, ffn2_norm=ffn2_norm, ffn2_w_gate=ffn2_w_gate, ffn2_w_up=ffn2_w_up,
             ffn2_w_down=ffn2_w_down)
    layers = [_prep_layer(l, p) for l in range(ffn1_norm.shape[0])]
    final_g = final_norm.reshape(1, -1)
    return _trunk(x_prompt, layers, final_g), _trunk(x_sample, layers, final_g)
```

```python
import functools
import math

import jax
import jax.numpy as jnp
from jax import lax
from jax.experimental import pallas as pl
from jax.experimental.pallas import tpu as pltpu

F32 = jnp.float32
BF16 = jnp.bfloat16

D_MODEL = 1024
HEAD_DIM = 64
ATTN_GROUPS = ((128, 1), (512, 4), (2048, 16))
ATTN_GROUP_HEADS = 4
ATTN_GW = ATTN_GROUP_HEADS * HEAD_DIM
ATTN_WIDTH = 3 * ATTN_GW
ROT_DIM = HEAD_DIM // 4
ROPE_THETA = 500000.0
DN_HEADS = 6
DN_HEAD_DIM = 128
DN_WIDTH = DN_HEADS * DN_HEAD_DIM
DN_CONV = 5
DN_CHUNK = 64
SC_WIDTH = 768
SC_CONV = 3
FFN_DIM = 2816
N_BRANCH = 3
NORM_EPS = 1e-6

MXU_N = 256
PJ_GATE = 0
PJ_Z = PJ_GATE + N_BRANCH * D_MODEL
PJ_W = PJ_Z + DN_WIDTH
BA_W = 128
BA_G0 = 2 * DN_HEADS

DN_AHEAD = 4
ATTN_UNITS = 4
HALO = 8

VMEM_LIMIT = 56 * 1024 * 1024


def _cparams(sem):
    return pltpu.CompilerParams(dimension_semantics=sem, vmem_limit_bytes=VMEM_LIMIT)


def _resident(shape):
    return pl.BlockSpec(shape, lambda *_: (0,) * len(shape), pipeline_mode=pl.Buffered(1))


def _rms(x, g):
    return x * lax.rsqrt(jnp.mean(x * x, axis=-1, keepdims=True) + NORM_EPS) * g


def _dot(a, b):
    return jnp.dot(a, b, preferred_element_type=F32)


def _dot_nt(a, b):
    return lax.dot_general(a, b, (((1,), (1,)), ((), ())), preferred_element_type=F32)


def _dot_exact_lhs(sel, x):
    x1 = x.astype(BF16)
    r = x - x1.astype(F32)
    x2 = r.astype(BF16)
    x3 = (r - x2.astype(F32)).astype(BF16)
    return _dot(sel, x1) + _dot(sel, x2) + _dot(sel, x3)


def _ffn_kernel(x_ref, g_ref, wg_ref, wu_ref, wd_ref, fg_ref, o_ref, *, tf, final):
    x = x_ref[...]
    xn = _rms(x, g_ref[...]).astype(BF16)
    acc = jnp.zeros_like(x)
    for c in range(FFN_DIM // tf):
        sl = slice(c * tf, (c + 1) * tf)
        hg = _dot(xn, wg_ref[:, sl])
        hu = _dot(xn, wu_ref[:, sl])
        h = (hg * jax.nn.sigmoid(hg) * hu).astype(BF16)
        acc = acc + _dot(h, wd_ref[sl, :])
    y = x + 0.5 * acc
    if final:
        y = _rms(y, fg_ref[...])
    o_ref[...] = y


def _ffn(x, g, wg, wu, wd, final_g=None, *, tm=1024, tf=256):
    m, d = x.shape
    final = final_g is not None
    fg = final_g if final else g
    return pl.pallas_call(
        functools.partial(_ffn_kernel, tf=tf, final=final),
        grid=(m // tm,),
        in_specs=[
            pl.BlockSpec((tm, d), lambda i: (i, 0)),
            _resident((1, d)),
            _resident((d, FFN_DIM)),
            _resident((d, FFN_DIM)),
            _resident((FFN_DIM, d)),
            _resident((1, d)),
        ],
        out_specs=pl.BlockSpec((tm, d), lambda i: (i, 0)),
        out_shape=jax.ShapeDtypeStruct((m, d), F32),
        compiler_params=_cparams(("parallel",)),
        name="ffn",
    )(x, g, wg, wu, wd, fg)


def _attn_proj_kernel(x_ref, g_ref, w_ref, cos_ref, sin_ref, cw_ref, at_ref, sc_ref, new_ref, old_ref, *, t, seq):
    i = pl.program_id(0)
    aw = 3 * ATTN_WIDTH

    @pl.when(i == 0)
    def _():
        new_ref[...] = jnp.zeros_like(new_ref)
        old_ref[...] = jnp.zeros_like(old_ref)

    first, last = _seq_edges(i - 1, t, seq)
    xn = _rms(x_ref[...], g_ref[...]).astype(BF16)

    def attn_chunk(c):
        y = _dot(xn, w_ref[:, c * MXU_N:(c + 1) * MXU_N])
        sect = c * MXU_N // ATTN_WIDTH
        for s in range(MXU_N // 128):
            ys = y[:, s * 128:(s + 1) * 128]
            if sect < 2:
                ys = ys * cos_ref[sect] + pltpu.roll(ys, HEAD_DIM, 1) * sin_ref[sect]
            at_ref[:, c * MXU_N + s * 128:c * MXU_N + (s + 1) * 128] = ys

    def sc_chunk(c):
        new_ref[:, c * MXU_N:(c + 1) * MXU_N] = _dot(xn, w_ref[:, aw + c * MXU_N:aw + (c + 1) * MXU_N])

    def sc_slab(s):
        lanes = [slice(part * SC_WIDTH + s * 128, part * SC_WIDTH + (s + 1) * 128) for part in range(3)]
        ext = [jnp.concatenate([jnp.where(first, 0.0, old_ref[:HALO, sl]), old_ref[HALO:, sl],
                                jnp.where(last, 0.0, new_ref[:HALO, sl])], axis=0) for sl in lanes[1:]]
        out = slice(s * 128, (s + 1) * 128)
        y = old_ref[HALO:, lanes[0]] * _conv_taps(ext[0] * ext[1], cw_ref, out, SC_CONV, t)
        sc_ref[:, out] = y.astype(BF16)
        for sl in lanes:
            old_ref[:HALO, sl] = old_ref[t:, sl]
            old_ref[HALO:, sl] = new_ref[:, sl]

    per = MXU_N // 128
    for grp in range(SC_WIDTH // MXU_N):
        for part in range(3):
            sc_chunk(part * (SC_WIDTH // MXU_N) + grp)
        for j in range(aw // MXU_N // (SC_WIDTH // MXU_N)):
            attn_chunk(grp * 3 + j)
            if j < per:
                sc_slab(grp * per + j)


def _attn_proj(x, g, w, cos_t, sin_t, sc_conv, seq, *, tm=512):
    m, d = x.shape
    nt = m // tm
    nseq = seq // tm
    aw, scw = 3 * ATTN_WIDTH, 3 * SC_WIDTH
    cur_tile = lambda i: (jnp.minimum(i, nt - 1), 0)
    table = lambda i: (0, jnp.minimum(i, nt - 1) % nseq, 0)
    return pl.pallas_call(
        functools.partial(_attn_proj_kernel, t=tm, seq=seq),
        grid=(nt + 1,),
        in_specs=[
            pl.BlockSpec((tm, d), cur_tile),
            _resident((1, d)),
            _resident((d, aw + scw)),
            pl.BlockSpec((2, tm, 128), table),
            pl.BlockSpec((2, tm, 128), table),
            _resident((SC_CONV, SC_WIDTH)),
        ],
        out_specs=[
            pl.BlockSpec((tm, aw), cur_tile),
            pl.BlockSpec((tm, SC_WIDTH), lambda i: (jnp.maximum(i - 1, 0), 0)),
        ],
        out_shape=[jax.ShapeDtypeStruct((m, aw), F32), jax.ShapeDtypeStruct((m, SC_WIDTH), BF16)],
        scratch_shapes=[pltpu.VMEM((tm, scw), F32), pltpu.VMEM((HALO + tm, scw), F32)],
        compiler_params=_cparams(("arbitrary",)),
        name="attn_proj",
    )(x, g, w, cos_t, sin_t, sc_conv)


def _proj_kernel(x_ref, g_ref, w_ref, wba_ref, cw_ref, gb_ref, pj_ref, ba_ref, qkv_ref, new_ref, old_ref, *, t, seq):
    i = pl.program_id(0)
    dnw = 3 * DN_WIDTH

    @pl.when(i == 0)
    def _():
        new_ref[...] = jnp.zeros_like(new_ref)
        old_ref[...] = jnp.zeros_like(old_ref)

    first, last = _seq_edges(i - 1, t, seq)
    xn = _rms(x_ref[...], g_ref[...]).astype(BF16)
    ba_ref[...] = _dot(xn, wba_ref[...])
    nslab = dnw // DN_HEAD_DIM
    npj = PJ_W // MXU_N
    for s in range(nslab):
        if s % (MXU_N // DN_HEAD_DIM) == 0:
            cs = slice(s * DN_HEAD_DIM, s * DN_HEAD_DIM + MXU_N)
            new_ref[:, cs] = _dot(xn, w_ref[:, cs])
        for c in range(npj * s // nslab, npj * (s + 1) // nslab):
            cs = slice(c * MXU_N, (c + 1) * MXU_N)
            y = _dot(xn, w_ref[:, dnw + c * MXU_N:dnw + (c + 1) * MXU_N])
            if cs.stop <= PJ_Z:
                y = jax.nn.sigmoid(y + gb_ref[:, cs])
            else:
                y = y * jax.nn.sigmoid(y)
            pj_ref[:, cs] = y.astype(BF16)
        sl = slice(s * DN_HEAD_DIM, (s + 1) * DN_HEAD_DIM)
        ext = jnp.concatenate([jnp.where(first, 0.0, old_ref[:HALO, sl]), old_ref[HALO:, sl],
                               jnp.where(last, 0.0, new_ref[:HALO, sl])], axis=0)
        y = _conv_taps(ext, cw_ref, sl, DN_CONV, t)
        y = y * jax.nn.sigmoid(y)
        if s < 2 * DN_HEADS:
            scale = DN_HEAD_DIM ** -0.5 if s < DN_HEADS else 1.0
            y = y * (lax.rsqrt(jnp.sum(y * y, axis=-1, keepdims=True) + NORM_EPS) * scale)
        qkv_ref[:, sl] = y.astype(BF16)
        old_ref[:HALO, sl] = old_ref[t:, sl]
        old_ref[HALO:, sl] = new_ref[:, sl]


def _proj(x, g, w, w_ba, dn_conv, gate_bias, seq, *, tm=512):
    m, d = x.shape
    nt = m // tm
    dnw = 3 * DN_WIDTH
    cur_tile = lambda i: (jnp.minimum(i, nt - 1), 0)
    return pl.pallas_call(
        functools.partial(_proj_kernel, t=tm, seq=seq),
        grid=(nt + 1,),
        in_specs=[
            pl.BlockSpec((tm, d), cur_tile),
            _resident((1, d)),
            _resident((d, dnw + PJ_W)),
            _resident((d, BA_W)),
            _resident((DN_CONV, dnw)),
            _resident((1, N_BRANCH * d)),
        ],
        out_specs=[
            pl.BlockSpec((tm, PJ_W), cur_tile),
            pl.BlockSpec((tm, BA_W), cur_tile),
            pl.BlockSpec((tm, dnw), lambda i: (jnp.maximum(i - 1, 0), 0)),
        ],
        out_shape=[
            jax.ShapeDtypeStruct((m, PJ_W), BF16),
            jax.ShapeDtypeStruct((m, BA_W), F32),
            jax.ShapeDtypeStruct((m, dnw), BF16),
        ],
        scratch_shapes=[pltpu.VMEM((tm, dnw), F32), pltpu.VMEM((HALO + tm, dnw), F32)],
        compiler_params=_cparams(("arbitrary",)),
        name="proj",
    )(x, g, w, w_ba, dn_conv, gate_bias)


def _rope_tables(seq):
    half = ROT_DIM // 2
    inv_freq = ROPE_THETA ** (-2.0 * jnp.arange(half, dtype=F32) / ROT_DIM)
    ang = jnp.arange(seq, dtype=F32)[:, None] * inv_freq[None, :]
    cos, sin = jnp.cos(ang), jnp.sin(ang)
    ones = jnp.ones((seq, HEAD_DIM - ROT_DIM), F32)
    cos_h = jnp.concatenate([cos, cos, ones, cos, cos, ones], axis=1)
    sin_h = jnp.concatenate([-sin, -sin, 0.0 * ones, sin, sin, 0.0 * ones], axis=1)
    qs = HEAD_DIM ** -0.5
    return jnp.stack([cos_h * qs, cos_h]), jnp.stack([sin_h * qs, sin_h])


def _qk_perm():
    a = jnp.arange
    half = ROT_DIM // 2
    rest = (HEAD_DIM - ROT_DIM) // 2
    lo = [a(0, half), a(HEAD_DIM, HEAD_DIM + half), a(ROT_DIM, ROT_DIM + rest),
          a(HEAD_DIM + ROT_DIM, HEAD_DIM + ROT_DIM + rest)]
    hi = [a(half, ROT_DIM), a(HEAD_DIM + half, HEAD_DIM + ROT_DIM), a(ROT_DIM + rest, HEAD_DIM),
          a(HEAD_DIM + ROT_DIM + rest, 2 * HEAD_DIM)]
    return jnp.concatenate(lo + hi)


def _qk_first_head(lane):
    half = ROT_DIM // 2
    rest = (HEAD_DIM - ROT_DIM) // 2
    l = lane % HEAD_DIM
    return (l < half) | ((l >= ROT_DIM) & (l < ROT_DIM + rest))


def _attn_kernel(q_ref, k_ref, v_ref, bias_ref, o_ref, lse_ref, kres_ref, vres_ref, *, dil, tq, tk, half, length, nsub):
    i = pl.program_id(2)

    @pl.when(i == 0)
    def _():
        for r in range(dil):
            rows = pl.ds(r, length, stride=dil) if dil > 1 else pl.ds(0, length)
            kres_ref[r * length:(r + 1) * length, :] = k_ref[0, rows, :].astype(BF16)
            vres_ref[r * length:(r + 1) * length, :] = v_ref[0, rows, :].astype(BF16)

    lane = lax.broadcasted_iota(jnp.int32, (1, 128), 1)
    first_out = lane < HEAD_DIM
    first_qk = _qk_first_head(lane)

    def group(g, carry):
        rows, bias, q, kw, vw = [], [], [], [], []
        for j in range(ATTN_UNITS):
            u = g * ATTN_UNITS + j
            r = u % dil
            sub = u // dil
            blk = i * nsub + sub
            ks = pl.multiple_of(jnp.clip(blk * tq - half, 0, length - tk), 64)
            bias.append(bias_ref[(blk * tq - ks) // half])
            if dil > 1:
                rows.append(pl.ds(r + sub * (tq * dil), tq, stride=dil))
            else:
                rows.append(pl.ds(pl.multiple_of(sub * tq, tq), tq))
            q.append(q_ref[0, rows[j], :].astype(BF16))
            kbase = pl.multiple_of(r * length + ks, 64)
            kw.append(kres_ref[pl.ds(kbase, tk), :])
            vw.append(vres_ref[pl.ds(kbase, tk), :])
        chains = [(j, sel) for j in range(ATTN_UNITS) for sel in (first_qk, ~first_qk)]
        sc = [_dot_nt(jnp.where(sel, q[j], jnp.zeros_like(q[j])), kw[j]) for j, sel in chains]
        sc = [s + bias[j] for (j, _), s in zip(chains, sc)]
        mx = [jnp.max(s, axis=-1, keepdims=True) for s in sc]
        p = [jnp.exp(s - m) for s, m in zip(sc, mx)]
        den = [jnp.sum(t, axis=-1, keepdims=True) for t in p]
        o = [_dot(t.astype(BF16), vw[j]) / d for (j, _), t, d in zip(chains, p, den)]
        lse = [m + jnp.log(d) for m, d in zip(mx, den)]
        for j in range(ATTN_UNITS):
            o_ref[0, rows[j], :] = jnp.where(first_out, o[2 * j], o[2 * j + 1])
            lse_ref[0, rows[j], :] = jnp.where(first_out, lse[2 * j], lse[2 * j + 1])
        return carry

    lax.fori_loop(0, dil * nsub // ATTN_UNITS, group, 0)


def _attn_group(at, batch, seq, gi):
    window, dil = ATTN_GROUPS[gi]
    half = (window // 2) // dil
    length = seq // dil
    tq = 2 * half
    tk = tq + 2 * half
    nsub = max(1, 2 * ATTN_UNITS // dil)
    assert half == 64 and seq % dil == 0 and tk <= length and length % (tq * nsub) == 0
    assert (dil * nsub) % ATTN_UNITS == 0
    tb = tq * dil * nsub
    at3 = at.reshape(batch, seq, 3 * ATTN_WIDTH)
    nslab = ATTN_GW // 128
    nb = ATTN_WIDTH // 128
    off = (jnp.arange(3) * half)[:, None, None]
    dist = jnp.arange(tq)[None, :, None] + off - jnp.arange(tk)[None, None, :]
    bias = jnp.where(jnp.abs(dist) <= half, 0.0, -jnp.inf).astype(F32)
    o, lse = pl.pallas_call(
        functools.partial(_attn_kernel, dil=dil, tq=tq, tk=tk, half=half, length=length, nsub=nsub),
        grid=(batch, nslab, seq // tb),
        in_specs=[
            pl.BlockSpec((1, tb, 128), lambda b, s, i: (b, i, gi * nslab + s)),
            pl.BlockSpec((1, seq, 128), lambda b, s, i: (b, 0, nb + gi * nslab + s), pipeline_mode=pl.Buffered(1)),
            pl.BlockSpec((1, seq, 128), lambda b, s, i: (b, 0, 2 * nb + gi * nslab + s), pipeline_mode=pl.Buffered(1)),
            _resident((3, tq, tk)),
        ],
        out_specs=[
            pl.BlockSpec((1, tb, 128), lambda b, s, i: (b, i, s)),
            pl.BlockSpec((1, tb, 128), lambda b, s, i: (b, i, s)),
        ],
        out_shape=[jax.ShapeDtypeStruct((batch, seq, ATTN_GW), F32)] * 2,
        scratch_shapes=[pltpu.VMEM((seq, 128), BF16), pltpu.VMEM((seq, 128), BF16)],
        compiler_params=_cparams(("parallel", "parallel", "arbitrary")),
        name=f"attn{gi}",
    )(at3, at3, at3, bias)
    return o.reshape(batch * seq, ATTN_GW), lse.reshape(batch * seq, ATTN_GW)


def _tap(ext, off, t):
    n = ext.shape[0]
    return pltpu.roll(ext, (-off) % n, 0)[HALO:HALO + t] if off else ext[HALO:HALO + t]


def _seq_edges(i, t, seq):
    p0 = (i * t) % seq
    return p0 == 0, p0 + t == seq


def _conv_taps(ext, cw_ref, sl, ntap, t):
    acc = None
    for tap in range(ntap):
        term = _tap(ext, tap - ntap // 2, t) * cw_ref[tap:tap + 1, sl]
        acc = term if acc is None else acc + term
    return acc


def _dn_kernel(q_ref, k_ref, v_ref, ba_ref, alog_ref, dtb_ref, tri_ref, *refs, ct, reverse, lane0, add_prev):
    prev_ref = refs[0] if add_prev else None
    o_ref, st_ref, beta_ref, gc_ref, gct_ref, *slot_refs = refs[1:] if add_prev else refs
    _dn_body(q_ref, k_ref, v_ref, ba_ref, alog_ref, dtb_ref, tri_ref, prev_ref, o_ref, st_ref, beta_ref, gc_ref,
             gct_ref, slot_refs, ct=ct, reverse=reverse, lane0=lane0)


def _dn_body(q_ref, k_ref, v_ref, ba_ref, alog_ref, dtb_ref, tri_ref, prev_ref, o_ref,
             st_ref, beta_ref, gc_ref, gct_ref, slot_refs, *, ct, reverse, lane0):
    c = DN_CHUNK
    nchunk = ct // c
    heads = range(DN_HEADS)
    hsl = [slice(h * DN_HEAD_DIM, (h + 1) * DN_HEAD_DIM) for h in heads]
    nslot = 2 * DN_AHEAD
    u_refs, wq_refs, upd_refs = slot_refs[:nslot], slot_refs[nslot:2 * nslot], slot_refs[2 * nslot:]

    @pl.when(pl.program_id(1) == 0)
    def _():
        st_ref[...] = jnp.zeros_like(st_ref)

    ba = ba_ref[0]
    beta_all = jax.nn.sigmoid(ba)
    xg = ba + dtb_ref[...]
    softplus = jnp.maximum(xg, 0.0) + jnp.log1p(jnp.exp(-jnp.abs(xg)))
    g_all = -jnp.exp(alog_ref[...]) * softplus
    gc_all = _dot_exact_lhs(tri_ref[...], g_all)
    for h in heads:
        lb = lane0 - BA_G0 + h
        beta_ref[:, hsl[h]] = jnp.broadcast_to(beta_all[:, lb:lb + 1], (ct, DN_HEAD_DIM))
        gc_ref[:, hsl[h]] = jnp.broadcast_to(gc_all[:, lane0 + h:lane0 + h + 1], (ct, DN_HEAD_DIM))
    for n in range(nchunk):
        gct_ref[n] = gc_all[n * c:(n + 1) * c, :].T

    row = lax.broadcasted_iota(jnp.int32, (c, c), 0)
    col = lax.broadcasted_iota(jnp.int32, (c, c), 1)
    incl = (col >= row) if reverse else (col <= row)
    strict = (col > row) if reverse else (col < row)
    upper = lax.broadcasted_iota(jnp.int32, (c, 2 * c), 1) >= c
    eye_hi = (lax.broadcasted_iota(jnp.int32, (c, 2 * c), 1) - c
              == lax.broadcasted_iota(jnp.int32, (c, 2 * c), 0)).astype(F32)

    def local(chunks):
        units = [(n, slot, h) for n, slot in chunks for h in heads]
        idx = range(len(units))
        rows = [slice(n * c, (n + 1) * c) for n, _, _ in units]
        last = [n * c + (0 if reverse else c - 1) for n, _, _ in units]
        hs = [hsl[h] for _, _, h in units]
        k16 = [k_ref[0, rows[i], hs[i]] for i in idx]
        q = [q_ref[0, rows[i], hs[i]].astype(F32) for i in idx]
        v = [v_ref[0, rows[i], hs[i]].astype(F32) for i in idx]
        beta = [beta_ref[rows[i], hs[i]] for i in idx]
        gcx = [gc_ref[rows[i], hs[i]] for i in idx]
        gtot = [gc_ref[last[i]:last[i] + 1, hs[i]] for i in idx]
        grow = [gct_ref[n, lane0 + h:lane0 + h + 1, :] for n, _, h in units]
        k = [t.astype(F32) for t in k16]
        kb = [k[i] * beta[i] for i in idx]
        kq = [_dot_nt(jnp.concatenate([kb[i], q[i]], axis=0).astype(BF16), k16[i]) for i in idx]
        decay = [jnp.where(incl, jnp.exp(jnp.where(incl, gcx[i][:, :c] - grow[i], 0.0)), 0.0) for i in idx]
        intra = [jnp.where(incl, kq[i][c:] * decay[i], 0.0) for i in idx]
        mk = [jnp.where(strict, -kq[i][:c] * decay[i], 0.0) for i in idx]
        z = [jnp.concatenate([mk[i], jnp.zeros_like(mk[i])], axis=1) + eye_hi for i in idx]
        for _ in range(int(math.log2(c))):
            y = [_dot(z[i][:, :c].astype(BF16), z[i].astype(BF16)) for i in idx]
            z = [y[i] + jnp.where(upper, z[i], 0.0) for i in idx]
        eg = [jnp.exp(gcx[i]) for i in idx]
        rhs = [jnp.concatenate([v[i] * beta[i], kb[i] * eg[i]], axis=1).astype(BF16) for i in idx]
        rhs = [jnp.concatenate([jnp.zeros_like(t), t], axis=0) for t in rhs]
        sol = [_dot(z[i].astype(BF16), rhs[i]) for i in idx]
        kdec_t = [(k[i].T * jnp.exp(gtot[i][:, :c] - grow[i])).astype(BF16) for i in idx]
        for i, (_, slot, h) in enumerate(units):
            u_refs[slot][h] = sol[i][:, :DN_HEAD_DIM]
            wq_refs[slot][h, :c] = sol[i][:, DN_HEAD_DIM:].astype(BF16)
            wq_refs[slot][h, c:] = (q[i] * eg[i]).astype(BF16)
            upd_refs[slot][h, :c] = intra[i].astype(BF16)
            upd_refs[slot][h, c:] = kdec_t[i]

    def recur(n, slot, state):
        rows = slice(n * c, (n + 1) * c)
        last = n * c + (0 if reverse else c - 1)
        etot = [jnp.exp(gc_ref[last:last + 1, hsl[h]]) for h in heads]
        wq = [_dot(wq_refs[slot][h], state[h].astype(BF16)) for h in heads]
        v_new = [(u_refs[slot][h] - wq[h][:c]).astype(BF16) for h in heads]
        upd = [_dot(upd_refs[slot][h], v_new[h]) for h in heads]
        for h in heads:
            o = wq[h][c:] + upd[h][:c]
            if prev_ref is not None:
                o = o + prev_ref[0, rows, hsl[h]].astype(F32)
            o_ref[0, rows, hsl[h]] = o.astype(o_ref.dtype)
        return [state[h] * etot[h] + upd[h][c:] for h in heads]

    order = list(range(nchunk))[::-1] if reverse else list(range(nchunk))
    groups = [order[i:i + DN_AHEAD] for i in range(0, nchunk, DN_AHEAD)]
    slots = lambda gi: [(n, (gi % 2) * DN_AHEAD + j) for j, n in enumerate(groups[gi])]
    state = [st_ref[h] for h in heads]
    local(slots(0))
    for gi in range(len(groups)):
        for n, slot in slots(gi):
            state = recur(n, slot, state)
        if gi + 1 < len(groups):
            local(slots(gi + 1))
    for h in heads:
        st_ref[h] = state[h]


def _dn_tri(ct, reverse):
    c = DN_CHUNK
    r = jnp.arange(ct)
    same = (r[:, None] // c) == (r[None, :] // c)
    tri = same & ((r[None, :] >= r[:, None]) if reverse else (r[None, :] <= r[:, None]))
    return tri.astype(BF16)


def _deltanet(qkv, ba, alog_row, dtb_row, batch, seq, direction, prev=None, *, ct=512):
    reverse = direction == 1
    nstep = seq // ct
    w = DN_WIDTH
    c = DN_CHUNK
    qkv3 = qkv.reshape(batch, seq, 3 * w)
    ba3 = ba.reshape(batch, seq, BA_W)
    step = (lambda n: nstep - 1 - n) if reverse else (lambda n: n)
    out = pl.pallas_call(
        functools.partial(_dn_kernel, ct=ct, reverse=reverse, lane0=BA_G0 + direction * DN_HEADS,
                          add_prev=prev is not None),
        grid=(batch, nstep),
        in_specs=[
            pl.BlockSpec((1, ct, w), lambda b, n: (b, step(n), 0)),
            pl.BlockSpec((1, ct, w), lambda b, n: (b, step(n), 1)),
            pl.BlockSpec((1, ct, w), lambda b, n: (b, step(n), 2)),
            pl.BlockSpec((1, ct, BA_W), lambda b, n: (b, step(n), 0)),
            _resident((1, BA_W)),
            _resident((1, BA_W)),
            _resident((ct, ct)),
        ] + ([] if prev is None else [pl.BlockSpec((1, ct, w), lambda b, n: (b, step(n), 0))]),
        out_specs=pl.BlockSpec((1, ct, w), lambda b, n: (b, step(n), 0)),
        out_shape=jax.ShapeDtypeStruct((batch, seq, w), BF16),
        scratch_shapes=[
            pltpu.VMEM((DN_HEADS, DN_HEAD_DIM, DN_HEAD_DIM), F32),
            pltpu.VMEM((ct, w), F32),
            pltpu.VMEM((ct, w), F32),
            pltpu.VMEM((ct // c, BA_W, c), F32),
        ]
        + [pltpu.VMEM((DN_HEADS, c, DN_HEAD_DIM), F32)] * (2 * DN_AHEAD)
        + [pltpu.VMEM((DN_HEADS, 2 * c, DN_HEAD_DIM), BF16)] * (2 * DN_AHEAD)
        + [pltpu.VMEM((DN_HEADS, c + DN_HEAD_DIM, c), BF16)] * (2 * DN_AHEAD),
        compiler_params=_cparams(("parallel", "arbitrary")),
        name=f"deltanet{direction}",
    )(qkv3, qkv3, qkv3, ba3, alog_row, dtb_row, _dn_tri(ct, reverse),
      *([] if prev is None else [prev.reshape(batch, seq, w)]))
    return out.reshape(batch * seq, w)


def _merge_kernel(x_ref, o1_ref, o2_ref, o3_ref, l1_ref, l2_ref, l3_ref, dn_ref, z_ref, sc_ref, gate_ref,
                  dng_ref, wa_ref, wd_ref, ws_ref, wo_ref, out_ref):
    l1, l2, l3 = l1_ref[...], l2_ref[...], l3_ref[...]
    mx = jnp.maximum(jnp.maximum(l1, l2), l3)
    e1, e2, e3 = jnp.exp(l1 - mx), jnp.exp(l2 - mx), jnp.exp(l3 - mx)
    attn = (o1_ref[...] * e1 + o2_ref[...] * e2 + o3_ref[...] * e3) / (e1 + e2 + e3)
    y_attn = _dot(attn.astype(BF16), wa_ref[...])
    dn = dn_ref[...].astype(F32)
    dng = dng_ref[...]
    parts = []
    for h in range(DN_HEADS):
        hs = slice(h * DN_HEAD_DIM, (h + 1) * DN_HEAD_DIM)
        oh = dn[:, hs]
        oh = oh * lax.rsqrt(jnp.mean(oh * oh, axis=-1, keepdims=True) + NORM_EPS) * dng
        parts.append((oh * z_ref[:, hs].astype(F32)).astype(BF16))
    y_dn = _dot(jnp.concatenate(parts, axis=1), wd_ref[...])
    y_sc = _dot(sc_ref[...], ws_ref[...])
    gates = gate_ref[...].astype(F32)
    d = D_MODEL
    merged = gates[:, :d] * y_attn + gates[:, d:2 * d] * y_dn + gates[:, 2 * d:] * y_sc
    out_ref[...] = x_ref[...] + _dot(merged.astype(BF16), wo_ref[...])


def _merge(x, attn_o, attn_l, dn, pj, sc, dn_norm, wa, wd, ws, wo, *, tm=512):
    m, d = x.shape
    row = lambda i: (i, 0)
    gw = N_BRANCH * d
    return pl.pallas_call(
        _merge_kernel,
        grid=(m // tm,),
        in_specs=[pl.BlockSpec((tm, d), row)]
        + [pl.BlockSpec((tm, ATTN_GW), row)] * 6
        + [pl.BlockSpec((tm, DN_WIDTH), row)]
        + [
            pl.BlockSpec((tm, DN_WIDTH), lambda i: (i, PJ_Z // DN_WIDTH)),
            pl.BlockSpec((tm, SC_WIDTH), row),
            pl.BlockSpec((tm, gw), lambda i: (i, PJ_GATE // gw)),
            _resident((1, DN_HEAD_DIM)),
            _resident((ATTN_GW, d)),
            _resident((DN_WIDTH, d)),
            _resident((SC_WIDTH, d)),
            _resident((d, d)),
        ],
        out_specs=pl.BlockSpec((tm, d), row),
        out_shape=jax.ShapeDtypeStruct((m, d), F32),
        compiler_params=_cparams(("parallel",)),
        name="merge",
    )(x, *attn_o, *attn_l, dn, pj, sc, pj, dn_norm, wa, wd, ws, wo)


def _prep_layer(l, p):
    w_in = p["w_in"][l]
    o_attn, o_dn = 0, 3 * ATTN_WIDTH
    o_z = o_dn + 3 * DN_WIDTH
    o_beta = o_z + DN_WIDTH
    o_sc = o_beta + 4 * DN_HEADS
    o_gate = o_sc + 3 * SC_WIDTH
    bf = lambda t: t.astype(BF16)
    w_qk = w_in[:, o_attn:o_attn + 2 * ATTN_WIDTH].reshape(D_MODEL, -1, 128)[:, :, _qk_perm()]
    w_attn = bf(jnp.concatenate([w_qk.reshape(D_MODEL, -1), w_in[:, o_attn + 2 * ATTN_WIDTH:o_dn], w_in[:, o_sc:o_gate]], axis=1))
    w_proj = bf(jnp.concatenate([w_in[:, o_dn:o_z], w_in[:, o_gate:], w_in[:, o_z:o_beta]], axis=1))
    w_ba = jnp.pad(w_in[:, o_beta:o_sc], ((0, 0), (0, BA_W - 4 * DN_HEADS)))
    pad_row = lambda t: jnp.pad(t.reshape(1, 2 * DN_HEADS), ((0, 0), (BA_G0, BA_W - BA_G0 - 2 * DN_HEADS)))
    row = lambda t: t.reshape(1, -1)
    return dict(
        ffn1=(row(p["ffn1_norm"][l]), bf(p["ffn1_w_gate"][l]), bf(p["ffn1_w_up"][l]), bf(p["ffn1_w_down"][l])),
        ffn2=(row(p["ffn2_norm"][l]), bf(p["ffn2_w_gate"][l]), bf(p["ffn2_w_up"][l]), bf(p["ffn2_w_down"][l])),
        mix_norm=row(p["mix_norm"][l]),
        w_attn=w_attn,
        w_proj=w_proj,
        w_ba=bf(w_ba),
        dn_conv=p["dn_conv"][l],
        alog=pad_row(p["dn_a_log"][l]),
        dtb=pad_row(p["dn_dt_bias"][l]),
        dn_norm=row(p["dn_norm"][l]),
        sc_conv=p["sc_conv"][l],
        gate_bias=row(p["gate_bias"][l]),
        wa=bf(p["w_attn_br"][l]),
        wd=bf(p["w_dn_br"][l]),
        ws=bf(p["w_sc_br"][l]),
        wo=bf(p["w_out"][l]),
    )


def _mixer(x, lp, batch, seq, tables):
    at, sc = _attn_proj(x, lp["mix_norm"], lp["w_attn"], tables[0], tables[1], lp["sc_conv"], seq)
    pj, ba, qkv = _proj(x, lp["mix_norm"], lp["w_proj"], lp["w_ba"], lp["dn_conv"], lp["gate_bias"], seq)
    attn = [_attn_group(at, batch, seq, gi) for gi in range(len(ATTN_GROUPS))]
    dn = _deltanet(qkv, ba, lp["alog"], lp["dtb"], batch, seq, 0)
    dn = _deltanet(qkv, ba, lp["alog"], lp["dtb"], batch, seq, 1, prev=dn)
    return _merge(x, [a[0] for a in attn], [a[1] for a in attn], dn, pj, sc,
                  lp["dn_norm"], lp["wa"], lp["wd"], lp["ws"], lp["wo"])


def _trunk(x, layers, final_g):
    batch, seq, d = x.shape
    tables = _rope_tables(seq)
    x = x.reshape(batch * seq, d)
    for l, lp in enumerate(layers):
        x = _ffn(x, *lp["ffn1"])
        x = _mixer(x, lp, batch, seq, tables)
        x = _ffn(x, *lp["ffn2"], final_g=final_g if l == len(layers) - 1 else None)
    return x.reshape(batch, seq, d)


def kernel(x_prompt, x_sample, ffn1_norm, ffn1_w_gate, ffn1_w_up, ffn1_w_down, mix_norm, w_in, dn_conv, dn_a_log, dn_dt_bias, dn_norm, sc_conv, gate_bias, w_attn_br, w_dn_br, w_sc_br, w_out, ffn2_norm, ffn2_w_gate, ffn2_w_up, ffn2_w_down, final_norm):
    p = dict(ffn1_norm=ffn1_norm, ffn1_w_gate=ffn1_w_gate, ffn1_w_up=ffn1_w_up, ffn1_w_down=ffn1_w_down,
             mix_norm=mix_norm, w_in=w_in, dn_conv=dn_conv, dn_a_log=dn_a_log, dn_dt_bias=dn_dt_bias,
             dn_norm=dn_norm, sc_conv=sc_conv, gate_bias=gate_bias, w_attn_br=w_attn_br, w_dn_br=w_dn_br,
             w_sc_br=w_sc_br, w_out=w_out, ffn2_norm=ffn2_norm, ffn2_w_gate=ffn2_w_gate, ffn2_w_up=ffn2_w_up,
             ffn2_w_down=ffn2_w_down)
    layers = [_prep_layer(l, p) for l in range(ffn1_norm.shape[0])]
    final_g = final_norm.reshape(1, -1)
    return _trunk(x_prompt, layers, final_g), _trunk(x_sample, layers, final_g)
```

```python
import functools
import math

import jax
import jax.numpy as jnp
from jax import lax
from jax.experimental import pallas as pl
from jax.experimental.pallas import tpu as pltpu

F32 = jnp.float32
BF16 = jnp.bfloat16

D_MODEL = 1024
HEAD_DIM = 64
ATTN_GROUPS = ((128, 1), (512, 4), (2048, 16))
ATTN_GROUP_HEADS = 4
ATTN_GW = ATTN_GROUP_HEADS * HEAD_DIM
ATTN_WIDTH = 3 * ATTN_GW
ROT_DIM = HEAD_DIM // 4
ROPE_THETA = 500000.0
DN_HEADS = 6
DN_HEAD_DIM = 128
DN_WIDTH = DN_HEADS * DN_HEAD_DIM
DN_CONV = 5
DN_CHUNK = 64
SC_WIDTH = 768
SC_CONV = 3
FFN_DIM = 2816
N_BRANCH = 3
NORM_EPS = 1e-6

MXU_N = 256
PJ_GATE = 0
PJ_Z = PJ_GATE + N_BRANCH * D_MODEL
PJ_W = PJ_Z + DN_WIDTH
BA_W = 128
BA_G0 = 2 * DN_HEADS

DN_AHEAD = 4
ATTN_UNITS = 4
HALO = 8

VMEM_LIMIT = 56 * 1024 * 1024


def _cparams(sem):
    return pltpu.CompilerParams(dimension_semantics=sem, vmem_limit_bytes=VMEM_LIMIT)


def _resident(shape):
    return pl.BlockSpec(shape, lambda *_: (0,) * len(shape), pipeline_mode=pl.Buffered(1))


def _rms(x, g):
    return x * lax.rsqrt(jnp.mean(x * x, axis=-1, keepdims=True) + NORM_EPS) * g


def _dot(a, b):
    return jnp.dot(a, b, preferred_element_type=F32)


def _dot_nt(a, b):
    return lax.dot_general(a, b, (((1,), (1,)), ((), ())), preferred_element_type=F32)


def _dot_exact_lhs(sel, x):
    x1 = x.astype(BF16)
    r = x - x1.astype(F32)
    x2 = r.astype(BF16)
    x3 = (r - x2.astype(F32)).astype(BF16)
    return _dot(sel, x1) + _dot(sel, x2) + _dot(sel, x3)


def _ffn_kernel(x_ref, g_ref, wg_ref, wu_ref, wd_ref, fg_ref, o_ref, *, tf, final):
    x = x_ref[...]
    xn = _rms(x, g_ref[...]).astype(BF16)
    acc = jnp.zeros_like(x)
    for c in range(FFN_DIM // tf):
        sl = slice(c * tf, (c + 1) * tf)
        hg = _dot(xn, wg_ref[:, sl])
        hu = _dot(xn, wu_ref[:, sl])
        h = (hg * jax.nn.sigmoid(hg) * hu).astype(BF16)
        acc = acc + _dot(h, wd_ref[sl, :])
    y = x + 0.5 * acc
    if final:
        y = _rms(y, fg_ref[...])
    o_ref[...] = y


def _ffn(x, g, wg, wu, wd, final_g=None, *, tm=1024, tf=256):
    m, d = x.shape
    final = final_g is not None
    fg = final_g if final else g
    return pl.pallas_call(
        functools.partial(_ffn_kernel, tf=tf, final=final),
        grid=(m // tm,),
        in_specs=[
            pl.BlockSpec((tm, d), lambda i: (i, 0)),
            _resident((1, d)),
            _resident((d, FFN_DIM)),
            _resident((d, FFN_DIM)),
            _resident((FFN_DIM, d)),
            _resident((1, d)),
        ],
        out_specs=pl.BlockSpec((tm, d), lambda i: (i, 0)),
        out_shape=jax.ShapeDtypeStruct((m, d), F32),
        compiler_params=_cparams(("parallel",)),
        name="ffn",
    )(x, g, wg, wu, wd, fg)


def _tap(ext, off, t):
    n = ext.shape[0]
    return pltpu.roll(ext, (-off) % n, 0)[HALO:HALO + t] if off else ext[HALO:HALO + t]


def _seq_edges(i, t, seq):
    p0 = (i * t) % seq
    return p0 == 0, p0 + t == seq


def _conv_taps(ext, cw_ref, sl, ntap, t):
    acc = None
    for tap in range(ntap):
        term = _tap(ext, tap - ntap // 2, t) * cw_ref[tap:tap + 1, sl]
        acc = term if acc is None else acc + term
    return acc


def _halo_ext(old_ref, new_ref, sl, first, last):
    return jnp.concatenate([jnp.where(first, 0.0, old_ref[:HALO, sl]), old_ref[HALO:, sl],
                            jnp.where(last, 0.0, new_ref[:HALO, sl])], axis=0)


def _advance_tile(old_ref, new_ref, sl, t):
    old_ref[:HALO, sl] = old_ref[t:, sl]
    old_ref[HALO:, sl] = new_ref[:, sl]


def _dn_prep_slab(old_ref, new_ref, cw_ref, sl, cw_sl, first, last, t, norm_scale):
    y = _conv_taps(_halo_ext(old_ref, new_ref, sl, first, last), cw_ref, cw_sl, DN_CONV, t)
    y = y * jax.nn.sigmoid(y)
    if norm_scale is not None:
        y = y * (lax.rsqrt(jnp.sum(y * y, axis=-1, keepdims=True) + NORM_EPS) * norm_scale)
    _advance_tile(old_ref, new_ref, sl, t)
    return y.astype(BF16)


def _init_tiles(new_ref, old_ref):
    @pl.when(pl.program_id(0) == 0)
    def _():
        new_ref[...] = jnp.zeros_like(new_ref)
        old_ref[...] = jnp.zeros_like(old_ref)


def _attn_proj_kernel(x_ref, g_ref, w_ref, cos_ref, sin_ref, scw_ref, dvw_ref, at_ref, sc_ref, v_ref, new_ref, old_ref,
                      *, t, seq):
    aw, scw = 3 * ATTN_WIDTH, 3 * SC_WIDTH
    _init_tiles(new_ref, old_ref)
    first, last = _seq_edges(pl.program_id(0) - 1, t, seq)
    xn = _rms(x_ref[...], g_ref[...]).astype(BF16)

    def attn_chunk(c):
        y = _dot(xn, w_ref[:, c * MXU_N:(c + 1) * MXU_N])
        sect = c * MXU_N // ATTN_WIDTH
        for s in range(MXU_N // 128):
            ys = y[:, s * 128:(s + 1) * 128]
            if sect < 2:
                ys = ys * cos_ref[sect] + pltpu.roll(ys, HEAD_DIM, 1) * sin_ref[sect]
            at_ref[:, c * MXU_N + s * 128:c * MXU_N + (s + 1) * 128] = ys

    def raw_chunk(c):
        new_ref[:, c * MXU_N:(c + 1) * MXU_N] = _dot(xn, w_ref[:, aw + c * MXU_N:aw + (c + 1) * MXU_N])

    def sc_slab(s):
        lanes = [slice(part * SC_WIDTH + s * 128, part * SC_WIDTH + (s + 1) * 128) for part in range(3)]
        out = slice(s * 128, (s + 1) * 128)
        cx = _halo_ext(old_ref, new_ref, lanes[1], first, last) * _halo_ext(old_ref, new_ref, lanes[2], first, last)
        sc_ref[:, out] = (old_ref[HALO:, lanes[0]] * _conv_taps(cx, scw_ref, out, SC_CONV, t)).astype(BF16)
        for sl in lanes:
            _advance_tile(old_ref, new_ref, sl, t)

    def v_slab(s):
        out = slice(s * DN_HEAD_DIM, (s + 1) * DN_HEAD_DIM)
        sl = slice(scw + out.start, scw + out.stop)
        v_ref[:, out] = _dn_prep_slab(old_ref, new_ref, dvw_ref, sl, out, first, last, t, None)

    ngrp = SC_WIDTH // MXU_N
    per = MXU_N // 128
    for grp in range(ngrp):
        for part in range(3):
            raw_chunk(part * ngrp + grp)
        raw_chunk(3 * ngrp + grp)
        for j in range(aw // MXU_N // ngrp):
            attn_chunk(grp * (aw // MXU_N // ngrp) + j)
            if j < per:
                sc_slab(grp * per + j)
                v_slab(grp * per + j)


def _attn_proj(x, g, w, cos_t, sin_t, sc_conv, dn_conv_v, seq, *, tm=512):
    m, d = x.shape
    nt = m // tm
    nseq = seq // tm
    aw, raww = 3 * ATTN_WIDTH, 3 * SC_WIDTH + DN_WIDTH
    cur_tile = lambda i: (jnp.minimum(i, nt - 1), 0)
    prev_tile = lambda i: (jnp.maximum(i - 1, 0), 0)
    table = lambda i: (0, jnp.minimum(i, nt - 1) % nseq, 0)
    return pl.pallas_call(
        functools.partial(_attn_proj_kernel, t=tm, seq=seq),
        grid=(nt + 1,),
        in_specs=[
            pl.BlockSpec((tm, d), cur_tile),
            _resident((1, d)),
            _resident((d, aw + raww)),
            pl.BlockSpec((2, tm, 128), table),
            pl.BlockSpec((2, tm, 128), table),
            _resident((SC_CONV, SC_WIDTH)),
            _resident((DN_CONV, DN_WIDTH)),
        ],
        out_specs=[
            pl.BlockSpec((tm, aw), cur_tile),
            pl.BlockSpec((tm, SC_WIDTH), prev_tile),
            pl.BlockSpec((tm, DN_WIDTH), prev_tile),
        ],
        out_shape=[
            jax.ShapeDtypeStruct((m, aw), F32),
            jax.ShapeDtypeStruct((m, SC_WIDTH), BF16),
            jax.ShapeDtypeStruct((m, DN_WIDTH), BF16),
        ],
        scratch_shapes=[pltpu.VMEM((tm, raww), F32), pltpu.VMEM((HALO + tm, raww), F32)],
        compiler_params=_cparams(("arbitrary",)),
        name="attn_proj",
    )(x, g, w, cos_t, sin_t, sc_conv, dn_conv_v)


def _proj_kernel(x_ref, g_ref, w_ref, wba_ref, cw_ref, gb_ref, pj_ref, ba_ref, qk_ref, new_ref, old_ref, *, t, seq):
    qkw = 2 * DN_WIDTH
    _init_tiles(new_ref, old_ref)
    first, last = _seq_edges(pl.program_id(0) - 1, t, seq)
    xn = _rms(x_ref[...], g_ref[...]).astype(BF16)
    ba_ref[...] = _dot(xn, wba_ref[...])
    nslab = qkw // DN_HEAD_DIM
    npj = PJ_W // MXU_N
    for s in range(nslab):
        if s % (MXU_N // DN_HEAD_DIM) == 0:
            cs = slice(s * DN_HEAD_DIM, s * DN_HEAD_DIM + MXU_N)
            new_ref[:, cs] = _dot(xn, w_ref[:, cs])
        for c in range(npj * s // nslab, npj * (s + 1) // nslab):
            cs = slice(c * MXU_N, (c + 1) * MXU_N)
            y = _dot(xn, w_ref[:, qkw + c * MXU_N:qkw + (c + 1) * MXU_N])
            if cs.stop <= PJ_Z:
                y = jax.nn.sigmoid(y + gb_ref[:, cs])
            else:
                y = y * jax.nn.sigmoid(y)
            pj_ref[:, cs] = y.astype(BF16)
        sl = slice(s * DN_HEAD_DIM, (s + 1) * DN_HEAD_DIM)
        scale = DN_HEAD_DIM ** -0.5 if s < DN_HEADS else 1.0
        qk_ref[:, sl] = _dn_prep_slab(old_ref, new_ref, cw_ref, sl, sl, first, last, t, scale)


def _proj(x, g, w, w_ba, dn_conv_qk, gate_bias, seq, *, tm=512):
    m, d = x.shape
    nt = m // tm
    qkw = 2 * DN_WIDTH
    cur_tile = lambda i: (jnp.minimum(i, nt - 1), 0)
    return pl.pallas_call(
        functools.partial(_proj_kernel, t=tm, seq=seq),
        grid=(nt + 1,),
        in_specs=[
            pl.BlockSpec((tm, d), cur_tile),
            _resident((1, d)),
            _resident((d, qkw + PJ_W)),
            _resident((d, BA_W)),
            _resident((DN_CONV, qkw)),
            _resident((1, N_BRANCH * d)),
        ],
        out_specs=[
            pl.BlockSpec((tm, PJ_W), cur_tile),
            pl.BlockSpec((tm, BA_W), cur_tile),
            pl.BlockSpec((tm, qkw), lambda i: (jnp.maximum(i - 1, 0), 0)),
        ],
        out_shape=[
            jax.ShapeDtypeStruct((m, PJ_W), BF16),
            jax.ShapeDtypeStruct((m, BA_W), F32),
            jax.ShapeDtypeStruct((m, qkw), BF16),
        ],
        scratch_shapes=[pltpu.VMEM((tm, qkw), F32), pltpu.VMEM((HALO + tm, qkw), F32)],
        compiler_params=_cparams(("arbitrary",)),
        name="proj",
    )(x, g, w, w_ba, dn_conv_qk, gate_bias)


def _rope_tables(seq):
    half = ROT_DIM // 2
    inv_freq = ROPE_THETA ** (-2.0 * jnp.arange(half, dtype=F32) / ROT_DIM)
    ang = jnp.arange(seq, dtype=F32)[:, None] * inv_freq[None, :]
    cos, sin = jnp.cos(ang), jnp.sin(ang)
    ones = jnp.ones((seq, HEAD_DIM - ROT_DIM), F32)
    cos_h = jnp.concatenate([cos, cos, ones, cos, cos, ones], axis=1)
    sin_h = jnp.concatenate([-sin, -sin, 0.0 * ones, sin, sin, 0.0 * ones], axis=1)
    qs = HEAD_DIM ** -0.5
    return jnp.stack([cos_h * qs, cos_h]), jnp.stack([sin_h * qs, sin_h])


def _qk_perm():
    a = jnp.arange
    half = ROT_DIM // 2
    rest = (HEAD_DIM - ROT_DIM) // 2
    lo = [a(0, half), a(HEAD_DIM, HEAD_DIM + half), a(ROT_DIM, ROT_DIM + rest),
          a(HEAD_DIM + ROT_DIM, HEAD_DIM + ROT_DIM + rest)]
    hi = [a(half, ROT_DIM), a(HEAD_DIM + half, HEAD_DIM + ROT_DIM), a(ROT_DIM + rest, HEAD_DIM),
          a(HEAD_DIM + ROT_DIM + rest, 2 * HEAD_DIM)]
    return jnp.concatenate(lo + hi)


def _qk_first_head(lane):
    half = ROT_DIM // 2
    rest = (HEAD_DIM - ROT_DIM) // 2
    l = lane % HEAD_DIM
    return (l < half) | ((l >= ROT_DIM) & (l < ROT_DIM + rest))


def _attn_kernel(q_ref, k_ref, v_ref, bias_ref, o_ref, lse_ref, kres_ref, vres_ref, *, dil, tq, tk, half, length, nsub):
    i = pl.program_id(2)

    @pl.when(i == 0)
    def _():
        for r in range(dil):
            rows = pl.ds(r, length, stride=dil) if dil > 1 else pl.ds(0, length)
            kres_ref[r * length:(r + 1) * length, :] = k_ref[0, rows, :].astype(BF16)
            vres_ref[r * length:(r + 1) * length, :] = v_ref[0, rows, :].astype(BF16)

    lane = lax.broadcasted_iota(jnp.int32, (1, 128), 1)
    first_out = lane < HEAD_DIM
    first_qk = _qk_first_head(lane)

    def group(g, carry):
        rows, bias, q, kw, vw = [], [], [], [], []
        for j in range(ATTN_UNITS):
            u = g * ATTN_UNITS + j
            r = u % dil
            sub = u // dil
            blk = i * nsub + sub
            ks = pl.multiple_of(jnp.clip(blk * tq - half, 0, length - tk), 64)
            bias.append(bias_ref[(blk * tq - ks) // half])
            if dil > 1:
                rows.append(pl.ds(r + sub * (tq * dil), tq, stride=dil))
            else:
                rows.append(pl.ds(pl.multiple_of(sub * tq, tq), tq))
            q.append(q_ref[0, rows[j], :].astype(BF16))
            kbase = pl.multiple_of(r * length + ks, 64)
            kw.append(kres_ref[pl.ds(kbase, tk), :])
            vw.append(vres_ref[pl.ds(kbase, tk), :])
        chains = [(j, sel) for j in range(ATTN_UNITS) for sel in (first_qk, ~first_qk)]
        sc = [_dot_nt(jnp.where(sel, q[j], jnp.zeros_like(q[j])), kw[j]) for j, sel in chains]
        sc = [s + bias[j] for (j, _), s in zip(chains, sc)]
        mx = [jnp.max(s, axis=-1, keepdims=True) for s in sc]
        p = [jnp.exp(s - m) for s, m in zip(sc, mx)]
        den = [jnp.sum(t, axis=-1, keepdims=True) for t in p]
        o = [_dot(t.astype(BF16), vw[j]) / d for (j, _), t, d in zip(chains, p, den)]
        lse = [m + jnp.log(d) for m, d in zip(mx, den)]
        for j in range(ATTN_UNITS):
            o_ref[0, rows[j], :] = jnp.where(first_out, o[2 * j], o[2 * j + 1])
            lse_ref[0, rows[j], :] = jnp.where(first_out, lse[2 * j], lse[2 * j + 1])
        return carry

    lax.fori_loop(0, dil * nsub // ATTN_UNITS, group, 0)


def _attn_group(at, batch, seq, gi):
    window, dil = ATTN_GROUPS[gi]
    half = (window // 2) // dil
    length = seq // dil
    tq = 2 * half
    tk = tq + 2 * half
    nsub = max(1, 2 * ATTN_UNITS // dil)
    assert half == 64 and seq % dil == 0 and tk <= length and length % (tq * nsub) == 0
    assert (dil * nsub) % ATTN_UNITS == 0
    tb = tq * dil * nsub
    at3 = at.reshape(batch, seq, 3 * ATTN_WIDTH)
    nslab = ATTN_GW // 128
    nb = ATTN_WIDTH // 128
    off = (jnp.arange(3) * half)[:, None, None]
    dist = jnp.arange(tq)[None, :, None] + off - jnp.arange(tk)[None, None, :]
    bias = jnp.where(jnp.abs(dist) <= half, 0.0, -jnp.inf).astype(F32)
    o, lse = pl.pallas_call(
        functools.partial(_attn_kernel, dil=dil, tq=tq, tk=tk, half=half, length=length, nsub=nsub),
        grid=(batch, nslab, seq // tb),
        in_specs=[
            pl.BlockSpec((1, tb, 128), lambda b, s, i: (b, i, gi * nslab + s)),
            pl.BlockSpec((1, seq, 128), lambda b, s, i: (b, 0, nb + gi * nslab + s), pipeline_mode=pl.Buffered(1)),
            pl.BlockSpec((1, seq, 128), lambda b, s, i: (b, 0, 2 * nb + gi * nslab + s), pipeline_mode=pl.Buffered(1)),
            _resident((3, tq, tk)),
        ],
        out_specs=[
            pl.BlockSpec((1, tb, 128), lambda b, s, i: (b, i, s)),
            pl.BlockSpec((1, tb, 128), lambda b, s, i: (b, i, s)),
        ],
        out_shape=[jax.ShapeDtypeStruct((batch, seq, ATTN_GW), F32)] * 2,
        scratch_shapes=[pltpu.VMEM((seq, 128), BF16), pltpu.VMEM((seq, 128), BF16)],
        compiler_params=_cparams(("parallel", "parallel", "arbitrary")),
        name=f"attn{gi}",
    )(at3, at3, at3, bias)
    return o.reshape(batch * seq, ATTN_GW), lse.reshape(batch * seq, ATTN_GW)


def _dn_kernel(q_ref, k_ref, v_ref, ba_ref, alog_ref, dtb_ref, tri_ref, *refs, ct, reverse, lane0, add_prev):
    prev_ref = refs[0] if add_prev else None
    o_ref, st_ref, beta_ref, gc_ref, gct_ref, *slot_refs = refs[1:] if add_prev else refs
    _dn_body(q_ref, k_ref, v_ref, ba_ref, alog_ref, dtb_ref, tri_ref, prev_ref, o_ref, st_ref, beta_ref, gc_ref,
             gct_ref, slot_refs, ct=ct, reverse=reverse, lane0=lane0)


def _dn_body(q_ref, k_ref, v_ref, ba_ref, alog_ref, dtb_ref, tri_ref, prev_ref, o_ref,
             st_ref, beta_ref, gc_ref, gct_ref, slot_refs, *, ct, reverse, lane0):
    c = DN_CHUNK
    nchunk = ct // c
    heads = range(DN_HEADS)
    hsl = [slice(h * DN_HEAD_DIM, (h + 1) * DN_HEAD_DIM) for h in heads]
    nslot = 2 * DN_AHEAD
    u_refs, wq_refs, upd_refs = slot_refs[:nslot], slot_refs[nslot:2 * nslot], slot_refs[2 * nslot:]

    @pl.when(pl.program_id(1) == 0)
    def _():
        st_ref[...] = jnp.zeros_like(st_ref)

    ba = ba_ref[0]
    beta_all = jax.nn.sigmoid(ba)
    xg = ba + dtb_ref[...]
    softplus = jnp.maximum(xg, 0.0) + jnp.log1p(jnp.exp(-jnp.abs(xg)))
    g_all = -jnp.exp(alog_ref[...]) * softplus
    gc_all = jnp.concatenate([_dot_exact_lhs(tri_ref[...], g_all[n * c:(n + 1) * c]) for n in range(nchunk)], axis=0)
    for h in heads:
        lb = lane0 - BA_G0 + h
        beta_ref[:, hsl[h]] = jnp.broadcast_to(beta_all[:, lb:lb + 1], (ct, DN_HEAD_DIM))
        gc_ref[:, hsl[h]] = jnp.broadcast_to(gc_all[:, lane0 + h:lane0 + h + 1], (ct, DN_HEAD_DIM))
    for n in range(nchunk):
        gct_ref[n] = gc_all[n * c:(n + 1) * c, :].T

    row = lax.broadcasted_iota(jnp.int32, (c, c), 0)
    col = lax.broadcasted_iota(jnp.int32, (c, c), 1)
    incl = (col >= row) if reverse else (col <= row)
    strict = (col > row) if reverse else (col < row)
    upper = lax.broadcasted_iota(jnp.int32, (c, 2 * c), 1) >= c
    eye_hi = (lax.broadcasted_iota(jnp.int32, (c, 2 * c), 1) - c
              == lax.broadcasted_iota(jnp.int32, (c, 2 * c), 0)).astype(F32)

    def local(chunks):
        units = [(n, slot, h) for n, slot in chunks for h in heads]
        idx = range(len(units))
        rows = [slice(n * c, (n + 1) * c) for n, _, _ in units]
        last = [n * c + (0 if reverse else c - 1) for n, _, _ in units]
        hs = [hsl[h] for _, _, h in units]
        k16 = [k_ref[0, rows[i], hs[i]] for i in idx]
        q = [q_ref[0, rows[i], hs[i]].astype(F32) for i in idx]
        v = [v_ref[0, rows[i], hs[i]].astype(F32) for i in idx]
        beta = [beta_ref[rows[i], hs[i]] for i in idx]
        gcx = [gc_ref[rows[i], hs[i]] for i in idx]
        gtot = [gc_ref[last[i]:last[i] + 1, hs[i]] for i in idx]
        grow = [gct_ref[n, lane0 + h:lane0 + h + 1, :] for n, _, h in units]
        k = [t.astype(F32) for t in k16]
        kb = [k[i] * beta[i] for i in idx]
        kq = [_dot_nt(jnp.concatenate([kb[i], q[i]], axis=0).astype(BF16), k16[i]) for i in idx]
        decay = [jnp.where(incl, jnp.exp(jnp.where(incl, gcx[i][:, :c] - grow[i], 0.0)), 0.0) for i in idx]
        intra = [jnp.where(incl, kq[i][c:] * decay[i], 0.0) for i in idx]
        mk = [jnp.where(strict, -kq[i][:c] * decay[i], 0.0) for i in idx]
        z = [jnp.concatenate([mk[i], jnp.zeros_like(mk[i])], axis=1) + eye_hi for i in idx]
        for _ in range(int(math.log2(c))):
            y = [_dot(z[i][:, :c].astype(BF16), z[i].astype(BF16)) for i in idx]
            z = [y[i] + jnp.where(upper, z[i], 0.0) for i in idx]
        eg = [jnp.exp(gcx[i]) for i in idx]
        rhs = [jnp.concatenate([v[i] * beta[i], kb[i] * eg[i]], axis=1).astype(BF16) for i in idx]
        rhs = [jnp.concatenate([jnp.zeros_like(t), t], axis=0) for t in rhs]
        sol = [_dot(z[i].astype(BF16), rhs[i]) for i in idx]
        kdec_t = [(k[i].T * jnp.exp(gtot[i][:, :c] - grow[i])).astype(BF16) for i in idx]
        for i, (_, slot, h) in enumerate(units):
            u_refs[slot][h] = sol[i][:, :DN_HEAD_DIM]
            wq_refs[slot][h, :c] = sol[i][:, DN_HEAD_DIM:].astype(BF16)
            wq_refs[slot][h, c:] = (q[i] * eg[i]).astype(BF16)
            upd_refs[slot][h, :c] = intra[i].astype(BF16)
            upd_refs[slot][h, c:] = kdec_t[i]

    def recur(n, slot, state):
        rows = slice(n * c, (n + 1) * c)
        last = n * c + (0 if reverse else c - 1)
        etot = [jnp.exp(gc_ref[last:last + 1, hsl[h]]) for h in heads]
        wq = [_dot(wq_refs[slot][h], state[h].astype(BF16)) for h in heads]
        v_new = [(u_refs[slot][h] - wq[h][:c]).astype(BF16) for h in heads]
        upd = [_dot(upd_refs[slot][h], v_new[h]) for h in heads]
        for h in heads:
            o = wq[h][c:] + upd[h][:c]
            if prev_ref is not None:
                o = o + prev_ref[0, rows, hsl[h]].astype(F32)
            o_ref[0, rows, hsl[h]] = o.astype(o_ref.dtype)
        return [state[h] * etot[h] + upd[h][c:] for h in heads]

    order = list(range(nchunk))[::-1] if reverse else list(range(nchunk))
    groups = [order[i:i + DN_AHEAD] for i in range(0, nchunk, DN_AHEAD)]
    slots = lambda gi: [(n, (gi % 2) * DN_AHEAD + j) for j, n in enumerate(groups[gi])]
    state = [st_ref[h] for h in heads]
    local(slots(0))
    for gi in range(len(groups)):
        for n, slot in slots(gi):
            state = recur(n, slot, state)
        if gi + 1 < len(groups):
            local(slots(gi + 1))
    for h in heads:
        st_ref[h] = state[h]


def _dn_tri(reverse):
    r = jnp.arange(DN_CHUNK)
    tri = (r[None, :] >= r[:, None]) if reverse else (r[None, :] <= r[:, None])
    return tri.astype(BF16)


def _deltanet(qk, v, ba, alog_row, dtb_row, batch, seq, direction, prev=None, *, ct=512):
    reverse = direction == 1
    nstep = seq // ct
    w = DN_WIDTH
    c = DN_CHUNK
    qk3 = qk.reshape(batch, seq, 2 * w)
    v3 = v.reshape(batch, seq, w)
    ba3 = ba.reshape(batch, seq, BA_W)
    step = (lambda n: nstep - 1 - n) if reverse else (lambda n: n)
    out = pl.pallas_call(
        functools.partial(_dn_kernel, ct=ct, reverse=reverse, lane0=BA_G0 + direction * DN_HEADS,
                          add_prev=prev is not None),
        grid=(batch, nstep),
        in_specs=[
            pl.BlockSpec((1, ct, w), lambda b, n: (b, step(n), 0)),
            pl.BlockSpec((1, ct, w), lambda b, n: (b, step(n), 1)),
            pl.BlockSpec((1, ct, w), lambda b, n: (b, step(n), 0)),
            pl.BlockSpec((1, ct, BA_W), lambda b, n: (b, step(n), 0)),
            _resident((1, BA_W)),
            _resident((1, BA_W)),
            _resident((c, c)),
        ] + ([] if prev is None else [pl.BlockSpec((1, ct, w), lambda b, n: (b, step(n), 0))]),
        out_specs=pl.BlockSpec((1, ct, w), lambda b, n: (b, step(n), 0)),
        out_shape=jax.ShapeDtypeStruct((batch, seq, w), BF16),
        scratch_shapes=[
            pltpu.VMEM((DN_HEADS, DN_HEAD_DIM, DN_HEAD_DIM), F32),
            pltpu.VMEM((ct, w), F32),
            pltpu.VMEM((ct, w), F32),
            pltpu.VMEM((ct // c, BA_W, c), F32),
        ]
        + [pltpu.VMEM((DN_HEADS, c, DN_HEAD_DIM), F32)] * (2 * DN_AHEAD)
        + [pltpu.VMEM((DN_HEADS, 2 * c, DN_HEAD_DIM), BF16)] * (2 * DN_AHEAD)
        + [pltpu.VMEM((DN_HEADS, c + DN_HEAD_DIM, c), BF16)] * (2 * DN_AHEAD),
        compiler_params=_cparams(("parallel", "arbitrary")),
        name=f"deltanet{direction}",
    )(qk3, qk3, v3, ba3, alog_row, dtb_row, _dn_tri(reverse),
      *([] if prev is None else [prev.reshape(batch, seq, w)]))
    return out.reshape(batch * seq, w)


def _merge_kernel(x_ref, o1_ref, o2_ref, o3_ref, l1_ref, l2_ref, l3_ref, dn_ref, z_ref, sc_ref, gate_ref,
                  dng_ref, wa_ref, wd_ref, ws_ref, wo_ref, out_ref):
    l1, l2, l3 = l1_ref[...], l2_ref[...], l3_ref[...]
    mx = jnp.maximum(jnp.maximum(l1, l2), l3)
    e1, e2, e3 = jnp.exp(l1 - mx), jnp.exp(l2 - mx), jnp.exp(l3 - mx)
    attn = (o1_ref[...] * e1 + o2_ref[...] * e2 + o3_ref[...] * e3) / (e1 + e2 + e3)
    y_attn = _dot(attn.astype(BF16), wa_ref[...])
    dn = dn_ref[...].astype(F32)
    dng = dng_ref[...]
    parts = []
    for h in range(DN_HEADS):
        hs = slice(h * DN_HEAD_DIM, (h + 1) * DN_HEAD_DIM)
        oh = dn[:, hs]
        oh = oh * lax.rsqrt(jnp.mean(oh * oh, axis=-1, keepdims=True) + NORM_EPS) * dng
        parts.append((oh * z_ref[:, hs].astype(F32)).astype(BF16))
    y_dn = _dot(jnp.concatenate(parts, axis=1), wd_ref[...])
    y_sc = _dot(sc_ref[...], ws_ref[...])
    gates = gate_ref[...].astype(F32)
    d = D_MODEL
    merged = gates[:, :d] * y_attn + gates[:, d:2 * d] * y_dn + gates[:, 2 * d:] * y_sc
    out_ref[...] = x_ref[...] + _dot(merged.astype(BF16), wo_ref[...])


def _merge(x, attn_o, attn_l, dn, pj, sc, dn_norm, wa, wd, ws, wo, *, tm=512):
    m, d = x.shape
    row = lambda i: (i, 0)
    gw = N_BRANCH * d
    return pl.pallas_call(
        _merge_kernel,
        grid=(m // tm,),
        in_specs=[pl.BlockSpec((tm, d), row)]
        + [pl.BlockSpec((tm, ATTN_GW), row)] * 6
        + [pl.BlockSpec((tm, DN_WIDTH), row)]
        + [
            pl.BlockSpec((tm, DN_WIDTH), lambda i: (i, PJ_Z // DN_WIDTH)),
            pl.BlockSpec((tm, SC_WIDTH), row),
            pl.BlockSpec((tm, gw), lambda i: (i, PJ_GATE // gw)),
            _resident((1, DN_HEAD_DIM)),
            _resident((ATTN_GW, d)),
            _resident((DN_WIDTH, d)),
            _resident((SC_WIDTH, d)),
            _resident((d, d)),
        ],
        out_specs=pl.BlockSpec((tm, d), row),
        out_shape=jax.ShapeDtypeStruct((m, d), F32),
        compiler_params=_cparams(("parallel",)),
        name="merge",
    )(x, *attn_o, *attn_l, dn, pj, sc, pj, dn_norm, wa, wd, ws, wo)


def _prep_layer(l, p):
    w_in = p["w_in"][l]
    o_attn, o_dn = 0, 3 * ATTN_WIDTH
    o_dnv = o_dn + 2 * DN_WIDTH
    o_z = o_dn + 3 * DN_WIDTH
    o_beta = o_z + DN_WIDTH
    o_sc = o_beta + 4 * DN_HEADS
    o_gate = o_sc + 3 * SC_WIDTH
    bf = lambda t: t.astype(BF16)
    w_qk = w_in[:, o_attn:o_attn + 2 * ATTN_WIDTH].reshape(D_MODEL, -1, 128)[:, :, _qk_perm()]
    w_attn = bf(jnp.concatenate([w_qk.reshape(D_MODEL, -1), w_in[:, o_attn + 2 * ATTN_WIDTH:o_dn], w_in[:, o_sc:o_gate],
                                 w_in[:, o_dnv:o_z]], axis=1))
    w_proj = bf(jnp.concatenate([w_in[:, o_dn:o_dnv], w_in[:, o_gate:], w_in[:, o_z:o_beta]], axis=1))
    w_ba = jnp.pad(w_in[:, o_beta:o_sc], ((0, 0), (0, BA_W - 4 * DN_HEADS)))
    pad_row = lambda t: jnp.pad(t.reshape(1, 2 * DN_HEADS), ((0, 0), (BA_G0, BA_W - BA_G0 - 2 * DN_HEADS)))
    row = lambda t: t.reshape(1, -1)
    return dict(
        ffn1=(row(p["ffn1_norm"][l]), bf(p["ffn1_w_gate"][l]), bf(p["ffn1_w_up"][l]), bf(p["ffn1_w_down"][l])),
        ffn2=(row(p["ffn2_norm"][l]), bf(p["ffn2_w_gate"][l]), bf(p["ffn2_w_up"][l]), bf(p["ffn2_w_down"][l])),
        mix_norm=row(p["mix_norm"][l]),
        w_attn=w_attn,
        w_proj=w_proj,
        w_ba=bf(w_ba),
        dn_conv_qk=p["dn_conv"][l][:, :2 * DN_WIDTH],
        dn_conv_v=p["dn_conv"][l][:, 2 * DN_WIDTH:],
        alog=pad_row(p["dn_a_log"][l]),
        dtb=pad_row(p["dn_dt_bias"][l]),
        dn_norm=row(p["dn_norm"][l]),
        sc_conv=p["sc_conv"][l],
        gate_bias=row(p["gate_bias"][l]),
        wa=bf(p["w_attn_br"][l]),
        wd=bf(p["w_dn_br"][l]),
        ws=bf(p["w_sc_br"][l]),
        wo=bf(p["w_out"][l]),
    )


def _mixer(x, lp, batch, seq, tables):
    at, sc, dn_v = _attn_proj(x, lp["mix_norm"], lp["w_attn"], tables[0], tables[1], lp["sc_conv"], lp["dn_conv_v"], seq)
    pj, ba, dn_qk = _proj(x, lp["mix_norm"], lp["w_proj"], lp["w_ba"], lp["dn_conv_qk"], lp["gate_bias"], seq)
    attn = [_attn_group(at, batch, seq, gi) for gi in range(len(ATTN_GROUPS))]
    dn = _deltanet(dn_qk, dn_v, ba, lp["alog"], lp["dtb"], batch, seq, 0)
    dn = _deltanet(dn_qk, dn_v, ba, lp["alog"], lp["dtb"], batch, seq, 1, prev=dn)
    return _merge(x, [a[0] for a in attn], [a[1] for a in attn], dn, pj, sc,
                  lp["dn_norm"], lp["wa"], lp["wd"], lp["ws"], lp["wo"])


def _trunk(x, layers, final_g):
    batch, seq, d = x.shape
    tables = _rope_tables(seq)
    x = x.reshape(batch * seq, d)
    for l, lp in enumerate(layers):
        x = _ffn(x, *lp["ffn1"])
        x = _mixer(x, lp, batch, seq, tables)
        x = _ffn(x, *lp["ffn2"], final_g=final_g if l == len(layers) - 1 else None)
    return x.reshape(batch, seq, d)


def kernel(x_prompt, x_sample, ffn1_norm, ffn1_w_gate, ffn1_w_up, ffn1_w_down, mix_norm, w_in, dn_conv, dn_a_log, dn_dt_bias, dn_norm, sc_conv, gate_bias, w_attn_br, w_dn_br, w_sc_br, w_out, ffn2_norm, ffn2_w_gate, ffn2_w_up, ffn2_w_down, final_norm):
    p = dict(ffn1_norm=ffn1_norm, ffn1_w_gate=ffn1_w_gate, ffn1_w_up=ffn1_w_up, ffn1_w_down=ffn1_w_down,
             mix_norm=mix_norm, w_in=w_in, dn_conv=dn_conv, dn_a_log=dn_a_log, dn_dt_bias=dn_dt_bias,
             dn_norm=dn_norm, sc_conv=sc_conv, gate_bias=gate_bias, w_attn_br=w_attn_br, w_dn_br=w_dn_br,
             w_sc_br=w_sc_br, w_out=w_out, ffn2_norm=ffn2_norm, ffn2_w_gate=ffn2_w_gate, ffn2_w_up=ffn2_w_up,
             ffn2_w_down=ffn2_w_down)
    layers = [_prep_layer(l, p) for l in range(ffn1_norm.shape[0])]
    final_g = final_norm.reshape(1, -1)
    return _trunk(x_prompt, layers, final_g), _trunk(x_sample, layers, final_g)
```

```python
import functools
import math

import jax
import jax.numpy as jnp
from jax import lax
from jax.experimental import pallas as pl
from jax.experimental.pallas import tpu as pltpu

F32 = jnp.float32
BF16 = jnp.bfloat16

D_MODEL = 1024
HEAD_DIM = 64
ATTN_GROUPS = ((128, 1), (512, 4), (2048, 16))
ATTN_GROUP_HEADS = 4
ATTN_GW = ATTN_GROUP_HEADS * HEAD_DIM
ATTN_WIDTH = 3 * ATTN_GW
ROT_DIM = HEAD_DIM // 4
ROPE_THETA = 500000.0
DN_HEADS = 6
DN_HEAD_DIM = 128
DN_WIDTH = DN_HEADS * DN_HEAD_DIM
DN_CONV = 5
DN_CHUNK = 64
SC_WIDTH = 768
SC_CONV = 3
FFN_DIM = 2816
N_BRANCH = 3
NORM_EPS = 1e-6

MXU_N = 256
PJ_GATE = 0
PJ_Z = PJ_GATE + N_BRANCH * D_MODEL
PJ_W = PJ_Z + DN_WIDTH
BA_W = 128
BA_G0 = 2 * DN_HEADS

DN_AHEAD = 4
ATTN_UNITS = 4
HALO = 8

V7X_VMEM_BYTES = 64 * 1024 * 1024
VMEM_LIMIT = V7X_VMEM_BYTES * 7 // 8


def _cparams(sem):
    return pltpu.CompilerParams(dimension_semantics=sem, vmem_limit_bytes=VMEM_LIMIT)


def _resident(shape):
    return pl.BlockSpec(shape, lambda *_: (0,) * len(shape), pipeline_mode=pl.Buffered(1))


def _rms(x, g):
    return x * lax.rsqrt(jnp.mean(x * x, axis=-1, keepdims=True) + NORM_EPS) * g


def _dot(a, b):
    return jnp.dot(a, b, preferred_element_type=F32)


def _dot_nt(a, b):
    return lax.dot_general(a, b, (((1,), (1,)), ((), ())), preferred_element_type=F32)


def _dot_exact_lhs(sel, x):
    x1 = x.astype(BF16)
    r = x - x1.astype(F32)
    x2 = r.astype(BF16)
    x3 = (r - x2.astype(F32)).astype(BF16)
    return _dot(sel, x1) + _dot(sel, x2) + _dot(sel, x3)


def _ffn_kernel(x_ref, g_ref, wg_ref, wu_ref, wd_ref, fg_ref, o_ref, *, tf, final):
    x = x_ref[...]
    xn = _rms(x, g_ref[...]).astype(BF16)
    acc = jnp.zeros_like(x)
    for c in range(FFN_DIM // tf):
        sl = slice(c * tf, (c + 1) * tf)
        hg = _dot(xn, wg_ref[:, sl])
        hu = _dot(xn, wu_ref[:, sl])
        h = (hg * jax.nn.sigmoid(hg) * hu).astype(BF16)
        acc = acc + _dot(h, wd_ref[sl, :])
    y = x + 0.5 * acc
    if final:
        y = _rms(y, fg_ref[...])
    o_ref[...] = y


def _ffn(x, g, wg, wu, wd, final_g=None, *, tm=1024, tf=256):
    m, d = x.shape
    final = final_g is not None
    fg = final_g if final else g
    return pl.pallas_call(
        functools.partial(_ffn_kernel, tf=tf, final=final),
        grid=(m // tm,),
        in_specs=[
            pl.BlockSpec((tm, d), lambda i: (i, 0)),
            _resident((1, d)),
            _resident((d, FFN_DIM)),
            _resident((d, FFN_DIM)),
            _resident((FFN_DIM, d)),
            _resident((1, d)),
        ],
        out_specs=pl.BlockSpec((tm, d), lambda i: (i, 0)),
        out_shape=jax.ShapeDtypeStruct((m, d), F32),
        compiler_params=_cparams(("parallel",)),
        name="ffn",
    )(x, g, wg, wu, wd, fg)


def _tap(ext, off, t):
    n = ext.shape[0]
    return pltpu.roll(ext, (-off) % n, 0)[HALO:HALO + t] if off else ext[HALO:HALO + t]


def _seq_edges(i, t, seq):
    p0 = (i * t) % seq
    return p0 == 0, p0 + t == seq


def _conv_taps(ext, cw_ref, sl, ntap, t):
    acc = None
    for tap in range(ntap):
        term = _tap(ext, tap - ntap // 2, t) * cw_ref[tap:tap + 1, sl]
        acc = term if acc is None else acc + term
    return acc


def _halo_ext(old_ref, new_ref, sl, first, last):
    return jnp.concatenate([jnp.where(first, 0.0, old_ref[:HALO, sl]), old_ref[HALO:, sl],
                            jnp.where(last, 0.0, new_ref[:HALO, sl])], axis=0)


def _advance_tile(old_ref, new_ref, sl, t):
    old_ref[:HALO, sl] = old_ref[t:, sl]
    old_ref[HALO:, sl] = new_ref[:, sl]


def _dn_prep_slab(old_ref, new_ref, cw_ref, sl, cw_sl, first, last, t, norm_scale):
    y = _conv_taps(_halo_ext(old_ref, new_ref, sl, first, last), cw_ref, cw_sl, DN_CONV, t)
    y = y * jax.nn.sigmoid(y)
    if norm_scale is not None:
        y = y * (lax.rsqrt(jnp.sum(y * y, axis=-1, keepdims=True) + NORM_EPS) * norm_scale)
    _advance_tile(old_ref, new_ref, sl, t)
    return y.astype(BF16)


def _init_tiles(new_ref, old_ref):
    @pl.when(pl.program_id(0) == 0)
    def _():
        new_ref[...] = jnp.zeros_like(new_ref)
        old_ref[...] = jnp.zeros_like(old_ref)


def _attn_proj_kernel(x_ref, g_ref, w_ref, cos_ref, sin_ref, scw_ref, dvw_ref, at_ref, sc_ref, v_ref, new_ref, old_ref,
                      *, t, seq):
    aw, scw = 3 * ATTN_WIDTH, 3 * SC_WIDTH
    _init_tiles(new_ref, old_ref)
    first, last = _seq_edges(pl.program_id(0) - 1, t, seq)
    xn = _rms(x_ref[...], g_ref[...]).astype(BF16)

    def attn_chunk(c):
        y = _dot(xn, w_ref[:, c * MXU_N:(c + 1) * MXU_N])
        sect = c * MXU_N // ATTN_WIDTH
        for s in range(MXU_N // 128):
            ys = y[:, s * 128:(s + 1) * 128]
            if sect < 2:
                ys = ys * cos_ref[sect] + pltpu.roll(ys, HEAD_DIM, 1) * sin_ref[sect]
            at_ref[:, c * MXU_N + s * 128:c * MXU_N + (s + 1) * 128] = ys

    def raw_chunk(c):
        new_ref[:, c * MXU_N:(c + 1) * MXU_N] = _dot(xn, w_ref[:, aw + c * MXU_N:aw + (c + 1) * MXU_N])

    def sc_slab(s):
        lanes = [slice(part * SC_WIDTH + s * 128, part * SC_WIDTH + (s + 1) * 128) for part in range(3)]
        out = slice(s * 128, (s + 1) * 128)
        cx = _halo_ext(old_ref, new_ref, lanes[1], first, last) * _halo_ext(old_ref, new_ref, lanes[2], first, last)
        sc_ref[:, out] = (old_ref[HALO:, lanes[0]] * _conv_taps(cx, scw_ref, out, SC_CONV, t)).astype(BF16)
        for sl in lanes:
            _advance_tile(old_ref, new_ref, sl, t)

    def v_slab(s):
        out = slice(s * DN_HEAD_DIM, (s + 1) * DN_HEAD_DIM)
        sl = slice(scw + out.start, scw + out.stop)
        v_ref[:, out] = _dn_prep_slab(old_ref, new_ref, dvw_ref, sl, out, first, last, t, None)

    ngrp = SC_WIDTH // MXU_N
    per = MXU_N // 128
    for grp in range(ngrp):
        for part in range(3):
            raw_chunk(part * ngrp + grp)
        raw_chunk(3 * ngrp + grp)
        for j in range(aw // MXU_N // ngrp):
            attn_chunk(grp * (aw // MXU_N // ngrp) + j)
            if j < per:
                sc_slab(grp * per + j)
                v_slab(grp * per + j)


def _attn_proj(x, g, w, cos_t, sin_t, sc_conv, dn_conv_v, seq, *, tm=512):
    m, d = x.shape
    nt = m // tm
    nseq = seq // tm
    aw, raww = 3 * ATTN_WIDTH, 3 * SC_WIDTH + DN_WIDTH
    cur_tile = lambda i: (jnp.minimum(i, nt - 1), 0)
    prev_tile = lambda i: (jnp.maximum(i - 1, 0), 0)
    table = lambda i: (0, jnp.minimum(i, nt - 1) % nseq, 0)
    return pl.pallas_call(
        functools.partial(_attn_proj_kernel, t=tm, seq=seq),
        grid=(nt + 1,),
        in_specs=[
            pl.BlockSpec((tm, d), cur_tile),
            _resident((1, d)),
            _resident((d, aw + raww)),
            pl.BlockSpec((2, tm, 128), table),
            pl.BlockSpec((2, tm, 128), table),
            _resident((SC_CONV, SC_WIDTH)),
            _resident((DN_CONV, DN_WIDTH)),
        ],
        out_specs=[
            pl.BlockSpec((tm, aw), cur_tile),
            pl.BlockSpec((tm, SC_WIDTH), prev_tile),
            pl.BlockSpec((tm, DN_WIDTH), prev_tile),
        ],
        out_shape=[
            jax.ShapeDtypeStruct((m, aw), F32),
            jax.ShapeDtypeStruct((m, SC_WIDTH), BF16),
            jax.ShapeDtypeStruct((m, DN_WIDTH), BF16),
        ],
        scratch_shapes=[pltpu.VMEM((tm, raww), F32), pltpu.VMEM((HALO + tm, raww), F32)],
        compiler_params=_cparams(("arbitrary",)),
        name="attn_proj",
    )(x, g, w, cos_t, sin_t, sc_conv, dn_conv_v)


def _proj_kernel(x_ref, g_ref, w_ref, wba_ref, cw_ref, gb_ref, pj_ref, ba_ref, qk_ref, new_ref, old_ref, *, t, seq):
    qkw = 2 * DN_WIDTH
    _init_tiles(new_ref, old_ref)
    first, last = _seq_edges(pl.program_id(0) - 1, t, seq)
    xn = _rms(x_ref[...], g_ref[...]).astype(BF16)
    ba_ref[...] = _dot(xn, wba_ref[...])
    nslab = qkw // DN_HEAD_DIM
    npj = PJ_W // MXU_N
    for s in range(nslab):
        if s % (MXU_N // DN_HEAD_DIM) == 0:
            cs = slice(s * DN_HEAD_DIM, s * DN_HEAD_DIM + MXU_N)
            new_ref[:, cs] = _dot(xn, w_ref[:, cs])
        for c in range(npj * s // nslab, npj * (s + 1) // nslab):
            cs = slice(c * MXU_N, (c + 1) * MXU_N)
            y = _dot(xn, w_ref[:, qkw + c * MXU_N:qkw + (c + 1) * MXU_N])
            if cs.stop <= PJ_Z:
                y = jax.nn.sigmoid(y + gb_ref[:, cs])
            else:
                y = y * jax.nn.sigmoid(y)
            pj_ref[:, cs] = y.astype(BF16)
        sl = slice(s * DN_HEAD_DIM, (s + 1) * DN_HEAD_DIM)
        scale = DN_HEAD_DIM ** -0.5 if s < DN_HEADS else 1.0
        qk_ref[:, sl] = _dn_prep_slab(old_ref, new_ref, cw_ref, sl, sl, first, last, t, scale)


def _proj(x, g, w, w_ba, dn_conv_qk, gate_bias, seq, *, tm=512):
    m, d = x.shape
    nt = m // tm
    qkw = 2 * DN_WIDTH
    cur_tile = lambda i: (jnp.minimum(i, nt - 1), 0)
    return pl.pallas_call(
        functools.partial(_proj_kernel, t=tm, seq=seq),
        grid=(nt + 1,),
        in_specs=[
            pl.BlockSpec((tm, d), cur_tile),
            _resident((1, d)),
            _resident((d, qkw + PJ_W)),
            _resident((d, BA_W)),
            _resident((DN_CONV, qkw)),
            _resident((1, N_BRANCH * d)),
        ],
        out_specs=[
            pl.BlockSpec((tm, PJ_W), cur_tile),
            pl.BlockSpec((tm, BA_W), cur_tile),
            pl.BlockSpec((tm, qkw), lambda i: (jnp.maximum(i - 1, 0), 0)),
        ],
        out_shape=[
            jax.ShapeDtypeStruct((m, PJ_W), BF16),
            jax.ShapeDtypeStruct((m, BA_W), F32),
            jax.ShapeDtypeStruct((m, qkw), BF16),
        ],
        scratch_shapes=[pltpu.VMEM((tm, qkw), F32), pltpu.VMEM((HALO + tm, qkw), F32)],
        compiler_params=_cparams(("arbitrary",)),
        name="proj",
    )(x, g, w, w_ba, dn_conv_qk, gate_bias)


def _rope_tables(seq):
    half = ROT_DIM // 2
    inv_freq = ROPE_THETA ** (-2.0 * jnp.arange(half, dtype=F32) / ROT_DIM)
    ang = jnp.arange(seq, dtype=F32)[:, None] * inv_freq[None, :]
    cos, sin = jnp.cos(ang), jnp.sin(ang)
    ones = jnp.ones((seq, HEAD_DIM - ROT_DIM), F32)
    cos_h = jnp.concatenate([cos, cos, ones, cos, cos, ones], axis=1)
    sin_h = jnp.concatenate([-sin, -sin, 0.0 * ones, sin, sin, 0.0 * ones], axis=1)
    qs = HEAD_DIM ** -0.5
    return jnp.stack([cos_h * qs, cos_h]), jnp.stack([sin_h * qs, sin_h])


def _qk_perm():
    a = jnp.arange
    half = ROT_DIM // 2
    rest = (HEAD_DIM - ROT_DIM) // 2
    lo = [a(0, half), a(HEAD_DIM, HEAD_DIM + half), a(ROT_DIM, ROT_DIM + rest),
          a(HEAD_DIM + ROT_DIM, HEAD_DIM + ROT_DIM + rest)]
    hi = [a(half, ROT_DIM), a(HEAD_DIM + half, HEAD_DIM + ROT_DIM), a(ROT_DIM + rest, HEAD_DIM),
          a(HEAD_DIM + ROT_DIM + rest, 2 * HEAD_DIM)]
    return jnp.concatenate(lo + hi)


def _qk_first_head(lane):
    half = ROT_DIM // 2
    rest = (HEAD_DIM - ROT_DIM) // 2
    l = lane % HEAD_DIM
    return (l < half) | ((l >= ROT_DIM) & (l < ROT_DIM + rest))


def _attn_kernel(q_ref, k_ref, v_ref, bias_ref, o_ref, lse_ref, kres_ref, vres_ref, *, dil, tq, tk, half, length, nsub):
    i = pl.program_id(2)

    @pl.when(i == 0)
    def _():
        for r in range(dil):
            rows = pl.ds(r, length, stride=dil) if dil > 1 else pl.ds(0, length)
            kres_ref[r * length:(r + 1) * length, :] = k_ref[0, rows, :].astype(BF16)
            vres_ref[r * length:(r + 1) * length, :] = v_ref[0, rows, :].astype(BF16)

    lane = lax.broadcasted_iota(jnp.int32, (1, 128), 1)
    first_out = lane < HEAD_DIM
    first_qk = _qk_first_head(lane)

    def group(g, carry):
        rows, bias, q, kw, vw = [], [], [], [], []
        for j in range(ATTN_UNITS):
            u = g * ATTN_UNITS + j
            r = u % dil
            sub = u // dil
            blk = i * nsub + sub
            ks = pl.multiple_of(jnp.clip(blk * tq - half, 0, length - tk), 64)
            bias.append(bias_ref[(blk * tq - ks) // half])
            if dil > 1:
                rows.append(pl.ds(r + sub * (tq * dil), tq, stride=dil))
            else:
                rows.append(pl.ds(pl.multiple_of(sub * tq, tq), tq))
            q.append(q_ref[0, rows[j], :].astype(BF16))
            kbase = pl.multiple_of(r * length + ks, 64)
            kw.append(kres_ref[pl.ds(kbase, tk), :])
            vw.append(vres_ref[pl.ds(kbase, tk), :])
        chains = [(j, sel) for j in range(ATTN_UNITS) for sel in (first_qk, ~first_qk)]
        sc = [_dot_nt(jnp.where(sel, q[j], jnp.zeros_like(q[j])), kw[j]) for j, sel in chains]
        sc = [s + bias[j] for (j, _), s in zip(chains, sc)]
        mx = [jnp.max(s, axis=-1, keepdims=True) for s in sc]
        p = [jnp.exp(s - m) for s, m in zip(sc, mx)]
        den = [jnp.sum(t, axis=-1, keepdims=True) for t in p]
        o = [_dot(t.astype(BF16), vw[j]) / d for (j, _), t, d in zip(chains, p, den)]
        lse = [m + jnp.log(d) for m, d in zip(mx, den)]
        for j in range(ATTN_UNITS):
            o_ref[0, rows[j], :] = jnp.where(first_out, o[2 * j], o[2 * j + 1])
            lse_ref[0, rows[j], :] = jnp.where(first_out, lse[2 * j], lse[2 * j + 1])
        return carry

    lax.fori_loop(0, dil * nsub // ATTN_UNITS, group, 0)


def _attn_group(at, batch, seq, gi):
    window, dil = ATTN_GROUPS[gi]
    half = (window // 2) // dil
    length = seq // dil
    tq = 2 * half
    tk = tq + 2 * half
    nsub = max(1, 2 * ATTN_UNITS // dil)
    assert half == 64 and seq % dil == 0 and tk <= length and length % (tq * nsub) == 0
    assert (dil * nsub) % ATTN_UNITS == 0
    tb = tq * dil * nsub
    at3 = at.reshape(batch, seq, 3 * ATTN_WIDTH)
    nslab = ATTN_GW // 128
    nb = ATTN_WIDTH // 128
    off = (jnp.arange(3) * half)[:, None, None]
    dist = jnp.arange(tq)[None, :, None] + off - jnp.arange(tk)[None, None, :]
    bias = jnp.where(jnp.abs(dist) <= half, 0.0, -jnp.inf).astype(F32)
    o, lse = pl.pallas_call(
        functools.partial(_attn_kernel, dil=dil, tq=tq, tk=tk, half=half, length=length, nsub=nsub),
        grid=(batch, nslab, seq // tb),
        in_specs=[
            pl.BlockSpec((1, tb, 128), lambda b, s, i: (b, i, gi * nslab + s)),
            pl.BlockSpec((1, seq, 128), lambda b, s, i: (b, 0, nb + gi * nslab + s), pipeline_mode=pl.Buffered(1)),
            pl.BlockSpec((1, seq, 128), lambda b, s, i: (b, 0, 2 * nb + gi * nslab + s), pipeline_mode=pl.Buffered(1)),
            _resident((3, tq, tk)),
        ],
        out_specs=[
            pl.BlockSpec((1, tb, 128), lambda b, s, i: (b, i, s)),
            pl.BlockSpec((1, tb, 128), lambda b, s, i: (b, i, s)),
        ],
        out_shape=[jax.ShapeDtypeStruct((batch, seq, ATTN_GW), F32)] * 2,
        scratch_shapes=[pltpu.VMEM((seq, 128), BF16), pltpu.VMEM((seq, 128), BF16)],
        compiler_params=_cparams(("parallel", "parallel", "arbitrary")),
        name=f"attn{gi}",
    )(at3, at3, at3, bias)
    return o.reshape(batch * seq, ATTN_GW), lse.reshape(batch * seq, ATTN_GW)


def _dn_kernel(q_ref, k_ref, v_ref, ba_ref, alog_ref, dtb_ref, tri_ref, *refs, ct, reverse, lane0, add_prev):
    prev_ref = refs[0] if add_prev else None
    o_ref, st_ref, beta_ref, gc_ref, gct_ref, *slot_refs = refs[1:] if add_prev else refs
    _dn_body(q_ref, k_ref, v_ref, ba_ref, alog_ref, dtb_ref, tri_ref, prev_ref, o_ref, st_ref, beta_ref, gc_ref,
             gct_ref, slot_refs, ct=ct, reverse=reverse, lane0=lane0)


def _dn_body(q_ref, k_ref, v_ref, ba_ref, alog_ref, dtb_ref, tri_ref, prev_ref, o_ref,
             st_ref, beta_ref, gc_ref, gct_ref, slot_refs, *, ct, reverse, lane0):
    c = DN_CHUNK
    nchunk = ct // c
    heads = range(DN_HEADS)
    hsl = [slice(h * DN_HEAD_DIM, (h + 1) * DN_HEAD_DIM) for h in heads]
    nslot = 2 * DN_AHEAD
    u_refs, wq_refs, upd_refs = slot_refs[:nslot], slot_refs[nslot:2 * nslot], slot_refs[2 * nslot:]

    @pl.when(pl.program_id(1) == 0)
    def _():
        st_ref[...] = jnp.zeros_like(st_ref)

    ba = ba_ref[0]
    beta_all = jax.nn.sigmoid(ba)
    xg = ba + dtb_ref[...]
    softplus = jnp.maximum(xg, 0.0) + jnp.log1p(jnp.exp(-jnp.abs(xg)))
    g_all = -jnp.exp(alog_ref[...]) * softplus
    gc_all = jnp.concatenate([_dot_exact_lhs(tri_ref[...], g_all[n * c:(n + 1) * c]) for n in range(nchunk)], axis=0)
    for h in heads:
        lb = lane0 - BA_G0 + h
        beta_ref[:, hsl[h]] = jnp.broadcast_to(beta_all[:, lb:lb + 1], (ct, DN_HEAD_DIM))
        gc_ref[:, hsl[h]] = jnp.broadcast_to(gc_all[:, lane0 + h:lane0 + h + 1], (ct, DN_HEAD_DIM))
    for n in range(nchunk):
        gct_ref[n] = gc_all[n * c:(n + 1) * c, :].T

    row = lax.broadcasted_iota(jnp.int32, (c, c), 0)
    col = lax.broadcasted_iota(jnp.int32, (c, c), 1)
    incl = (col >= row) if reverse else (col <= row)
    strict = (col > row) if reverse else (col < row)
    upper = lax.broadcasted_iota(jnp.int32, (c, 2 * c), 1) >= c
    eye_hi = (lax.broadcasted_iota(jnp.int32, (c, 2 * c), 1) - c
              == lax.broadcasted_iota(jnp.int32, (c, 2 * c), 0)).astype(F32)

    def local(chunks):
        units = [(n, slot, h) for n, slot in chunks for h in heads]
        idx = range(len(units))
        rows = [slice(n * c, (n + 1) * c) for n, _, _ in units]
        last = [n * c + (0 if reverse else c - 1) for n, _, _ in units]
        hs = [hsl[h] for _, _, h in units]
        load = lambda ref: [ref[0, rows[i], hs[i]] for i in idx]
        k16 = load(k_ref)
        gcx = [gc_ref[rows[i], hs[i]] for i in idx]
        gtot = [gc_ref[last[i]:last[i] + 1, hs[i]] for i in idx]
        grow = [gct_ref[n, lane0 + h:lane0 + h + 1, :] for n, _, h in units]
        k = [t.astype(F32) for t in k16]
        kb = [(k[i] * beta_ref[rows[i], hs[i]]).astype(BF16) for i in idx]
        kq = [_dot_nt(jnp.concatenate([kb[i], q16], axis=0), k16[i]) for i, q16 in zip(idx, load(q_ref))]
        decay = [jnp.where(incl, jnp.exp(jnp.where(incl, gcx[i][:, :c] - grow[i], 0.0)), 0.0) for i in idx]
        for i, (_, slot, h) in enumerate(units):
            upd_refs[slot][h, :c] = jnp.where(incl, kq[i][c:] * decay[i], 0.0).astype(BF16)
            upd_refs[slot][h, c:] = (k[i].T * jnp.exp(gtot[i][:, :c] - grow[i])).astype(BF16)
        mk = [jnp.where(strict, -kq[i][:c] * decay[i], 0.0) for i in idx]
        z = [jnp.concatenate([mk[i], jnp.zeros_like(mk[i])], axis=1) + eye_hi for i in idx]
        for _ in range(int(math.log2(c))):
            zb = [t.astype(BF16) for t in z]
            z = [_dot(zb[i][:, :c], zb[i]) + jnp.where(upper, z[i], 0.0) for i in idx]
        beta = [beta_ref[rows[i], hs[i]] for i in idx]
        eg = [jnp.exp(gc_ref[rows[i], hs[i]]) for i in idx]
        kbe = [t.astype(F32) * beta[i] * eg[i] for i, t in zip(idx, load(k_ref))]
        rhs = [jnp.concatenate([t.astype(F32) * beta[i], kbe[i]], axis=1).astype(BF16) for i, t in zip(idx, load(v_ref))]
        rhs = [jnp.concatenate([jnp.zeros_like(t), t], axis=0) for t in rhs]
        sol = [_dot(z[i].astype(BF16), rhs[i]) for i in idx]
        for i, ((_, slot, h), q16) in enumerate(zip(units, load(q_ref))):
            u_refs[slot][h] = sol[i][:, :DN_HEAD_DIM]
            wq_refs[slot][h, :c] = sol[i][:, DN_HEAD_DIM:].astype(BF16)
            wq_refs[slot][h, c:] = (q16.astype(F32) * eg[i]).astype(BF16)

    def recur(n, slot, state):
        rows = slice(n * c, (n + 1) * c)
        last = n * c + (0 if reverse else c - 1)
        etot = [jnp.exp(gc_ref[last:last + 1, hsl[h]]) for h in heads]
        wq = [_dot(wq_refs[slot][h], state[h].astype(BF16)) for h in heads]
        v_new = [(u_refs[slot][h] - wq[h][:c]).astype(BF16) for h in heads]
        upd = [_dot(upd_refs[slot][h], v_new[h]) for h in heads]
        for h in heads:
            o = wq[h][c:] + upd[h][:c]
            if prev_ref is not None:
                o = o + prev_ref[0, rows, hsl[h]].astype(F32)
            o_ref[0, rows, hsl[h]] = o.astype(o_ref.dtype)
        return [state[h] * etot[h] + upd[h][c:] for h in heads]

    order = list(range(nchunk))[::-1] if reverse else list(range(nchunk))
    groups = [order[i:i + DN_AHEAD] for i in range(0, nchunk, DN_AHEAD)]
    slots = lambda gi: [(n, (gi % 2) * DN_AHEAD + j) for j, n in enumerate(groups[gi])]
    state = [st_ref[h] for h in heads]
    local(slots(0))
    for gi in range(len(groups)):
        for n, slot in slots(gi):
            state = recur(n, slot, state)
        if gi + 1 < len(groups):
            local(slots(gi + 1))
    for h in heads:
        st_ref[h] = state[h]


def _dn_tri(reverse):
    r = jnp.arange(DN_CHUNK)
    tri = (r[None, :] >= r[:, None]) if reverse else (r[None, :] <= r[:, None])
    return tri.astype(BF16)


def _deltanet(qk, v, ba, alog_row, dtb_row, batch, seq, direction, prev=None, *, ct=512):
    reverse = direction == 1
    nstep = seq // ct
    w = DN_WIDTH
    c = DN_CHUNK
    qk3 = qk.reshape(batch, seq, 2 * w)
    v3 = v.reshape(batch, seq, w)
    ba3 = ba.reshape(batch, seq, BA_W)
    step = (lambda n: nstep - 1 - n) if reverse else (lambda n: n)
    out = pl.pallas_call(
        functools.partial(_dn_kernel, ct=ct, reverse=reverse, lane0=BA_G0 + direction * DN_HEADS,
                          add_prev=prev is not None),
        grid=(batch, nstep),
        in_specs=[
            pl.BlockSpec((1, ct, w), lambda b, n: (b, step(n), 0)),
            pl.BlockSpec((1, ct, w), lambda b, n: (b, step(n), 1)),
            pl.BlockSpec((1, ct, w), lambda b, n: (b, step(n), 0)),
            pl.BlockSpec((1, ct, BA_W), lambda b, n: (b, step(n), 0)),
            _resident((1, BA_W)),
            _resident((1, BA_W)),
            _resident((c, c)),
        ] + ([] if prev is None else [pl.BlockSpec((1, ct, w), lambda b, n: (b, step(n), 0))]),
        out_specs=pl.BlockSpec((1, ct, w), lambda b, n: (b, step(n), 0)),
        out_shape=jax.ShapeDtypeStruct((batch, seq, w), BF16),
        scratch_shapes=[
            pltpu.VMEM((DN_HEADS, DN_HEAD_DIM, DN_HEAD_DIM), F32),
            pltpu.VMEM((ct, w), F32),
            pltpu.VMEM((ct, w), F32),
            pltpu.VMEM((ct // c, BA_W, c), F32),
        ]
        + [pltpu.VMEM((DN_HEADS, c, DN_HEAD_DIM), F32)] * (2 * DN_AHEAD)
        + [pltpu.VMEM((DN_HEADS, 2 * c, DN_HEAD_DIM), BF16)] * (2 * DN_AHEAD)
        + [pltpu.VMEM((DN_HEADS, c + DN_HEAD_DIM, c), BF16)] * (2 * DN_AHEAD),
        compiler_params=_cparams(("parallel", "arbitrary")),
        name=f"deltanet{direction}",
    )(qk3, qk3, v3, ba3, alog_row, dtb_row, _dn_tri(reverse),
      *([] if prev is None else [prev.reshape(batch, seq, w)]))
    return out.reshape(batch * seq, w)


def _merge_kernel(x_ref, o1_ref, o2_ref, o3_ref, l1_ref, l2_ref, l3_ref, dn_ref, z_ref, sc_ref, gate_ref,
                  dng_ref, wa_ref, wd_ref, ws_ref, wo_ref, out_ref):
    l1, l2, l3 = l1_ref[...], l2_ref[...], l3_ref[...]
    mx = jnp.maximum(jnp.maximum(l1, l2), l3)
    e1, e2, e3 = jnp.exp(l1 - mx), jnp.exp(l2 - mx), jnp.exp(l3 - mx)
    attn = (o1_ref[...] * e1 + o2_ref[...] * e2 + o3_ref[...] * e3) / (e1 + e2 + e3)
    y_attn = _dot(attn.astype(BF16), wa_ref[...])
    dn = dn_ref[...].astype(F32)
    dng = dng_ref[...]
    parts = []
    for h in range(DN_HEADS):
        hs = slice(h * DN_HEAD_DIM, (h + 1) * DN_HEAD_DIM)
        oh = dn[:, hs]
        oh = oh * lax.rsqrt(jnp.mean(oh * oh, axis=-1, keepdims=True) + NORM_EPS) * dng
        parts.append((oh * z_ref[:, hs].astype(F32)).astype(BF16))
    y_dn = _dot(jnp.concatenate(parts, axis=1), wd_ref[...])
    y_sc = _dot(sc_ref[...], ws_ref[...])
    gates = gate_ref[...].astype(F32)
    d = D_MODEL
    merged = gates[:, :d] * y_attn + gates[:, d:2 * d] * y_dn + gates[:, 2 * d:] * y_sc
    out_ref[...] = x_ref[...] + _dot(merged.astype(BF16), wo_ref[...])


def _merge(x, attn_o, attn_l, dn, pj, sc, dn_norm, wa, wd, ws, wo, *, tm=512):
    m, d = x.shape
    row = lambda i: (i, 0)
    gw = N_BRANCH * d
    return pl.pallas_call(
        _merge_kernel,
        grid=(m // tm,),
        in_specs=[pl.BlockSpec((tm, d), row)]
        + [pl.BlockSpec((tm, ATTN_GW), row)] * 6
        + [pl.BlockSpec((tm, DN_WIDTH), row)]
        + [
            pl.BlockSpec((tm, DN_WIDTH), lambda i: (i, PJ_Z // DN_WIDTH)),
            pl.BlockSpec((tm, SC_WIDTH), row),
            pl.BlockSpec((tm, gw), lambda i: (i, PJ_GATE // gw)),
            _resident((1, DN_HEAD_DIM)),
            _resident((ATTN_GW, d)),
            _resident((DN_WIDTH, d)),
            _resident((SC_WIDTH, d)),
            _resident((d, d)),
        ],
        out_specs=pl.BlockSpec((tm, d), row),
        out_shape=jax.ShapeDtypeStruct((m, d), F32),
        compiler_params=_cparams(("parallel",)),
        name="merge",
    )(x, *attn_o, *attn_l, dn, pj, sc, pj, dn_norm, wa, wd, ws, wo)


def _prep_layer(l, p):
    w_in = p["w_in"][l]
    o_attn, o_dn = 0, 3 * ATTN_WIDTH
    o_dnv = o_dn + 2 * DN_WIDTH
    o_z = o_dn + 3 * DN_WIDTH
    o_beta = o_z + DN_WIDTH
    o_sc = o_beta + 4 * DN_HEADS
    o_gate = o_sc + 3 * SC_WIDTH
    bf = lambda t: t.astype(BF16)
    w_qk = w_in[:, o_attn:o_attn + 2 * ATTN_WIDTH].reshape(D_MODEL, -1, 128)[:, :, _qk_perm()]
    w_attn = bf(jnp.concatenate([w_qk.reshape(D_MODEL, -1), w_in[:, o_attn + 2 * ATTN_WIDTH:o_dn], w_in[:, o_sc:o_gate],
                                 w_in[:, o_dnv:o_z]], axis=1))
    w_proj = bf(jnp.concatenate([w_in[:, o_dn:o_dnv], w_in[:, o_gate:], w_in[:, o_z:o_beta]], axis=1))
    w_ba = jnp.pad(w_in[:, o_beta:o_sc], ((0, 0), (0, BA_W - 4 * DN_HEADS)))
    pad_row = lambda t: jnp.pad(t.reshape(1, 2 * DN_HEADS), ((0, 0), (BA_G0, BA_W - BA_G0 - 2 * DN_HEADS)))
    row = lambda t: t.reshape(1, -1)
    return dict(
        ffn1=(row(p["ffn1_norm"][l]), bf(p["ffn1_w_gate"][l]), bf(p["ffn1_w_up"][l]), bf(p["ffn1_w_down"][l])),
        ffn2=(row(p["ffn2_norm"][l]), bf(p["ffn2_w_gate"][l]), bf(p["ffn2_w_up"][l]), bf(p["ffn2_w_down"][l])),
        mix_norm=row(p["mix_norm"][l]),
        w_attn=w_attn,
        w_proj=w_proj,
        w_ba=bf(w_ba),
        dn_conv_qk=p["dn_conv"][l][:, :2 * DN_WIDTH],
        dn_conv_v=p["dn_conv"][l][:, 2 * DN_WIDTH:],
        alog=pad_row(p["dn_a_log"][l]),
        dtb=pad_row(p["dn_dt_bias"][l]),
        dn_norm=row(p["dn_norm"][l]),
        sc_conv=p["sc_conv"][l],
        gate_bias=row(p["gate_bias"][l]),
        wa=bf(p["w_attn_br"][l]),
        wd=bf(p["w_dn_br"][l]),
        ws=bf(p["w_sc_br"][l]),
        wo=bf(p["w_out"][l]),
    )


def _mixer(x, lp, batch, seq, tables):
    at, sc, dn_v = _attn_proj(x, lp["mix_norm"], lp["w_attn"], tables[0], tables[1], lp["sc_conv"], lp["dn_conv_v"], seq)
    pj, ba, dn_qk = _proj(x, lp["mix_norm"], lp["w_proj"], lp["w_ba"], lp["dn_conv_qk"], lp["gate_bias"], seq)
    attn = [_attn_group(at, batch, seq, gi) for gi in range(len(ATTN_GROUPS))]
    dn = _deltanet(dn_qk, dn_v, ba, lp["alog"], lp["dtb"], batch, seq, 0)
    dn = _deltanet(dn_qk, dn_v, ba, lp["alog"], lp["dtb"], batch, seq, 1, prev=dn)
    return _merge(x, [a[0] for a in attn], [a[1] for a in attn], dn, pj, sc,
                  lp["dn_norm"], lp["wa"], lp["wd"], lp["ws"], lp["wo"])


def _trunk(x, layers, final_g):
    batch, seq, d = x.shape
    tables = _rope_tables(seq)
    x = x.reshape(batch * seq, d)
    for l, lp in enumerate(layers):
        x = _ffn(x, *lp["ffn1"])
        x = _mixer(x, lp, batch, seq, tables)
        x = _ffn(x, *lp["ffn2"], final_g=final_g if l == len(layers) - 1 else None)
    return x.reshape(batch, seq, d)


def kernel(x_prompt, x_sample, ffn1_norm, ffn1_w_gate, ffn1_w_up, ffn1_w_down, mix_norm, w_in, dn_conv, dn_a_log, dn_dt_bias, dn_norm, sc_conv, gate_bias, w_attn_br, w_dn_br, w_sc_br, w_out, ffn2_norm, ffn2_w_gate, ffn2_w_up, ffn2_w_down, final_norm):
    p = dict(ffn1_norm=ffn1_norm, ffn1_w_gate=ffn1_w_gate, ffn1_w_up=ffn1_w_up, ffn1_w_down=ffn1_w_down,
             mix_norm=mix_norm, w_in=w_in, dn_conv=dn_conv, dn_a_log=dn_a_log, dn_dt_bias=dn_dt_bias,
             dn_norm=dn_norm, sc_conv=sc_conv, gate_bias=gate_bias, w_attn_br=w_attn_br, w_dn_br=w_dn_br,
             w_sc_br=w_sc_br, w_out=w_out, ffn2_norm=ffn2_norm, ffn2_w_gate=ffn2_w_gate, ffn2_w_up=ffn2_w_up,
             ffn2_w_down=ffn2_w_down)
    layers = [_prep_layer(l, p) for l in range(ffn1_norm.shape[0])]
    final_g = final_norm.reshape(1, -1)
    return _trunk(x_prompt, layers, final_g), _trunk(x_sample, layers, final_g)
```

```python
import functools
import math

import jax
import jax.numpy as jnp
from jax import lax
from jax.experimental import pallas as pl
from jax.experimental.pallas import tpu as pltpu

F32 = jnp.float32
BF16 = jnp.bfloat16

D_MODEL = 1024
HEAD_DIM = 64
ATTN_GROUPS = ((128, 1), (512, 4), (2048, 16))
ATTN_GROUP_HEADS = 4
ATTN_GW = ATTN_GROUP_HEADS * HEAD_DIM
ATTN_WIDTH = 3 * ATTN_GW
ROT_DIM = HEAD_DIM // 4
ROPE_THETA = 500000.0
DN_HEADS = 6
DN_HEAD_DIM = 128
DN_WIDTH = DN_HEADS * DN_HEAD_DIM
DN_CONV = 5
DN_CHUNK = 64
SC_WIDTH = 768
SC_CONV = 3
FFN_DIM = 2816
N_BRANCH = 3
NORM_EPS = 1e-6

MXU_N = 256
PJ_GATE = 0
PJ_Z = PJ_GATE + N_BRANCH * D_MODEL
PJ_W = PJ_Z + DN_WIDTH
BA_W = 128
BA_G0 = 2 * DN_HEADS

DN_AHEAD = 4
ATTN_UNITS = 4
HALO = 8

V7X_VMEM_BYTES = 64 * 1024 * 1024
VMEM_LIMIT = V7X_VMEM_BYTES * 7 // 8


def _cparams(sem):
    return pltpu.CompilerParams(dimension_semantics=sem, vmem_limit_bytes=VMEM_LIMIT)


def _resident(shape):
    return pl.BlockSpec(shape, lambda *_: (0,) * len(shape), pipeline_mode=pl.Buffered(1))


def _rms(x, g):
    return x * lax.rsqrt(jnp.mean(x * x, axis=-1, keepdims=True) + NORM_EPS) * g


def _dot(a, b):
    return jnp.dot(a, b, preferred_element_type=F32)


def _dot_nt(a, b):
    return lax.dot_general(a, b, (((1,), (1,)), ((), ())), preferred_element_type=F32)


def _dot_exact_lhs(sel, x):
    x1 = x.astype(BF16)
    r = x - x1.astype(F32)
    x2 = r.astype(BF16)
    x3 = (r - x2.astype(F32)).astype(BF16)
    return _dot(sel, x1) + _dot(sel, x2) + _dot(sel, x3)


def _ffn_kernel(x_ref, g_ref, wg_ref, wu_ref, wd_ref, fg_ref, o_ref, *, tf, final):
    x = x_ref[...]
    xn = _rms(x, g_ref[...]).astype(BF16)
    acc = jnp.zeros_like(x)
    for c in range(FFN_DIM // tf):
        sl = slice(c * tf, (c + 1) * tf)
        hg = _dot(xn, wg_ref[:, sl])
        hu = _dot(xn, wu_ref[:, sl])
        h = (hg * jax.nn.sigmoid(hg) * hu).astype(BF16)
        acc = acc + _dot(h, wd_ref[sl, :])
    y = x + 0.5 * acc
    if final:
        y = _rms(y, fg_ref[...])
    o_ref[...] = y


def _ffn(x, g, wg, wu, wd, final_g=None, *, tm=1024, tf=256):
    m, d = x.shape
    final = final_g is not None
    fg = final_g if final else g
    return pl.pallas_call(
        functools.partial(_ffn_kernel, tf=tf, final=final),
        grid=(m // tm,),
        in_specs=[
            pl.BlockSpec((tm, d), lambda i: (i, 0)),
            _resident((1, d)),
            _resident((d, FFN_DIM)),
            _resident((d, FFN_DIM)),
            _resident((FFN_DIM, d)),
            _resident((1, d)),
        ],
        out_specs=pl.BlockSpec((tm, d), lambda i: (i, 0)),
        out_shape=jax.ShapeDtypeStruct((m, d), F32),
        compiler_params=_cparams(("parallel",)),
        name="ffn",
    )(x, g, wg, wu, wd, fg)


def _tap(ext, off, t):
    n = ext.shape[0]
    return pltpu.roll(ext, (-off) % n, 0)[HALO:HALO + t] if off else ext[HALO:HALO + t]


def _seq_edges(i, t, seq):
    p0 = (i * t) % seq
    return p0 == 0, p0 + t == seq


def _conv_taps(ext, cw_ref, sl, ntap, t):
    acc = None
    for tap in range(ntap):
        term = _tap(ext, tap - ntap // 2, t) * cw_ref[tap:tap + 1, sl]
        acc = term if acc is None else acc + term
    return acc


def _halo_ext(old_ref, new_ref, sl, first, last):
    return jnp.concatenate([jnp.where(first, 0.0, old_ref[:HALO, sl]), old_ref[HALO:, sl],
                            jnp.where(last, 0.0, new_ref[:HALO, sl])], axis=0)


def _advance_tile(old_ref, new_ref, sl, t):
    old_ref[:HALO, sl] = old_ref[t:, sl]
    old_ref[HALO:, sl] = new_ref[:, sl]


def _dn_prep_slab(old_ref, new_ref, cw_ref, sl, cw_sl, first, last, t, norm_scale):
    y = _conv_taps(_halo_ext(old_ref, new_ref, sl, first, last), cw_ref, cw_sl, DN_CONV, t)
    y = y * jax.nn.sigmoid(y)
    if norm_scale is not None:
        y = y * (lax.rsqrt(jnp.sum(y * y, axis=-1, keepdims=True) + NORM_EPS) * norm_scale)
    _advance_tile(old_ref, new_ref, sl, t)
    return y.astype(BF16)


def _init_tiles(new_ref, old_ref):
    @pl.when(pl.program_id(0) == 0)
    def _():
        new_ref[...] = jnp.zeros_like(new_ref)
        old_ref[...] = jnp.zeros_like(old_ref)


def _attn_proj_kernel(x_ref, g_ref, w_ref, cos_ref, sin_ref, scw_ref, dvw_ref, at_ref, sc_ref, v_ref, new_ref, old_ref,
                      *, t, seq):
    aw, scw = 3 * ATTN_WIDTH, 3 * SC_WIDTH
    _init_tiles(new_ref, old_ref)
    first, last = _seq_edges(pl.program_id(0) - 1, t, seq)
    xn = _rms(x_ref[...], g_ref[...]).astype(BF16)

    def attn_chunk(c):
        y = _dot(xn, w_ref[:, c * MXU_N:(c + 1) * MXU_N])
        sect = c * MXU_N // ATTN_WIDTH
        for s in range(MXU_N // 128):
            ys = y[:, s * 128:(s + 1) * 128]
            if sect < 2:
                ys = ys * cos_ref[sect] + pltpu.roll(ys, HEAD_DIM, 1) * sin_ref[sect]
            at_ref[:, c * MXU_N + s * 128:c * MXU_N + (s + 1) * 128] = ys

    def raw_chunk(c):
        new_ref[:, c * MXU_N:(c + 1) * MXU_N] = _dot(xn, w_ref[:, aw + c * MXU_N:aw + (c + 1) * MXU_N])

    def sc_slab(s):
        lanes = [slice(part * SC_WIDTH + s * 128, part * SC_WIDTH + (s + 1) * 128) for part in range(3)]
        out = slice(s * 128, (s + 1) * 128)
        cx = _halo_ext(old_ref, new_ref, lanes[1], first, last) * _halo_ext(old_ref, new_ref, lanes[2], first, last)
        sc_ref[:, out] = (old_ref[HALO:, lanes[0]] * _conv_taps(cx, scw_ref, out, SC_CONV, t)).astype(BF16)
        for sl in lanes:
            _advance_tile(old_ref, new_ref, sl, t)

    def v_slab(s):
        out = slice(s * DN_HEAD_DIM, (s + 1) * DN_HEAD_DIM)
        sl = slice(scw + out.start, scw + out.stop)
        v_ref[:, out] = _dn_prep_slab(old_ref, new_ref, dvw_ref, sl, out, first, last, t, None)

    ngrp = SC_WIDTH // MXU_N
    per = MXU_N // 128
    for grp in range(ngrp):
        for part in range(3):
            raw_chunk(part * ngrp + grp)
        raw_chunk(3 * ngrp + grp)
        for j in range(aw // MXU_N // ngrp):
            attn_chunk(grp * (aw // MXU_N // ngrp) + j)
            if j < per:
                sc_slab(grp * per + j)
                v_slab(grp * per + j)


def _attn_proj(x, g, w, cos_t, sin_t, sc_conv, dn_conv_v, seq, *, tm=512):
    m, d = x.shape
    nt = m // tm
    nseq = seq // tm
    aw, raww = 3 * ATTN_WIDTH, 3 * SC_WIDTH + DN_WIDTH
    cur_tile = lambda i: (jnp.minimum(i, nt - 1), 0)
    prev_tile = lambda i: (jnp.maximum(i - 1, 0), 0)
    table = lambda i: (0, jnp.minimum(i, nt - 1) % nseq, 0)
    return pl.pallas_call(
        functools.partial(_attn_proj_kernel, t=tm, seq=seq),
        grid=(nt + 1,),
        in_specs=[
            pl.BlockSpec((tm, d), cur_tile),
            _resident((1, d)),
            _resident((d, aw + raww)),
            pl.BlockSpec((2, tm, 128), table),
            pl.BlockSpec((2, tm, 128), table),
            _resident((SC_CONV, SC_WIDTH)),
            _resident((DN_CONV, DN_WIDTH)),
        ],
        out_specs=[
            pl.BlockSpec((tm, aw), cur_tile),
            pl.BlockSpec((tm, SC_WIDTH), prev_tile),
            pl.BlockSpec((tm, DN_WIDTH), prev_tile),
        ],
        out_shape=[
            jax.ShapeDtypeStruct((m, aw), F32),
            jax.ShapeDtypeStruct((m, SC_WIDTH), BF16),
            jax.ShapeDtypeStruct((m, DN_WIDTH), BF16),
        ],
        scratch_shapes=[pltpu.VMEM((tm, raww), F32), pltpu.VMEM((HALO + tm, raww), F32)],
        compiler_params=_cparams(("arbitrary",)),
        name="attn_proj",
    )(x, g, w, cos_t, sin_t, sc_conv, dn_conv_v)


def _proj_kernel(x_ref, g_ref, w_ref, wba_ref, cw_ref, gb_ref, pj_ref, ba_ref, qk_ref, new_ref, old_ref, *, t, seq):
    qkw = 2 * DN_WIDTH
    _init_tiles(new_ref, old_ref)
    first, last = _seq_edges(pl.program_id(0) - 1, t, seq)
    xn = _rms(x_ref[...], g_ref[...]).astype(BF16)
    ba_ref[...] = _dot(xn, wba_ref[...])
    nslab = qkw // DN_HEAD_DIM
    npj = PJ_W // MXU_N
    for s in range(nslab):
        if s % (MXU_N // DN_HEAD_DIM) == 0:
            cs = slice(s * DN_HEAD_DIM, s * DN_HEAD_DIM + MXU_N)
            new_ref[:, cs] = _dot(xn, w_ref[:, cs])
        for c in range(npj * s // nslab, npj * (s + 1) // nslab):
            cs = slice(c * MXU_N, (c + 1) * MXU_N)
            y = _dot(xn, w_ref[:, qkw + c * MXU_N:qkw + (c + 1) * MXU_N])
            if cs.stop <= PJ_Z:
                y = jax.nn.sigmoid(y + gb_ref[:, cs])
            else:
                y = y * jax.nn.sigmoid(y)
            pj_ref[:, cs] = y.astype(BF16)
        sl = slice(s * DN_HEAD_DIM, (s + 1) * DN_HEAD_DIM)
        scale = DN_HEAD_DIM ** -0.5 if s < DN_HEADS else 1.0
        qk_ref[:, sl] = _dn_prep_slab(old_ref, new_ref, cw_ref, sl, sl, first, last, t, scale)


def _proj(x, g, w, w_ba, dn_conv_qk, gate_bias, seq, *, tm=512):
    m, d = x.shape
    nt = m // tm
    qkw = 2 * DN_WIDTH
    cur_tile = lambda i: (jnp.minimum(i, nt - 1), 0)
    return pl.pallas_call(
        functools.partial(_proj_kernel, t=tm, seq=seq),
        grid=(nt + 1,),
        in_specs=[
            pl.BlockSpec((tm, d), cur_tile),
            _resident((1, d)),
            _resident((d, qkw + PJ_W)),
            _resident((d, BA_W)),
            _resident((DN_CONV, qkw)),
            _resident((1, N_BRANCH * d)),
        ],
        out_specs=[
            pl.BlockSpec((tm, PJ_W), cur_tile),
            pl.BlockSpec((tm, BA_W), cur_tile),
            pl.BlockSpec((tm, qkw), lambda i: (jnp.maximum(i - 1, 0), 0)),
        ],
        out_shape=[
            jax.ShapeDtypeStruct((m, PJ_W), BF16),
            jax.ShapeDtypeStruct((m, BA_W), F32),
            jax.ShapeDtypeStruct((m, qkw), BF16),
        ],
        scratch_shapes=[pltpu.VMEM((tm, qkw), F32), pltpu.VMEM((HALO + tm, qkw), F32)],
        compiler_params=_cparams(("arbitrary",)),
        name="proj",
    )(x, g, w, w_ba, dn_conv_qk, gate_bias)


def _rope_tables(seq):
    half = ROT_DIM // 2
    inv_freq = ROPE_THETA ** (-2.0 * jnp.arange(half, dtype=F32) / ROT_DIM)
    ang = jnp.arange(seq, dtype=F32)[:, None] * inv_freq[None, :]
    cos, sin = jnp.cos(ang), jnp.sin(ang)
    ones = jnp.ones((seq, HEAD_DIM - ROT_DIM), F32)
    cos_h = jnp.concatenate([cos, cos, ones, cos, cos, ones], axis=1)
    sin_h = jnp.concatenate([-sin, -sin, 0.0 * ones, sin, sin, 0.0 * ones], axis=1)
    qs = HEAD_DIM ** -0.5
    return jnp.stack([cos_h * qs, cos_h]), jnp.stack([sin_h * qs, sin_h])


def _qk_perm():
    a = jnp.arange
    half = ROT_DIM // 2
    rest = (HEAD_DIM - ROT_DIM) // 2
    lo = [a(0, half), a(HEAD_DIM, HEAD_DIM + half), a(ROT_DIM, ROT_DIM + rest),
          a(HEAD_DIM + ROT_DIM, HEAD_DIM + ROT_DIM + rest)]
    hi = [a(half, ROT_DIM), a(HEAD_DIM + half, HEAD_DIM + ROT_DIM), a(ROT_DIM + rest, HEAD_DIM),
          a(HEAD_DIM + ROT_DIM + rest, 2 * HEAD_DIM)]
    return jnp.concatenate(lo + hi)


def _qk_first_head(lane):
    half = ROT_DIM // 2
    rest = (HEAD_DIM - ROT_DIM) // 2
    l = lane % HEAD_DIM
    return (l < half) | ((l >= ROT_DIM) & (l < ROT_DIM + rest))


def _attn_kernel(q_ref, k_ref, v_ref, bias_ref, o_ref, lse_ref, kres_ref, vres_ref, *, dil, tq, tk, half, length, nsub):
    i = pl.program_id(2)

    @pl.when(i == 0)
    def _():
        for r in range(dil):
            rows = pl.ds(r, length, stride=dil) if dil > 1 else pl.ds(0, length)
            kres_ref[r * length:(r + 1) * length, :] = k_ref[0, rows, :].astype(BF16)
            vres_ref[r * length:(r + 1) * length, :] = v_ref[0, rows, :].astype(BF16)

    lane = lax.broadcasted_iota(jnp.int32, (1, 128), 1)
    first_out = lane < HEAD_DIM
    first_qk = _qk_first_head(lane)

    def group(g, carry):
        rows, bias, q, kw, vw = [], [], [], [], []
        for j in range(ATTN_UNITS):
            u = g * ATTN_UNITS + j
            r = u % dil
            sub = u // dil
            blk = i * nsub + sub
            ks = pl.multiple_of(jnp.clip(blk * tq - half, 0, length - tk), 64)
            bias.append(bias_ref[(blk * tq - ks) // half])
            if dil > 1:
                rows.append(pl.ds(r + sub * (tq * dil), tq, stride=dil))
            else:
                rows.append(pl.ds(pl.multiple_of(sub * tq, tq), tq))
            q.append(q_ref[0, rows[j], :].astype(BF16))
            kbase = pl.multiple_of(r * length + ks, 64)
            kw.append(kres_ref[pl.ds(kbase, tk), :])
            vw.append(vres_ref[pl.ds(kbase, tk), :])
        chains = [(j, sel) for j in range(ATTN_UNITS) for sel in (first_qk, ~first_qk)]
        sc = [_dot_nt(jnp.where(sel, q[j], jnp.zeros_like(q[j])), kw[j]) for j, sel in chains]
        sc = [s + bias[j] for (j, _), s in zip(chains, sc)]
        mx = [jnp.max(s, axis=-1, keepdims=True) for s in sc]
        p = [jnp.exp(s - m) for s, m in zip(sc, mx)]
        den = [jnp.sum(t, axis=-1, keepdims=True) for t in p]
        o = [_dot(t.astype(BF16), vw[j]) / d for (j, _), t, d in zip(chains, p, den)]
        lse = [m + jnp.log(d) for m, d in zip(mx, den)]
        for j in range(ATTN_UNITS):
            o_ref[0, rows[j], :] = jnp.where(first_out, o[2 * j], o[2 * j + 1])
            lse_ref[0, rows[j], :] = jnp.where(first_out, lse[2 * j], lse[2 * j + 1])
        return carry

    lax.fori_loop(0, dil * nsub // ATTN_UNITS, group, 0)


def _attn_group(at, batch, seq, gi):
    window, dil = ATTN_GROUPS[gi]
    half = (window // 2) // dil
    length = seq // dil
    tq = 2 * half
    tk = tq + 2 * half
    nsub = max(1, 4 * ATTN_UNITS // dil)
    assert half == 64 and seq % dil == 0 and tk <= length and length % (tq * nsub) == 0
    assert (dil * nsub) % ATTN_UNITS == 0
    tb = tq * dil * nsub
    at3 = at.reshape(batch, seq, 3 * ATTN_WIDTH)
    nslab = ATTN_GW // 128
    nb = ATTN_WIDTH // 128
    off = (jnp.arange(3) * half)[:, None, None]
    dist = jnp.arange(tq)[None, :, None] + off - jnp.arange(tk)[None, None, :]
    bias = jnp.where(jnp.abs(dist) <= half, 0.0, -jnp.inf).astype(F32)
    o, lse = pl.pallas_call(
        functools.partial(_attn_kernel, dil=dil, tq=tq, tk=tk, half=half, length=length, nsub=nsub),
        grid=(batch, nslab, seq // tb),
        in_specs=[
            pl.BlockSpec((1, tb, 128), lambda b, s, i: (b, i, gi * nslab + s)),
            pl.BlockSpec((1, seq, 128), lambda b, s, i: (b, 0, nb + gi * nslab + s), pipeline_mode=pl.Buffered(1)),
            pl.BlockSpec((1, seq, 128), lambda b, s, i: (b, 0, 2 * nb + gi * nslab + s), pipeline_mode=pl.Buffered(1)),
            _resident((3, tq, tk)),
        ],
        out_specs=[
            pl.BlockSpec((1, tb, 128), lambda b, s, i: (b, i, s)),
            pl.BlockSpec((1, tb, 128), lambda b, s, i: (b, i, s)),
        ],
        out_shape=[jax.ShapeDtypeStruct((batch, seq, ATTN_GW), F32)] * 2,
        scratch_shapes=[pltpu.VMEM((seq, 128), BF16), pltpu.VMEM((seq, 128), BF16)],
        compiler_params=_cparams(("parallel", "parallel", "arbitrary")),
        name=f"attn{gi}",
    )(at3, at3, at3, bias)
    return o.reshape(batch * seq, ATTN_GW), lse.reshape(batch * seq, ATTN_GW)


def _dn_kernel(q_ref, k_ref, v_ref, ba_ref, alog_ref, dtb_ref, tri_ref, *refs, ct, reverse, lane0, add_prev):
    prev_ref = refs[0] if add_prev else None
    o_ref, st_ref, beta_ref, gc_ref, gct_ref, *slot_refs = refs[1:] if add_prev else refs
    _dn_body(q_ref, k_ref, v_ref, ba_ref, alog_ref, dtb_ref, tri_ref, prev_ref, o_ref, st_ref, beta_ref, gc_ref,
             gct_ref, slot_refs, ct=ct, reverse=reverse, lane0=lane0)


def _dn_body(q_ref, k_ref, v_ref, ba_ref, alog_ref, dtb_ref, tri_ref, prev_ref, o_ref,
             st_ref, beta_ref, gc_ref, gct_ref, slot_refs, *, ct, reverse, lane0):
    c = DN_CHUNK
    nchunk = ct // c
    heads = range(DN_HEADS)
    hsl = [slice(h * DN_HEAD_DIM, (h + 1) * DN_HEAD_DIM) for h in heads]
    nslot = 2 * DN_AHEAD
    u_refs, wq_refs, upd_refs = slot_refs[:nslot], slot_refs[nslot:2 * nslot], slot_refs[2 * nslot:]

    @pl.when(pl.program_id(1) == 0)
    def _():
        st_ref[...] = jnp.zeros_like(st_ref)

    ba = ba_ref[0]
    beta_all = jax.nn.sigmoid(ba)
    xg = ba + dtb_ref[...]
    softplus = jnp.maximum(xg, 0.0) + jnp.log1p(jnp.exp(-jnp.abs(xg)))
    g_all = -jnp.exp(alog_ref[...]) * softplus
    gc_all = jnp.concatenate([_dot_exact_lhs(tri_ref[...], g_all[n * c:(n + 1) * c]) for n in range(nchunk)], axis=0)
    for h in heads:
        lb = lane0 - BA_G0 + h
        beta_ref[:, hsl[h]] = jnp.broadcast_to(beta_all[:, lb:lb + 1], (ct, DN_HEAD_DIM))
        gc_ref[:, hsl[h]] = jnp.broadcast_to(gc_all[:, lane0 + h:lane0 + h + 1], (ct, DN_HEAD_DIM))
    for n in range(nchunk):
        gct_ref[n] = gc_all[n * c:(n + 1) * c, :].T

    row = lax.broadcasted_iota(jnp.int32, (c, c), 0)
    col = lax.broadcasted_iota(jnp.int32, (c, c), 1)
    incl = (col >= row) if reverse else (col <= row)
    strict = (col > row) if reverse else (col < row)
    upper = lax.broadcasted_iota(jnp.int32, (c, 2 * c), 1) >= c
    eye_hi = (lax.broadcasted_iota(jnp.int32, (c, 2 * c), 1) - c
              == lax.broadcasted_iota(jnp.int32, (c, 2 * c), 0)).astype(F32)

    def local(chunks):
        units = [(n, slot, h) for n, slot in chunks for h in heads]
        idx = range(len(units))
        rows = [slice(n * c, (n + 1) * c) for n, _, _ in units]
        last = [n * c + (0 if reverse else c - 1) for n, _, _ in units]
        hs = [hsl[h] for _, _, h in units]
        load = lambda ref: [ref[0, rows[i], hs[i]] for i in idx]
        k16 = load(k_ref)
        gcx = [gc_ref[rows[i], hs[i]] for i in idx]
        gtot = [gc_ref[last[i]:last[i] + 1, hs[i]] for i in idx]
        grow = [gct_ref[n, lane0 + h:lane0 + h + 1, :] for n, _, h in units]
        k = [t.astype(F32) for t in k16]
        kb = [(k[i] * beta_ref[rows[i], hs[i]]).astype(BF16) for i in idx]
        kq = [_dot_nt(jnp.concatenate([kb[i], q16], axis=0), k16[i]) for i, q16 in zip(idx, load(q_ref))]
        decay = [jnp.where(incl, jnp.exp(jnp.where(incl, gcx[i][:, :c] - grow[i], 0.0)), 0.0) for i in idx]
        for i, (_, slot, h) in enumerate(units):
            upd_refs[slot][h, :c] = jnp.where(incl, kq[i][c:] * decay[i], 0.0).astype(BF16)
            upd_refs[slot][h, c:] = (k[i].T * jnp.exp(gtot[i][:, :c] - grow[i])).astype(BF16)
        mk = [jnp.where(strict, -kq[i][:c] * decay[i], 0.0) for i in idx]
        z = [jnp.concatenate([mk[i], jnp.zeros_like(mk[i])], axis=1) + eye_hi for i in idx]
        for _ in range(int(math.log2(c))):
            zb = [t.astype(BF16) for t in z]
            z = [_dot(zb[i][:, :c], zb[i]) + jnp.where(upper, z[i], 0.0) for i in idx]
        beta = [beta_ref[rows[i], hs[i]] for i in idx]
        eg = [jnp.exp(gc_ref[rows[i], hs[i]]) for i in idx]
        kbe = [t.astype(F32) * beta[i] * eg[i] for i, t in zip(idx, load(k_ref))]
        rhs = [jnp.concatenate([t.astype(F32) * beta[i], kbe[i]], axis=1).astype(BF16) for i, t in zip(idx, load(v_ref))]
        rhs = [jnp.concatenate([jnp.zeros_like(t), t], axis=0) for t in rhs]
        sol = [_dot(z[i].astype(BF16), rhs[i]) for i in idx]
        for i, ((_, slot, h), q16) in enumerate(zip(units, load(q_ref))):
            u_refs[slot][h] = sol[i][:, :DN_HEAD_DIM]
            wq_refs[slot][h, :c] = sol[i][:, DN_HEAD_DIM:].astype(BF16)
            wq_refs[slot][h, c:] = (q16.astype(F32) * eg[i]).astype(BF16)

    def recur(n, slot, state):
        rows = slice(n * c, (n + 1) * c)
        last = n * c + (0 if reverse else c - 1)
        etot = [jnp.exp(gc_ref[last:last + 1, hsl[h]]) for h in heads]
        wq = [_dot(wq_refs[slot][h], state[h].astype(BF16)) for h in heads]
        v_new = [(u_refs[slot][h] - wq[h][:c]).astype(BF16) for h in heads]
        upd = [_dot(upd_refs[slot][h], v_new[h]) for h in heads]
        for h in heads:
            o = wq[h][c:] + upd[h][:c]
            if prev_ref is not None:
                o = o + prev_ref[0, rows, hsl[h]].astype(F32)
            o_ref[0, rows, hsl[h]] = o.astype(o_ref.dtype)
        return [state[h] * etot[h] + upd[h][c:] for h in heads]

    order = list(range(nchunk))[::-1] if reverse else list(range(nchunk))
    groups = [order[i:i + DN_AHEAD] for i in range(0, nchunk, DN_AHEAD)]
    slots = lambda gi: [(n, (gi % 2) * DN_AHEAD + j) for j, n in enumerate(groups[gi])]
    state = [st_ref[h] for h in heads]
    local(slots(0))
    for gi in range(len(groups)):
        for n, slot in slots(gi):
            state = recur(n, slot, state)
        if gi + 1 < len(groups):
            local(slots(gi + 1))
    for h in heads:
        st_ref[h] = state[h]


def _dn_tri(reverse):
    r = jnp.arange(DN_CHUNK)
    tri = (r[None, :] >= r[:, None]) if reverse else (r[None, :] <= r[:, None])
    return tri.astype(BF16)


def _deltanet(qk, v, ba, alog_row, dtb_row, batch, seq, direction, prev=None, *, ct=512):
    reverse = direction == 1
    nstep = seq // ct
    w = DN_WIDTH
    c = DN_CHUNK
    qk3 = qk.reshape(batch, seq, 2 * w)
    v3 = v.reshape(batch, seq, w)
    ba3 = ba.reshape(batch, seq, BA_W)
    step = (lambda n: nstep - 1 - n) if reverse else (lambda n: n)
    out = pl.pallas_call(
        functools.partial(_dn_kernel, ct=ct, reverse=reverse, lane0=BA_G0 + direction * DN_HEADS,
                          add_prev=prev is not None),
        grid=(batch, nstep),
        in_specs=[
            pl.BlockSpec((1, ct, w), lambda b, n: (b, step(n), 0)),
            pl.BlockSpec((1, ct, w), lambda b, n: (b, step(n), 1)),
            pl.BlockSpec((1, ct, w), lambda b, n: (b, step(n), 0)),
            pl.BlockSpec((1, ct, BA_W), lambda b, n: (b, step(n), 0)),
            _resident((1, BA_W)),
            _resident((1, BA_W)),
            _resident((c, c)),
        ] + ([] if prev is None else [pl.BlockSpec((1, ct, w), lambda b, n: (b, step(n), 0))]),
        out_specs=pl.BlockSpec((1, ct, w), lambda b, n: (b, step(n), 0)),
        out_shape=jax.ShapeDtypeStruct((batch, seq, w), BF16),
        scratch_shapes=[
            pltpu.VMEM((DN_HEADS, DN_HEAD_DIM, DN_HEAD_DIM), F32),
            pltpu.VMEM((ct, w), F32),
            pltpu.VMEM((ct, w), F32),
            pltpu.VMEM((ct // c, BA_W, c), F32),
        ]
        + [pltpu.VMEM((DN_HEADS, c, DN_HEAD_DIM), F32)] * (2 * DN_AHEAD)
        + [pltpu.VMEM((DN_HEADS, 2 * c, DN_HEAD_DIM), BF16)] * (2 * DN_AHEAD)
        + [pltpu.VMEM((DN_HEADS, c + DN_HEAD_DIM, c), BF16)] * (2 * DN_AHEAD),
        compiler_params=_cparams(("parallel", "arbitrary")),
        name=f"deltanet{direction}",
    )(qk3, qk3, v3, ba3, alog_row, dtb_row, _dn_tri(reverse),
      *([] if prev is None else [prev.reshape(batch, seq, w)]))
    return out.reshape(batch * seq, w)


def _merge_kernel(x_ref, o1_ref, o2_ref, o3_ref, l1_ref, l2_ref, l3_ref, dn_ref, z_ref, sc_ref, gate_ref,
                  dng_ref, wa_ref, wd_ref, ws_ref, wo_ref, out_ref):
    l1, l2, l3 = l1_ref[...], l2_ref[...], l3_ref[...]
    mx = jnp.maximum(jnp.maximum(l1, l2), l3)
    e1, e2, e3 = jnp.exp(l1 - mx), jnp.exp(l2 - mx), jnp.exp(l3 - mx)
    attn = (o1_ref[...] * e1 + o2_ref[...] * e2 + o3_ref[...] * e3) / (e1 + e2 + e3)
    y_attn = _dot(attn.astype(BF16), wa_ref[...])
    dn = dn_ref[...].astype(F32)
    dng = dng_ref[...]
    parts = []
    for h in range(DN_HEADS):
        hs = slice(h * DN_HEAD_DIM, (h + 1) * DN_HEAD_DIM)
        oh = dn[:, hs]
        oh = oh * lax.rsqrt(jnp.mean(oh * oh, axis=-1, keepdims=True) + NORM_EPS) * dng
        parts.append((oh * z_ref[:, hs].astype(F32)).astype(BF16))
    y_dn = _dot(jnp.concatenate(parts, axis=1), wd_ref[...])
    y_sc = _dot(sc_ref[...], ws_ref[...])
    gates = gate_ref[...].astype(F32)
    d = D_MODEL
    merged = gates[:, :d] * y_attn + gates[:, d:2 * d] * y_dn + gates[:, 2 * d:] * y_sc
    out_ref[...] = x_ref[...] + _dot(merged.astype(BF16), wo_ref[...])


def _merge(x, attn_o, attn_l, dn, pj, sc, dn_norm, wa, wd, ws, wo, *, tm=512):
    m, d = x.shape
    row = lambda i: (i, 0)
    gw = N_BRANCH * d
    return pl.pallas_call(
        _merge_kernel,
        grid=(m // tm,),
        in_specs=[pl.BlockSpec((tm, d), row)]
        + [pl.BlockSpec((tm, ATTN_GW), row)] * 6
        + [pl.BlockSpec((tm, DN_WIDTH), row)]
        + [
            pl.BlockSpec((tm, DN_WIDTH), lambda i: (i, PJ_Z // DN_WIDTH)),
            pl.BlockSpec((tm, SC_WIDTH), row),
            pl.BlockSpec((tm, gw), lambda i: (i, PJ_GATE // gw)),
            _resident((1, DN_HEAD_DIM)),
            _resident((ATTN_GW, d)),
            _resident((DN_WIDTH, d)),
            _resident((SC_WIDTH, d)),
            _resident((d, d)),
        ],
        out_specs=pl.BlockSpec((tm, d), row),
        out_shape=jax.ShapeDtypeStruct((m, d), F32),
        compiler_params=_cparams(("parallel",)),
        name="merge",
    )(x, *attn_o, *attn_l, dn, pj, sc, pj, dn_norm, wa, wd, ws, wo)


def _prep_layer(l, p):
    w_in = p["w_in"][l]
    o_attn, o_dn = 0, 3 * ATTN_WIDTH
    o_dnv = o_dn + 2 * DN_WIDTH
    o_z = o_dn + 3 * DN_WIDTH
    o_beta = o_z + DN_WIDTH
    o_sc = o_beta + 4 * DN_HEADS
    o_gate = o_sc + 3 * SC_WIDTH
    bf = lambda t: t.astype(BF16)
    w_qk = w_in[:, o_attn:o_attn + 2 * ATTN_WIDTH].reshape(D_MODEL, -1, 128)[:, :, _qk_perm()]
    w_attn = bf(jnp.concatenate([w_qk.reshape(D_MODEL, -1), w_in[:, o_attn + 2 * ATTN_WIDTH:o_dn], w_in[:, o_sc:o_gate],
                                 w_in[:, o_dnv:o_z]], axis=1))
    w_proj = bf(jnp.concatenate([w_in[:, o_dn:o_dnv], w_in[:, o_gate:], w_in[:, o_z:o_beta]], axis=1))
    w_ba = jnp.pad(w_in[:, o_beta:o_sc], ((0, 0), (0, BA_W - 4 * DN_HEADS)))
    pad_row = lambda t: jnp.pad(t.reshape(1, 2 * DN_HEADS), ((0, 0), (BA_G0, BA_W - BA_G0 - 2 * DN_HEADS)))
    row = lambda t: t.reshape(1, -1)
    return dict(
        ffn1=(row(p["ffn1_norm"][l]), bf(p["ffn1_w_gate"][l]), bf(p["ffn1_w_up"][l]), bf(p["ffn1_w_down"][l])),
        ffn2=(row(p["ffn2_norm"][l]), bf(p["ffn2_w_gate"][l]), bf(p["ffn2_w_up"][l]), bf(p["ffn2_w_down"][l])),
        mix_norm=row(p["mix_norm"][l]),
        w_attn=w_attn,
        w_proj=w_proj,
        w_ba=bf(w_ba),
        dn_conv_qk=p["dn_conv"][l][:, :2 * DN_WIDTH],
        dn_conv_v=p["dn_conv"][l][:, 2 * DN_WIDTH:],
        alog=pad_row(p["dn_a_log"][l]),
        dtb=pad_row(p["dn_dt_bias"][l]),
        dn_norm=row(p["dn_norm"][l]),
        sc_conv=p["sc_conv"][l],
        gate_bias=row(p["gate_bias"][l]),
        wa=bf(p["w_attn_br"][l]),
        wd=bf(p["w_dn_br"][l]),
        ws=bf(p["w_sc_br"][l]),
        wo=bf(p["w_out"][l]),
    )


def _mixer(x, lp, batch, seq, tables):
    at, sc, dn_v = _attn_proj(x, lp["mix_norm"], lp["w_attn"], tables[0], tables[1], lp["sc_conv"], lp["dn_conv_v"], seq)
    pj, ba, dn_qk = _proj(x, lp["mix_norm"], lp["w_proj"], lp["w_ba"], lp["dn_conv_qk"], lp["gate_bias"], seq)
    attn = [_attn_group(at, batch, seq, gi) for gi in range(len(ATTN_GROUPS))]
    dn = _deltanet(dn_qk, dn_v, ba, lp["alog"], lp["dtb"], batch, seq, 0)
    dn = _deltanet(dn_qk, dn_v, ba, lp["alog"], lp["dtb"], batch, seq, 1, prev=dn)
    return _merge(x, [a[0] for a in attn], [a[1] for a in attn], dn, pj, sc,
                  lp["dn_norm"], lp["wa"], lp["wd"], lp["ws"], lp["wo"])


def _trunk(x, layers, final_g):
    batch, seq, d = x.shape
    tables = _rope_tables(seq)
    x = x.reshape(batch * seq, d)
    for l, lp in enumerate(layers):
        x = _ffn(x, *lp["ffn1"])
        x = _mixer(x, lp, batch, seq, tables)
        x = _ffn(x, *lp["ffn2"], final_g=final_g if l == len(layers) - 1 else None)
    return x.reshape(batch, seq, d)


def kernel(x_prompt, x_sample, ffn1_norm, ffn1_w_gate, ffn1_w_up, ffn1_w_down, mix_norm, w_in, dn_conv, dn_a_log, dn_dt_bias, dn_norm, sc_conv, gate_bias, w_attn_br, w_dn_br, w_sc_br, w_out, ffn2_norm, ffn2_w_gate, ffn2_w_up, ffn2_w_down, final_norm):
    p = dict(ffn1_norm=ffn1_norm, ffn1_w_gate=ffn1_w_gate, ffn1_w_up=ffn1_w_up, ffn1_w_down=ffn1_w_down,
             mix_norm=mix_norm, w_in=w_in, dn_conv=dn_conv, dn_a_log=dn_a_log, dn_dt_bias=dn_dt_bias,
             dn_norm=dn_norm, sc_conv=sc_conv, gate_bias=gate_bias, w_attn_br=w_attn_br, w_dn_br=w_dn_br,
             w_sc_br=w_sc_br, w_out=w_out, ffn2_norm=ffn2_norm, ffn2_w_gate=ffn2_w_gate, ffn2_w_up=ffn2_w_up,
             ffn2_w_down=ffn2_w_down)
    layers = [_prep_layer(l, p) for l in range(ffn1_norm.shape[0])]
    final_g = final_norm.reshape(1, -1)
    return _trunk(x_prompt, layers, final_g), _trunk(x_sample, layers, final_g)
```
